```python
import jax, jax.numpy as jnp
from jax import lax
import numpy as np

D_MODEL = 1024
BATCH = 8
SEQ = 4096
DEPTH = 1

HEAD_DIM = 64
MOBA_HEADS = 8
RET_HEADS = 4
MEM_HEADS = 4
MOBA_WIDTH = MOBA_HEADS * HEAD_DIM
RET_WIDTH = RET_HEADS * HEAD_DIM
MEM_WIDTH = MEM_HEADS * HEAD_DIM
MIX_WIDTH = MOBA_WIDTH + RET_WIDTH + MEM_WIDTH
IN_WIDTH = 3 * MOBA_WIDTH + 4 * RET_WIDTH + MEM_WIDTH
MOBA_BLOCK = 256
MOBA_TOPK = 3
MOBA_QCHUNK = 32
RET_CHUNK = 128
RET_ROPE_BASE = 10000.0
MEM_LEN = 256
N_EXPERTS = 32
TOP_K = 4
D_FF = D_MODEL
SWIGLU_ALPHA = 1.702
SWIGLU_LIMIT = 7.0
MOE_BLOCK = 512
EPS = 1e-6
NEG = -1e30

kernel_name = 'hybrid_moba_retention_memory_moe'


def rms_norm(x, gain):
    xf = x.astype(jnp.float32)
    y = xf * lax.rsqrt(jnp.mean(xf * xf, axis=-1, keepdims=True) + EPS)
    return (y * gain.astype(jnp.float32)).astype(x.dtype)


def split_heads(t, n_heads):
    b, s, _ = t.shape
    return t.reshape(b, s, n_heads, HEAD_DIM).transpose(0, 2, 1, 3)


def merge_heads(t):
    b, h, s, d = t.shape
    return t.transpose(0, 2, 1, 3).reshape(b, s, h * d)


def rotate(t, cos, sin):
    t1, t2 = jnp.split(t, 2, axis=-1)
    return jnp.concatenate([t1 * cos - t2 * sin, t2 * cos + t1 * sin], axis=-1)


def moba_attention(q, k, v):
    b, nh, s, dh = q.shape
    n_blk = -(-s // MOBA_BLOCK)
    sp = n_blk * MOBA_BLOCK
    pad = ((0, 0), (0, 0), (0, sp - s), (0, 0))
    q, k, v = (jnp.pad(t.astype(jnp.float32), pad) for t in (q, k, v))
    scale = dh ** -0.5
    qb = q.reshape(b, nh, n_blk, MOBA_BLOCK, dh)
    kb = k.reshape(b, nh, n_blk, MOBA_BLOCK, dh)
    vb = v.reshape(b, nh, n_blk, MOBA_BLOCK, dh)
    causal = jnp.tril(jnp.ones((MOBA_BLOCK, MOBA_BLOCK), dtype=bool))
    s_own = jnp.where(causal, jnp.einsum('bhnqd,bhnkd->bhnqk', qb, kb) * scale, NEG)
    lse_own = jax.nn.logsumexp(s_own, axis=-1)
    o_own = jnp.einsum('bhnqk,bhnkd->bhnqd', jnp.exp(s_own - lse_own[..., None]), vb)
    o_own = o_own.reshape(b, nh, sp, dh)
    lse_own = lse_own.reshape(b, nh, sp)
    k_mean = kb.mean(axis=3)
    own_blk = jnp.arange(sp) // MOBA_BLOCK
    past = jnp.arange(n_blk)[None, :] < own_blk[:, None]
    gate = jnp.where(past, jnp.einsum('bhsd,bhnd->bhsn', q, k_mean), NEG)
    n_sel = min(MOBA_TOPK, n_blk)
    _, sel = lax.top_k(gate, n_sel)
    valid = sel < own_blk[:, None]
    n_qc = sp // MOBA_QCHUNK

    def to_chunks(t):
        return jnp.moveaxis(t.reshape(b, nh, n_qc, MOBA_QCHUNK, *t.shape[3:]), 2, 0)

    bidx = jnp.arange(b)[:, None, None, None]
    hidx = jnp.arange(nh)[None, :, None, None]

    def chunk(args):
        qq, ii, vv = args
        ks = kb[bidx, hidx, ii]
        vs = vb[bidx, hidx, ii]
        sc = jnp.einsum('bhqd,bhqjkd->bhqjk', qq, ks) * scale
        sc = jnp.where(vv[..., None], sc, NEG).reshape(b, nh, MOBA_QCHUNK, n_sel * MOBA_BLOCK)
        lse = jax.nn.logsumexp(sc, axis=-1)
        p = jnp.exp(sc - lse[..., None])
        o = jnp.einsum('bhqm,bhqmd->bhqd', p, vs.reshape(b, nh, MOBA_QCHUNK, n_sel * MOBA_BLOCK, dh))
        return o, lse

    o_past, lse_past = lax.map(chunk, (to_chunks(q), to_chunks(sel), to_chunks(valid)))
    o_past = jnp.moveaxis(o_past, 0, 2).reshape(b, nh, sp, dh)
    lse_past = jnp.moveaxis(lse_past, 0, 2).reshape(b, nh, sp)
    m = jnp.maximum(lse_own, lse_past)
    w_own = jnp.exp(lse_own - m)[..., None]
    w_past = jnp.exp(lse_past - m)[..., None]
    o = (w_own * o_own + w_past * o_past) / (w_own + w_past)
    return o[:, :, :s]


def retention(q, k, v, g, out_gain):
    b, s, _ = q.shape
    dh = HEAD_DIM
    q, k, v = (split_heads(t.astype(jnp.float32), RET_HEADS) for t in (q, k, v))
    inv_freq = 1.0 / (RET_ROPE_BASE ** jnp.linspace(0.0, 1.0, dh // 2, dtype=jnp.float32))
    ang = jnp.arange(s, dtype=jnp.float32)[:, None] * inv_freq[None, :]
    cos, sin = jnp.cos(ang), jnp.sin(ang)
    q = rotate(q, cos, sin)
    k = rotate(k, cos, sin) * dh ** -0.5
    log_gamma = jnp.log1p(-jnp.exp2(-5.0 - jnp.arange(RET_HEADS, dtype=jnp.float32)))
    n_ch = s // RET_CHUNK
    qc, kc, vc = (t.reshape(b, RET_HEADS, n_ch, RET_CHUNK, dh) for t in (q, k, v))
    idx = jnp.arange(RET_CHUNK, dtype=jnp.float32)
    diff = idx[:, None] - idx[None, :]
    intra = jnp.where(diff >= 0, jnp.exp(log_gamma[:, None, None] * jnp.maximum(diff, 0.0)), 0.0)
    scores = jnp.einsum('bhnid,bhnjd->bhnij', qc, kc) * intra[None, :, None]
    o_intra = jnp.einsum('bhnij,bhnje->bhnie', scores, vc)
    k_decay = jnp.exp(log_gamma[:, None] * (RET_CHUNK - 1.0 - idx)[None, :])
    chunk_kv = jnp.einsum('bhnjd,bhnje->bhnde', kc * k_decay[None, :, None, :, None], vc)
    chunk_decay = jnp.exp(log_gamma * RET_CHUNK)[None, :, None, None]

    def step(state, kv):
        return chunk_decay * state + kv, state

    _, states = lax.scan(step, jnp.zeros((b, RET_HEADS, dh, dh), jnp.float32), jnp.moveaxis(chunk_kv, 2, 0))
    states = jnp.moveaxis(states, 0, 2)
    q_decay = jnp.exp(log_gamma[:, None] * (idx + 1.0)[None, :])
    o_cross = jnp.einsum('bhnid,bhnde->bhnie', qc, states) * q_decay[None, :, None, :, None]
    o = (o_intra + o_cross).reshape(b, RET_HEADS, s, dh)
    o = o * lax.rsqrt(jnp.mean(o * o, axis=-1, keepdims=True) + EPS)
    o = merge_heads(o) * out_gain.astype(jnp.float32)
    return jax.nn.silu(g.astype(jnp.float32)) * o


def memory_attention(q, mem_h, w_mem_kv, q_gain, k_gain):
    kv = jnp.einsum('bmd,de->bme', mem_h, w_mem_kv)
    k_m, v_m = jnp.split(kv, 2, axis=-1)
    q = rms_norm(split_heads(q, MEM_HEADS), q_gain).astype(jnp.float32)
    k_m = rms_norm(split_heads(k_m, MEM_HEADS), k_gain).astype(jnp.float32)
    v_m = split_heads(v_m, MEM_HEADS).astype(jnp.float32)
    p = jax.nn.softmax(jnp.einsum('bhsd,bhmd->bhsm', q, k_m) * HEAD_DIM ** -0.5, axis=-1)
    return merge_heads(jnp.einsum('bhsm,bhmd->bhsd', p, v_m))


def mixer_block(x, mem, g_attn, w_in, moba_q_gain, moba_k_gain, ret_out_gain, g_mem, w_mem_kv,
                mem_q_gain, mem_k_gain, w_out):
    h = rms_norm(x, g_attn)
    proj = jnp.einsum('bsd,de->bse', h, w_in)
    sizes = [MOBA_WIDTH] * 3 + [RET_WIDTH] * 4 + [MEM_WIDTH]
    cuts = [sum(sizes[:i + 1]) for i in range(len(sizes) - 1)]
    qa, ka, va, qr, kr, vr, gr, qm = jnp.split(proj, cuts, axis=-1)
    qa = rms_norm(split_heads(qa, MOBA_HEADS), moba_q_gain)
    ka = rms_norm(split_heads(ka, MOBA_HEADS), moba_k_gain)
    oa = merge_heads(moba_attention(qa, ka, split_heads(va, MOBA_HEADS)))
    orr = retention(qr, kr, vr, gr, ret_out_gain)
    om = memory_attention(qm, rms_norm(mem, g_mem), w_mem_kv, mem_q_gain, mem_k_gain)
    y = jnp.concatenate([oa, orr, om], axis=-1).astype(x.dtype)
    return x + jnp.einsum('bse,ed->bsd', y, w_out)


def clamped_swiglu(z):
    x_glu = jnp.minimum(z[..., ::2], SWIGLU_LIMIT)
    x_lin = jnp.clip(z[..., 1::2], -SWIGLU_LIMIT, SWIGLU_LIMIT)
    return x_glu * jax.nn.sigmoid(SWIGLU_ALPHA * x_glu) * (x_lin + 1.0)


def routed_experts(h, expert_idx, expert_w, w1, b1, w2, b2):
    t, d = h.shape
    a = t * TOP_K
    e_flat = expert_idx.reshape(a)
    tok = jnp.arange(a, dtype=jnp.int32) // TOP_K
    order = jnp.argsort(e_flat)
    e_sorted = e_flat[order]
    counts = jnp.bincount(e_flat, length=N_EXPERTS)
    padded = ((counts + MOE_BLOCK - 1) // MOE_BLOCK) * MOE_BLOCK
    starts = jnp.cumsum(counts) - counts
    pends = jnp.cumsum(padded)
    pstarts = pends - padded
    dest = pstarts[e_sorted] + (jnp.arange(a, dtype=jnp.int32) - starts[e_sorted])
    n_blocks = -(-(a + N_EXPERTS * (MOE_BLOCK - 1)) // MOE_BLOCK)
    p_rows = n_blocks * MOE_BLOCK
    row_tok = jnp.zeros((p_rows,), jnp.int32).at[dest].set(tok[order])
    row_w = jnp.zeros((p_rows,), jnp.float32).at[dest].set(expert_w.reshape(a)[order])
    blk_start = jnp.arange(n_blocks, dtype=jnp.int32) * MOE_BLOCK
    blk_exp = jnp.minimum(jnp.searchsorted(pends, blk_start, side='right'), N_EXPERTS - 1)

    def block(args):
        toks, wts, e = args
        xb = jnp.take(h, toks, axis=0)
        z = xb @ w1[e] + b1[e]
        y = clamped_swiglu(z) @ w2[e] + b2[e]
        return (y.astype(jnp.float32) * wts[:, None]).astype(h.dtype)

    ys = lax.map(block, (row_tok.reshape(n_blocks, MOE_BLOCK), row_w.reshape(n_blocks, MOE_BLOCK), blk_exp))
    return jnp.zeros((t, d), h.dtype).at[row_tok].add(ys.reshape(p_rows, d))


def moe_block(x, g_ffn, w_router, b_router, w1, b1, w2, b2):
    b, s, d = x.shape
    h = rms_norm(x, g_ffn).reshape(b * s, d)
    logits = (h @ w_router).astype(jnp.float32) + b_router.astype(jnp.float32)
    top_vals, top_idx = lax.top_k(logits, TOP_K)
    weights = jax.nn.softmax(top_vals, axis=-1)
    out = routed_experts(h, top_idx, weights, w1, b1, w2, b2)
    return x + out.reshape(b, s, d)


def setup_inputs(seed: int = 0) -> dict:
    key = jax.random.key(seed)
    ks = jax.random.split(key, 20)
    f32 = jnp.float32

    def nrm(k, shape, scale):
        return jax.random.normal(k, shape, f32) * scale

    def gain(k, shape):
        return 1.0 + 0.02 * jax.random.normal(k, shape, f32)

    L = DEPTH
    return {
        'x': nrm(ks[0], (BATCH, SEQ, D_MODEL), 1.0),
        'mem': nrm(ks[1], (BATCH, MEM_LEN, D_MODEL), 1.0),
        'g_attn': gain(ks[2], (L, D_MODEL)),
        'w_in': nrm(ks[3], (L, D_MODEL, IN_WIDTH), D_MODEL ** -0.5),
        'moba_q_gain': gain(ks[4], (L, HEAD_DIM)),
        'moba_k_gain': gain(ks[5], (L, HEAD_DIM)),
        'ret_out_gain': gain(ks[6], (L, RET_WIDTH)),
        'g_mem': gain(ks[7], (L, D_MODEL)),
        'w_mem_kv': nrm(ks[8], (L, D_MODEL, 2 * MEM_WIDTH), D_MODEL ** -0.5),
        'mem_q_gain': gain(ks[9], (L, HEAD_DIM)),
        'mem_k_gain': gain(ks[10], (L, HEAD_DIM)),
        'w_out': nrm(ks[11], (L, MIX_WIDTH, D_MODEL), MIX_WIDTH ** -0.5),
        'g_ffn': gain(ks[12], (L, D_MODEL)),
        'w_router': nrm(ks[13], (L, D_MODEL, N_EXPERTS), D_MODEL ** -0.5),
        'b_router': nrm(ks[14], (L, N_EXPERTS), 0.01),
        'w1': nrm(ks[15], (L, N_EXPERTS, D_MODEL, 2 * D_FF), D_MODEL ** -0.5),
        'b1': nrm(ks[16], (L, N_EXPERTS, 2 * D_FF), 0.01),
        'w2': nrm(ks[17], (L, N_EXPERTS, D_FF, D_MODEL), D_FF ** -0.5),
        'b2': nrm(ks[18], (L, N_EXPERTS, D_MODEL), 0.01),
    }


def reference(x, mem, g_attn, w_in, moba_q_gain, moba_k_gain, ret_out_gain, g_mem, w_mem_kv,
              mem_q_gain, mem_k_gain, w_out, g_ffn, w_router, b_router, w1, b1, w2, b2):
    for l in range(DEPTH):
        x = mixer_block(x, mem, g_attn[l], w_in[l], moba_q_gain[l], moba_k_gain[l], ret_out_gain[l],
                        g_mem[l], w_mem_kv[l], mem_q_gain[l], mem_k_gain[l], w_out[l])
        x = moe_block(x, g_ffn[l], w_router[l], b_router[l], w1[l], b1[l], w2[l], b2[l])
    return x
```

```python
import functools

import jax
import jax.numpy as jnp
from jax import lax
from jax.experimental import pallas as pl
from jax.experimental.pallas import tpu as pltpu

F32 = jnp.float32
BF16 = jnp.bfloat16

LANES = 128
HEAD_DIM = 64
MOBA_HEADS = 8
RET_HEADS = 4
MEM_HEADS = 4
MOBA_WIDTH = MOBA_HEADS * HEAD_DIM
RET_WIDTH = RET_HEADS * HEAD_DIM
MEM_WIDTH = MEM_HEADS * HEAD_DIM
IN_WIDTH = 3 * MOBA_WIDTH + 4 * RET_WIDTH + MEM_WIDTH
MOBA_BLOCK = 256
MOBA_TOPK = 3
RET_CHUNK = 128
RET_ROPE_BASE = 10000.0
N_EXPERTS = 32
TOP_K = 4
SWIGLU_ALPHA = 1.702
SWIGLU_LIMIT = 7.0
MOE_BLOCK = 512
EPS = 1e-6
NEG = -1e30
QK_SCALE = HEAD_DIM ** -0.5

VMEM_LIMIT = 48 * 1024 * 1024

_NT = (((1,), (1,)), ((), ()))
_TN = (((0,), (0,)), ((), ()))


def _params(*sem):
    return pltpu.CompilerParams(dimension_semantics=sem, vmem_limit_bytes=VMEM_LIMIT)


def _lane_iota(shape):
    return lax.broadcasted_iota(jnp.int32, shape, len(shape) - 1)


def _pair_rms(t, gain, lo):
    t2 = t * t
    s0 = jnp.sum(jnp.where(lo, t2, 0.0), axis=-1, keepdims=True)
    s1 = jnp.sum(jnp.where(lo, 0.0, t2), axis=-1, keepdims=True)
    r = jnp.where(lo, lax.rsqrt(s0 / HEAD_DIM + EPS), lax.rsqrt(s1 / HEAD_DIM + EPS))
    return t * r * gain


def _rms_proj_kernel(x_ref, g_ref, w_ref, o_ref):
    x = x_ref[...]
    ms = jnp.mean(x * x, axis=-1, keepdims=True)
    h = (x * lax.rsqrt(ms + EPS) * g_ref[...]).astype(BF16)
    o_ref[...] = jnp.dot(h, w_ref[...], preferred_element_type=F32)


def _rms_proj(x2d, gain, w, tm):
    t, d = x2d.shape
    n = w.shape[1]
    return pl.pallas_call(
        _rms_proj_kernel,
        grid=(t // tm,),
        in_specs=[pl.BlockSpec((tm, d), lambda i: (i, 0)),
                  pl.BlockSpec((1, d), lambda i: (0, 0)),
                  pl.BlockSpec((d, n), lambda i: (0, 0))],
        out_specs=pl.BlockSpec((tm, n), lambda i: (i, 0)),
        out_shape=jax.ShapeDtypeStruct((t, n), F32),
        compiler_params=_params("parallel"),
    )(x2d, gain.reshape(1, d), w)


def _moba_kernel(q_ref, k_ref, v_ref, qg_ref, kg_ref, o_ref, ka_ref, vb_ref, gt_ref, *, n_blk):
    blk = MOBA_BLOCK
    lane = _lane_iota((blk, LANES))
    lo = lane < HEAD_DIM
    col = jnp.where(lo, lane, lane - HEAD_DIM)
    col_f = col.astype(F32)
    lane1 = _lane_iota((1, LANES))
    qg = qg_ref[...]
    kg = kg_ref[...]
    tri = (lax.broadcasted_iota(jnp.int32, (blk, blk), 0)
           >= lax.broadcasted_iota(jnp.int32, (blk, blk), 1))

    def rows_of(j):
        return pl.ds(pl.multiple_of(j * blk, blk), blk)

    gt_ref[...] = jnp.zeros_like(gt_ref)

    def prep(j, carry):
        rows = rows_of(j)
        kn = _pair_rms(k_ref[rows, :], kg, lo)
        km = jnp.mean(kn, axis=0, keepdims=True)
        gt_ref[pl.ds(HEAD_DIM + j, 1), :] = jnp.where(lane1 < HEAD_DIM, km, 0.0)
        gt_ref[pl.ds(j, 1), :] = jnp.where(lane1 < HEAD_DIM, 0.0, km)
        one0 = jnp.where(lane == HEAD_DIM + j, 1.0, 0.0)
        one1 = jnp.where(lane == j, 1.0, 0.0)
        ka_ref[0, rows, :] = jnp.where(lo, kn, one0).astype(BF16)
        ka_ref[1, rows, :] = jnp.where(lo, one1, kn).astype(BF16)
        vb_ref[rows, :] = v_ref[rows, :].astype(BF16)
        return carry

    lax.fori_loop(0, n_blk, prep, 0)

    def seg_max(t):
        a = jnp.max(jnp.where(lo, t, -jnp.inf), axis=-1, keepdims=True)
        b = jnp.max(jnp.where(lo, -jnp.inf, t), axis=-1, keepdims=True)
        return jnp.where(lo, a, b)

    def seg_min(t):
        a = jnp.min(jnp.where(lo, t, jnp.inf), axis=-1, keepdims=True)
        b = jnp.min(jnp.where(lo, jnp.inf, t), axis=-1, keepdims=True)
        return jnp.where(lo, a, b)

    def qblock(n, carry):
        rows_n = rows_of(n)
        qn = _pair_rms(q_ref[rows_n, :], qg, lo)
        gate = lax.dot_general(qn.astype(BF16), gt_ref[...].astype(BF16), _NT,
                               preferred_element_type=F32)
        valid = col < n
        g = jnp.where(valid, gate, NEG)
        sel = jnp.zeros((blk, LANES), jnp.bool_)
        for _ in range(MOBA_TOPK):
            m = seg_max(g)
            first = seg_min(jnp.where(g == m, col_f, jnp.inf))
            pick = col_f == first
            sel = jnp.logical_or(sel, pick)
            g = jnp.where(pick, -jnp.inf, g)
        keep = jnp.logical_or(jnp.logical_and(sel, valid), col == n)
        bias = jnp.where(jnp.logical_or(keep, col >= n_blk), 0.0, NEG)
        qs = qn * QK_SCALE
        qa = (jnp.where(lo, qs, bias).astype(BF16), jnp.where(lo, bias, qs).astype(BF16))

        def tile(h, j, causal):
            s = lax.dot_general(qa[h], ka_ref[h, rows_of(j), :], _NT, preferred_element_type=F32)
            return jnp.where(tri, s, NEG) if causal else s

        state = []
        for h in range(2):
            s = tile(h, n, True)
            m = jnp.max(s, axis=-1, keepdims=True)
            p = jnp.exp(s - m)
            l = jnp.sum(p, axis=-1, keepdims=True)
            acc = jnp.dot(p.astype(BF16), vb_ref[rows_n, :], preferred_element_type=F32)
            state += [m, l, acc]

        def past(j, st):
            out = []
            vj = vb_ref[rows_of(j), :]
            for h in range(2):
                m, l, acc = st[3 * h:3 * h + 3]
                s = tile(h, j, False)
                m_new = jnp.maximum(m, jnp.max(s, axis=-1, keepdims=True))
                alpha = jnp.exp(m - m_new)
                p = jnp.exp(s - m_new)
                l = alpha * l + jnp.sum(p, axis=-1, keepdims=True)
                acc = alpha * acc + jnp.dot(p.astype(BF16), vj, preferred_element_type=F32)
                out += [m_new, l, acc]
            return tuple(out)

        st = lax.fori_loop(0, n, past, tuple(state))
        o_ref[rows_n, :] = jnp.where(lo, st[2] / st[1], st[5] / st[4])
        return carry

    lax.fori_loop(0, n_blk, qblock, 0)


def _moba(proj, q_gain, k_gain, batch, seq):
    n_blk = seq // MOBA_BLOCK
    assert seq % MOBA_BLOCK == 0 and n_blk <= HEAD_DIM and n_blk >= MOBA_TOPK
    n_pairs = MOBA_WIDTH // LANES
    qg = jnp.tile(q_gain, 2).reshape(1, LANES)
    kg = jnp.tile(k_gain, 2).reshape(1, LANES)
    blk = lambda off: pl.BlockSpec((seq, LANES), lambda b, p: (b, off + p))
    vec = pl.BlockSpec((1, LANES), lambda b, p: (0, 0))
    return pl.pallas_call(
        functools.partial(_moba_kernel, n_blk=n_blk),
        grid=(batch, n_pairs),
        in_specs=[blk(0), blk(n_pairs), blk(2 * n_pairs), vec, vec],
        out_specs=pl.BlockSpec((seq, LANES), lambda b, p: (b, p)),
        out_shape=jax.ShapeDtypeStruct((batch * seq, MOBA_WIDTH), F32),
        scratch_shapes=[pltpu.VMEM((2, seq, LANES), BF16),
                        pltpu.VMEM((seq, LANES), BF16),
                        pltpu.VMEM((LANES, LANES), F32)],
        compiler_params=_params("parallel", "parallel"),
    )(proj, proj, proj, qg, kg)


def _ret_kernel(q_ref, k_ref, v_ref, g_ref, cos_ref, sin_ref, intra_ref, kdec_ref, qdec_ref,
                cdm_ref, mask_ref, gain_ref, o_ref, *, n_ch):
    c = RET_CHUNK
    lane = _lane_iota((c, LANES))
    lo = lane < HEAD_DIM
    first_half = (lane % HEAD_DIM) < (HEAD_DIM // 2)
    gain = gain_ref[...]
    intra0 = intra_ref[0, 0]
    intra1 = intra_ref[0, 1]
    kdec = kdec_ref[0]
    qdec = qdec_ref[0]
    cdm = cdm_ref[0]
    mask = mask_ref[...]

    def rope(t, cos_t, sin_t):
        partner = jnp.where(first_half, pltpu.roll(t, LANES - HEAD_DIM // 2, 1),
                            pltpu.roll(t, HEAD_DIM // 2, 1))
        return t * cos_t + partner * sin_t

    def chunk(i, state):
        rows = pl.ds(pl.multiple_of(i * c, c), c)
        cos_t = cos_ref[rows, :]
        sin_t = sin_ref[rows, :]
        q = rope(q_ref[rows, :], cos_t, sin_t)
        k = rope(k_ref[rows, :], cos_t, sin_t) * QK_SCALE
        v = v_ref[rows, :]
        kb = k.astype(BF16)
        s0 = lax.dot_general(jnp.where(lo, q, 0.0).astype(BF16), kb, _NT,
                             preferred_element_type=F32) * intra0
        s1 = lax.dot_general(jnp.where(lo, 0.0, q).astype(BF16), kb, _NT,
                             preferred_element_type=F32) * intra1
        sc = jnp.concatenate([s0, s1], axis=1).astype(BF16)
        vv = jnp.concatenate([jnp.where(lo, v, 0.0), jnp.where(lo, 0.0, v)], axis=0).astype(BF16)
        o = jnp.dot(sc, vv, preferred_element_type=F32)
        o = o + jnp.dot(q.astype(BF16), state.astype(BF16), preferred_element_type=F32) * qdec
        kd = (k * kdec).astype(BF16)
        kv = lax.dot_general(kd, v.astype(BF16), _TN, preferred_element_type=F32)
        state = cdm * state + mask * kv
        o2 = o * o
        m0 = jnp.sum(jnp.where(lo, o2, 0.0), axis=-1, keepdims=True)
        m1 = jnp.sum(jnp.where(lo, 0.0, o2), axis=-1, keepdims=True)
        r = jnp.where(lo, lax.rsqrt(m0 / HEAD_DIM + EPS), lax.rsqrt(m1 / HEAD_DIM + EPS))
        gt = g_ref[rows, :]
        o_ref[rows, :] = (gt * jax.nn.sigmoid(gt)) * (o * r * gain)
        return state

    lax.fori_loop(0, n_ch, chunk, jnp.zeros((LANES, LANES), F32))


def _ret_tables(seq):
    c = RET_CHUNK
    half = HEAD_DIM // 2
    inv_freq = 1.0 / (RET_ROPE_BASE ** jnp.linspace(0.0, 1.0, half, dtype=F32))
    ang = jnp.arange(seq, dtype=F32)[:, None] * inv_freq[None, :]
    cos, sin = jnp.cos(ang), jnp.sin(ang)
    cos_t = jnp.tile(cos, (1, LANES // half))
    sin_t = jnp.tile(jnp.concatenate([-sin, sin], axis=1), (1, LANES // HEAD_DIM))
    log_gamma = jnp.log1p(-jnp.exp2(-5.0 - jnp.arange(RET_HEADS, dtype=F32)))
    idx = jnp.arange(c, dtype=F32)
    diff = idx[:, None] - idx[None, :]
    intra = jnp.where(diff >= 0, jnp.exp(log_gamma[:, None, None] * jnp.maximum(diff, 0.0)), 0.0)
    k_decay = jnp.exp(log_gamma[:, None] * (c - 1.0 - idx)[None, :])
    q_decay = jnp.exp(log_gamma[:, None] * (idx + 1.0)[None, :])
    chunk_decay = jnp.exp(log_gamma * c)
    n_pairs = RET_HEADS // 2

    def lanes(t):
        return jnp.repeat(t.reshape(n_pairs, 2, c).transpose(0, 2, 1), HEAD_DIM, axis=2)

    blockdiag = (jnp.arange(LANES)[:, None] // HEAD_DIM) == (jnp.arange(LANES)[None, :] // HEAD_DIM)
    mask = blockdiag.astype(F32)
    cdm = jnp.repeat(chunk_decay.reshape(n_pairs, 2), HEAD_DIM, axis=1)[:, :, None] * mask[None]
    return (cos_t, sin_t, intra.reshape(n_pairs, 2, c, c), lanes(k_decay), lanes(q_decay), cdm, mask)


def _retention(proj, out_gain, batch, seq):
    c = RET_CHUNK
    assert seq % c == 0
    n_pairs = RET_HEADS // 2
    base = 3 * MOBA_WIDTH // LANES
    cos_t, sin_t, intra, kdec, qdec, cdm, mask = _ret_tables(seq)
    blk = lambda off: pl.BlockSpec((seq, LANES), lambda b, p: (b, base + off + p))
    tab = pl.BlockSpec((seq, LANES), lambda b, p: (0, 0))
    return pl.pallas_call(
        functools.partial(_ret_kernel, n_ch=seq // c),
        grid=(batch, n_pairs),
        in_specs=[blk(0), blk(n_pairs), blk(2 * n_pairs), blk(3 * n_pairs), tab, tab,
                  pl.BlockSpec((1, 2, c, c), lambda b, p: (p, 0, 0, 0)),
                  pl.BlockSpec((1, c, LANES), lambda b, p: (p, 0, 0)),
                  pl.BlockSpec((1, c, LANES), lambda b, p: (p, 0, 0)),
                  pl.BlockSpec((1, LANES, LANES), lambda b, p: (p, 0, 0)),
                  pl.BlockSpec((LANES, LANES), lambda b, p: (0, 0)),
                  pl.BlockSpec((1, LANES), lambda b, p: (0, p))],
        out_specs=pl.BlockSpec((seq, LANES), lambda b, p: (b, p)),
        out_shape=jax.ShapeDtypeStruct((batch * seq, RET_WIDTH), F32),
        compiler_params=_params("parallel", "parallel"),
    )(proj, proj, proj, proj, cos_t, sin_t, intra, kdec, qdec, cdm, mask,
      out_gain.reshape(1, RET_WIDTH))


def _mem_kv_kernel(m_ref, g_ref, w_ref, kg_ref, k_ref, v_ref):
    x = m_ref[0]
    ms = jnp.mean(x * x, axis=-1, keepdims=True)
    h = (x * lax.rsqrt(ms + EPS) * g_ref[...]).astype(BF16)
    kv = jnp.dot(h, w_ref[...], preferred_element_type=F32)
    lane = _lane_iota((x.shape[0], LANES))
    lo = lane < HEAD_DIM
    for p in range(MEM_WIDTH // LANES):
        kt = kv[:, p * LANES:(p + 1) * LANES]
        k_ref[0, :, p * LANES:(p + 1) * LANES] = _pair_rms(kt, kg_ref[...], lo).astype(BF16)
    v_ref[0] = kv[:, MEM_WIDTH:].astype(BF16)


def _mem_kv(mem, g_mem, w_mem_kv, k_gain):
    b, m, d = mem.shape
    return pl.pallas_call(
        _mem_kv_kernel,
        grid=(b,),
        in_specs=[pl.BlockSpec((1, m, d), lambda i: (i, 0, 0)),
                  pl.BlockSpec((1, d), lambda i: (0, 0)),
                  pl.BlockSpec((d, 2 * MEM_WIDTH), lambda i: (0, 0)),
                  pl.BlockSpec((1, LANES), lambda i: (0, 0))],
        out_specs=[pl.BlockSpec((1, m, MEM_WIDTH), lambda i: (i, 0, 0)),
                   pl.BlockSpec((1, m, MEM_WIDTH), lambda i: (i, 0, 0))],
        out_shape=[jax.ShapeDtypeStruct((b, m, MEM_WIDTH), BF16),
                   jax.ShapeDtypeStruct((b, m, MEM_WIDTH), BF16)],
        compiler_params=_params("parallel"),
    )(mem, g_mem.reshape(1, d), w_mem_kv.astype(BF16), jnp.tile(k_gain, 2).reshape(1, LANES))


def _mem_attn_kernel(q_ref, k_ref, v_ref, qg_ref, o_ref):
    tq = q_ref.shape[0]
    lane = _lane_iota((tq, LANES))
    lo = lane < HEAD_DIM
    for p in range(MEM_WIDTH // LANES):
        cols = slice(p * LANES, (p + 1) * LANES)
        qn = _pair_rms(q_ref[:, cols], qg_ref[...], lo) * QK_SCALE
        kt = k_ref[0, :, cols]
        vt = v_ref[0, :, cols]
        outs = []
        for h in range(2):
            qh = jnp.where(lo, qn, 0.0) if h == 0 else jnp.where(lo, 0.0, qn)
            s = lax.dot_general(qh.astype(BF16), kt, _NT, preferred_element_type=F32)
            m = jnp.max(s, axis=-1, keepdims=True)
            e = jnp.exp(s - m)
            pr = e / jnp.sum(e, axis=-1, keepdims=True)
            outs.append(jnp.dot(pr.astype(BF16), vt, preferred_element_type=F32))
        o_ref[:, cols] = jnp.where(lo, outs[0], outs[1])


def _mem_attn(proj, km, vm, q_gain, batch, seq, tq):
    m = km.shape[1]
    qcol = (IN_WIDTH - MEM_WIDTH) // MEM_WIDTH
    assert qcol * MEM_WIDTH == IN_WIDTH - MEM_WIDTH
    nq = seq // tq
    return pl.pallas_call(
        _mem_attn_kernel,
        grid=(batch, nq),
        in_specs=[pl.BlockSpec((tq, MEM_WIDTH), lambda b, i: (b * nq + i, qcol)),
                  pl.BlockSpec((1, m, MEM_WIDTH), lambda b, i: (b, 0, 0)),
                  pl.BlockSpec((1, m, MEM_WIDTH), lambda b, i: (b, 0, 0)),
                  pl.BlockSpec((1, LANES), lambda b, i: (0, 0))],
        out_specs=pl.BlockSpec((tq, MEM_WIDTH), lambda b, i: (b * nq + i, 0)),
        out_shape=jax.ShapeDtypeStruct((batch * seq, MEM_WIDTH), F32),
        compiler_params=_params("parallel", "parallel"),
    )(proj, km, vm, jnp.tile(q_gain, 2).reshape(1, LANES))


def _out_router_kernel(oa_ref, or_ref, om_ref, x_ref, wo_ref, g_ref, wr_ref, br_ref,
                       x2_ref, h2_ref, aux_ref, cnt_ref, run_ref):
    tm = x_ref.shape[0]

    @pl.when(pl.program_id(0) == 0)
    def _():
        run_ref[...] = jnp.zeros_like(run_ref)

    y = jnp.dot(oa_ref[...].astype(BF16), wo_ref[0:MOBA_WIDTH, :], preferred_element_type=F32)
    y += jnp.dot(or_ref[...].astype(BF16), wo_ref[MOBA_WIDTH:MOBA_WIDTH + RET_WIDTH, :],
                 preferred_element_type=F32)
    y += jnp.dot(om_ref[...].astype(BF16), wo_ref[MOBA_WIDTH + RET_WIDTH:, :],
                 preferred_element_type=F32)
    x2 = x_ref[...] + y
    x2_ref[...] = x2
    ms = jnp.mean(x2 * x2, axis=-1, keepdims=True)
    h2 = x2 * lax.rsqrt(ms + EPS) * g_ref[...]
    h2b = h2.astype(BF16)
    h2_ref[...] = h2b
    logits = jnp.dot(h2b, wr_ref[...], preferred_element_type=F32) + br_ref[...]
    lane = _lane_iota((tm, LANES))
    lane_f = lane.astype(F32)
    lg = jnp.where(lane < N_EXPERTS, logits, -jnp.inf)
    vals, picks = [], []
    for _ in range(TOP_K):
        m = jnp.max(lg, axis=-1, keepdims=True)
        first = jnp.min(jnp.where(lg == m, lane_f, jnp.inf), axis=-1, keepdims=True)
        pick = lane_f == first
        vals.append(m)
        picks.append(pick)
        lg = jnp.where(pick, -jnp.inf, lg)
    exps = [jnp.exp(v - vals[0]) for v in vals]
    denom = exps[0] + exps[1] + exps[2] + exps[3]
    sel = jnp.zeros((tm, LANES), F32)
    for pick in picks:
        sel = sel + jnp.where(pick, 1.0, 0.0)
    strict = (lax.broadcasted_iota(jnp.int32, (tm, tm), 0)
              > lax.broadcasted_iota(jnp.int32, (tm, tm), 1))
    pos = run_ref[0:1, :] + jnp.dot(jnp.where(strict, 1.0, 0.0).astype(BF16), sel.astype(BF16),
                                    preferred_element_type=F32)
    aux = jnp.zeros((tm, LANES), F32)
    for k in range(TOP_K):
        eid = jnp.sum(jnp.where(picks[k], lane_f, 0.0), axis=-1, keepdims=True)
        rank = jnp.sum(jnp.where(picks[k], pos, 0.0), axis=-1, keepdims=True)
        aux = aux + jnp.where(lane == k, eid, 0.0)
        aux = aux + jnp.where(lane == TOP_K + k, exps[k] / denom, 0.0)
        aux = aux + jnp.where(lane == 2 * TOP_K + k, rank, 0.0)
    aux_ref[...] = aux
    run = run_ref[0:1, :] + jnp.sum(sel, axis=0, keepdims=True)
    run_ref[...] = jnp.broadcast_to(run, run_ref.shape)
    cnt_ref[...] = jnp.broadcast_to(run, cnt_ref.shape)


def _out_router(oa, orr, om, x2d, w_out, g_ffn, w_router, b_router, tm):
    t, d = x2d.shape
    wr = jnp.zeros((d, LANES), BF16).at[:, :N_EXPERTS].set(w_router.astype(BF16))
    br = jnp.zeros((1, LANES), F32).at[0, :N_EXPERTS].set(b_router)
    row = lambda w: pl.BlockSpec((tm, w), lambda i: (i, 0))
    const = lambda r, c: pl.BlockSpec((r, c), lambda i: (0, 0))
    return pl.pallas_call(
        _out_router_kernel,
        grid=(t // tm,),
        in_specs=[row(MOBA_WIDTH), row(RET_WIDTH), row(MEM_WIDTH), row(d),
                  const(d, d), const(1, d), const(d, LANES), const(1, LANES)],
        out_specs=[row(d), row(d), row(LANES), const(8, LANES)],
        out_shape=[jax.ShapeDtypeStruct((t, d), F32), jax.ShapeDtypeStruct((t, d), BF16),
                   jax.ShapeDtypeStruct((t, LANES), F32), jax.ShapeDtypeStruct((8, LANES), F32)],
        scratch_shapes=[pltpu.VMEM((8, LANES), F32)],
        compiler_params=_params("arbitrary"),
    )(oa, orr, om, x2d, w_out.astype(BF16), g_ffn.reshape(1, d), wr, br)


def _moe_kernel(be_ref, nu_ref, xs_ref, w1g_ref, w1l_ref, b1g_ref, b1l_ref, w2_ref, b2_ref, o_ref):
    @pl.when(pl.program_id(0) < nu_ref[0])
    def _():
        x = xs_ref[...]
        zg = jnp.dot(x, w1g_ref[0], preferred_element_type=F32) + b1g_ref[0]
        zl = jnp.dot(x, w1l_ref[0], preferred_element_type=F32) + b1l_ref[0]
        xg = jnp.minimum(zg, SWIGLU_LIMIT)
        xl = jnp.clip(zl, -SWIGLU_LIMIT, SWIGLU_LIMIT)
        act = xg * jax.nn.sigmoid(SWIGLU_ALPHA * xg) * (xl + 1.0)
        o_ref[...] = jnp.dot(act.astype(BF16), w2_ref[0], preferred_element_type=F32) + b2_ref[0]


def _moe_experts(xs, blk_exp, n_used, w1g, w1l, b1g, b1l, w2, b2):
    p_rows, d = xs.shape
    d_ff = w2.shape[1]
    n_blocks = p_rows // MOE_BLOCK
    rows = lambda i, be, nu: (jnp.minimum(i, nu[0] - 1), 0)
    wsel = lambda i, be, nu: (be[i], 0, 0)
    grid_spec = pltpu.PrefetchScalarGridSpec(
        num_scalar_prefetch=2,
        grid=(n_blocks,),
        in_specs=[pl.BlockSpec((MOE_BLOCK, d), rows),
                  pl.BlockSpec((1, d, d_ff), wsel), pl.BlockSpec((1, d, d_ff), wsel),
                  pl.BlockSpec((1, 1, d_ff), wsel), pl.BlockSpec((1, 1, d_ff), wsel),
                  pl.BlockSpec((1, d_ff, d), wsel), pl.BlockSpec((1, 1, d), wsel)],
        out_specs=pl.BlockSpec((MOE_BLOCK, d), rows),
    )
    return pl.pallas_call(
        _moe_kernel,
        grid_spec=grid_spec,
        out_shape=jax.ShapeDtypeStruct((p_rows, d), F32),
        compiler_params=_params("arbitrary"),
    )(blk_exp, n_used, xs, w1g, w1l, b1g, b1l, w2, b2)


def _combine_kernel(x2_ref, aux_ref, yg_ref, o_ref):
    out = x2_ref[...]
    aux = aux_ref[...]
    d = out.shape[1]
    for k in range(TOP_K):
        out = out + aux[:, TOP_K + k:TOP_K + k + 1] * yg_ref[:, k * d:(k + 1) * d]
    o_ref[...] = out


def _combine(x2, aux, yg, tm):
    t, d = x2.shape
    return pl.pallas_call(
        _combine_kernel,
        grid=(t // tm,),
        in_specs=[pl.BlockSpec((tm, d), lambda i: (i, 0)),
                  pl.BlockSpec((tm, LANES), lambda i: (i, 0)),
                  pl.BlockSpec((tm, TOP_K * d), lambda i: (i, 0))],
        out_specs=pl.BlockSpec((tm, d), lambda i: (i, 0)),
        out_shape=jax.ShapeDtypeStruct((t, d), F32),
        compiler_params=_params("parallel"),
    )(x2, aux, yg)


def _layer(x, mem, g_attn, w_in, moba_q_gain, moba_k_gain, ret_out_gain, g_mem, w_mem_kv,
           mem_q_gain, mem_k_gain, w_out, g_ffn, w_router, b_router, w1, b1, w2, b2):
    batch, seq, d = x.shape
    t = batch * seq
    x2d = x.reshape(t, d)
    tm = min(512, t)

    proj = _rms_proj(x2d, g_attn, w_in.astype(BF16), tm)
    oa = _moba(proj, moba_q_gain, moba_k_gain, batch, seq)
    orr = _retention(proj, ret_out_gain, batch, seq)
    km, vm = _mem_kv(mem, g_mem, w_mem_kv, mem_k_gain)
    om = _mem_attn(proj, km, vm, mem_q_gain, batch, seq, min(512, seq))
    x2, h2, aux, counts = _out_router(oa, orr, om, x2d, w_out, g_ffn, w_router, b_router, tm)

    eidx = aux[:, 0:TOP_K].astype(jnp.int32)
    rank = aux[:, 2 * TOP_K:3 * TOP_K].astype(jnp.int32)
    cnt = counts[0, :N_EXPERTS].astype(jnp.int32)
    padded = ((cnt + MOE_BLOCK - 1) // MOE_BLOCK) * MOE_BLOCK
    pends = jnp.cumsum(padded)
    pstarts = pends - padded
    dest = pstarts[eidx] + rank
    a = t * TOP_K
    n_blocks = -(-(a + N_EXPERTS * (MOE_BLOCK - 1)) // MOE_BLOCK)
    p_rows = n_blocks * MOE_BLOCK
    blk_start = jnp.arange(n_blocks, dtype=jnp.int32) * MOE_BLOCK
    blk_exp = jnp.minimum(jnp.searchsorted(pends, blk_start, side='right'),
                          N_EXPERTS - 1).astype(jnp.int32)
    n_used = (pends[-1] // MOE_BLOCK).astype(jnp.int32).reshape(1)

    row_tok = jnp.zeros((p_rows,), jnp.int32).at[dest.reshape(a)].set(
        jnp.arange(a, dtype=jnp.int32) // TOP_K)
    xs = jnp.take(h2, row_tok, axis=0)

    d_ff = w2.shape[1]
    w1g = w1[:, :, 0::2].astype(BF16)
    w1l = w1[:, :, 1::2].astype(BF16)
    b1g = b1[:, 0::2].reshape(N_EXPERTS, 1, d_ff)
    b1l = b1[:, 1::2].reshape(N_EXPERTS, 1, d_ff)
    ys = _moe_experts(xs, blk_exp, n_used, w1g, w1l, b1g, b1l, w2.astype(BF16),
                      b2.reshape(N_EXPERTS, 1, d))
    yg = jnp.take(ys, dest.reshape(a), axis=0).reshape(t, TOP_K * d)
    out = _combine(x2, aux, yg, min(256, t))
    return out.reshape(batch, seq, d)


def kernel(x, mem, g_attn, w_in, moba_q_gain, moba_k_gain, ret_out_gain, g_mem, w_mem_kv,
           mem_q_gain, mem_k_gain, w_out, g_ffn, w_router, b_router, w1, b1, w2, b2):
    for l in range(g_attn.shape[0]):
        x = _layer(x, mem, g_attn[l], w_in[l], moba_q_gain[l], moba_k_gain[l], ret_out_gain[l],
                   g_mem[l], w_mem_kv[l], mem_q_gain[l], mem_k_gain[l], w_out[l], g_ffn[l],
                   w_router[l], b_router[l], w1[l], b1[l], w2[l], b2[l])
    return x
```

```python
import functools

import jax
import jax.numpy as jnp
from jax import lax
from jax.experimental import pallas as pl
from jax.experimental.pallas import tpu as pltpu

F32 = jnp.float32
BF16 = jnp.bfloat16

LANES = 128
HEAD_DIM = 64
MOBA_HEADS = 8
RET_HEADS = 4
MEM_HEADS = 4
MOBA_WIDTH = MOBA_HEADS * HEAD_DIM
RET_WIDTH = RET_HEADS * HEAD_DIM
MEM_WIDTH = MEM_HEADS * HEAD_DIM
IN_WIDTH = 3 * MOBA_WIDTH + 4 * RET_WIDTH + MEM_WIDTH
MOBA_BLOCK = 256
MOBA_TOPK = 3
RET_CHUNK = 128
RET_ROPE_BASE = 10000.0
N_EXPERTS = 32
TOP_K = 4
SWIGLU_ALPHA = 1.702
SWIGLU_LIMIT = 7.0
MOE_BLOCK = 512
EPS = 1e-6
NEG = -1e30
QK_SCALE = HEAD_DIM ** -0.5

VMEM_LIMIT = 48 * 1024 * 1024
MOE_VMEM_LIMIT = 56 * 1024 * 1024

_NT = (((1,), (1,)), ((), ()))
_TN = (((0,), (0,)), ((), ()))


def _params(*sem):
    return pltpu.CompilerParams(dimension_semantics=sem, vmem_limit_bytes=VMEM_LIMIT)


def _lane_iota(shape):
    return lax.broadcasted_iota(jnp.int32, shape, len(shape) - 1)


def _pair_rms(t, gain, lo):
    t2 = t * t
    s0 = jnp.sum(jnp.where(lo, t2, 0.0), axis=-1, keepdims=True)
    s1 = jnp.sum(jnp.where(lo, 0.0, t2), axis=-1, keepdims=True)
    r = jnp.where(lo, lax.rsqrt(s0 / HEAD_DIM + EPS), lax.rsqrt(s1 / HEAD_DIM + EPS))
    return t * r * gain


def _rms_proj_kernel(x_ref, g_ref, w_ref, o_ref):
    x = x_ref[...]
    ms = jnp.mean(x * x, axis=-1, keepdims=True)
    h = (x * lax.rsqrt(ms + EPS) * g_ref[...]).astype(BF16)
    o_ref[...] = jnp.dot(h, w_ref[...], preferred_element_type=F32)


def _rms_proj(x2d, gain, w, tm):
    t, d = x2d.shape
    n = w.shape[1]
    return pl.pallas_call(
        _rms_proj_kernel,
        grid=(t // tm,),
        in_specs=[pl.BlockSpec((tm, d), lambda i: (i, 0)),
                  pl.BlockSpec((1, d), lambda i: (0, 0)),
                  pl.BlockSpec((d, n), lambda i: (0, 0))],
        out_specs=pl.BlockSpec((tm, n), lambda i: (i, 0)),
        out_shape=jax.ShapeDtypeStruct((t, n), F32),
        compiler_params=_params("parallel"),
    )(x2d, gain.reshape(1, d), w)


def _moba_kernel(q_ref, k_ref, v_ref, qg_ref, kg_ref, o_ref, ka_ref, vb_ref, gt_ref, *, n_blk):
    blk = MOBA_BLOCK
    lane = _lane_iota((blk, LANES))
    lo = lane < HEAD_DIM
    col = jnp.where(lo, lane, lane - HEAD_DIM)
    col_f = col.astype(F32)
    lane1 = _lane_iota((1, LANES))
    qg = qg_ref[...]
    kg = kg_ref[...]
    tri = (lax.broadcasted_iota(jnp.int32, (blk, blk), 0)
           >= lax.broadcasted_iota(jnp.int32, (blk, blk), 1))

    def rows_of(j):
        return pl.ds(pl.multiple_of(j * blk, blk), blk)

    gt_ref[...] = jnp.zeros_like(gt_ref)

    def prep(j, carry):
        rows = rows_of(j)
        kn = _pair_rms(k_ref[rows, :], kg, lo)
        km = jnp.mean(kn, axis=0, keepdims=True)
        gt_ref[pl.ds(HEAD_DIM + j, 1), :] = jnp.where(lane1 < HEAD_DIM, km, 0.0)
        gt_ref[pl.ds(j, 1), :] = jnp.where(lane1 < HEAD_DIM, 0.0, km)
        one0 = jnp.where(lane == HEAD_DIM + j, 1.0, 0.0)
        one1 = jnp.where(lane == j, 1.0, 0.0)
        ka_ref[0, rows, :] = jnp.where(lo, kn, one0).astype(BF16)
        ka_ref[1, rows, :] = jnp.where(lo, one1, kn).astype(BF16)
        vb_ref[rows, :] = v_ref[rows, :].astype(BF16)
        return carry

    lax.fori_loop(0, n_blk, prep, 0)

    def seg_max(t):
        a = jnp.max(jnp.where(lo, t, -jnp.inf), axis=-1, keepdims=True)
        b = jnp.max(jnp.where(lo, -jnp.inf, t), axis=-1, keepdims=True)
        return jnp.where(lo, a, b)

    def seg_min(t):
        a = jnp.min(jnp.where(lo, t, jnp.inf), axis=-1, keepdims=True)
        b = jnp.min(jnp.where(lo, jnp.inf, t), axis=-1, keepdims=True)
        return jnp.where(lo, a, b)

    def qblock(n, carry):
        rows_n = rows_of(n)
        qn = _pair_rms(q_ref[rows_n, :], qg, lo)
        gate = lax.dot_general(qn.astype(BF16), gt_ref[...].astype(BF16), _NT,
                               preferred_element_type=F32)
        valid = col < n
        g = jnp.where(valid, gate, NEG)
        sel = jnp.zeros((blk, LANES), jnp.bool_)
        for _ in range(MOBA_TOPK):
            m = seg_max(g)
            first = seg_min(jnp.where(g == m, col_f, jnp.inf))
            pick = col_f == first
            sel = jnp.logical_or(sel, pick)
            g = jnp.where(pick, -jnp.inf, g)
        keep = jnp.logical_or(jnp.logical_and(sel, valid), col == n)
        bias = jnp.where(jnp.logical_or(keep, col >= n_blk), 0.0, NEG)
        qs = qn * QK_SCALE
        qa = (jnp.where(lo, qs, bias).astype(BF16), jnp.where(lo, bias, qs).astype(BF16))

        def tile(h, j, causal):
            s = lax.dot_general(qa[h], ka_ref[h, rows_of(j), :], _NT, preferred_element_type=F32)
            return jnp.where(tri, s, NEG) if causal else s

        state = []
        for h in range(2):
            s = tile(h, n, True)
            m = jnp.max(s, axis=-1, keepdims=True)
            p = jnp.exp(s - m)
            l = jnp.sum(p, axis=-1, keepdims=True)
            acc = jnp.dot(p.astype(BF16), vb_ref[rows_n, :], preferred_element_type=F32)
            state += [m, l, acc]

        def past(j, st):
            out = []
            vj = vb_ref[rows_of(j), :]
            for h in range(2):
                m, l, acc = st[3 * h:3 * h + 3]
                s = tile(h, j, False)
                m_new = jnp.maximum(m, jnp.max(s, axis=-1, keepdims=True))
                alpha = jnp.exp(m - m_new)
                p = jnp.exp(s - m_new)
                l = alpha * l + jnp.sum(p, axis=-1, keepdims=True)
                acc = alpha * acc + jnp.dot(p.astype(BF16), vj, preferred_element_type=F32)
                out += [m_new, l, acc]
            return tuple(out)

        st = lax.fori_loop(0, n, past, tuple(state))
        o_ref[rows_n, :] = jnp.where(lo, st[2] / st[1], st[5] / st[4])
        return carry

    lax.fori_loop(0, n_blk, qblock, 0)


def _moba(proj, q_gain, k_gain, batch, seq):
    n_blk = seq // MOBA_BLOCK
    assert seq % MOBA_BLOCK == 0 and n_blk <= HEAD_DIM and n_blk >= MOBA_TOPK
    n_pairs = MOBA_WIDTH // LANES
    qg = jnp.tile(q_gain, 2).reshape(1, LANES)
    kg = jnp.tile(k_gain, 2).reshape(1, LANES)
    blk = lambda off: pl.BlockSpec((seq, LANES), lambda b, p: (b, off + p))
    vec = pl.BlockSpec((1, LANES), lambda b, p: (0, 0))
    return pl.pallas_call(
        functools.partial(_moba_kernel, n_blk=n_blk),
        grid=(batch, n_pairs),
        in_specs=[blk(0), blk(n_pairs), blk(2 * n_pairs), vec, vec],
        out_specs=pl.BlockSpec((seq, LANES), lambda b, p: (b, p)),
        out_shape=jax.ShapeDtypeStruct((batch * seq, MOBA_WIDTH), F32),
        scratch_shapes=[pltpu.VMEM((2, seq, LANES), BF16),
                        pltpu.VMEM((seq, LANES), BF16),
                        pltpu.VMEM((LANES, LANES), F32)],
        compiler_params=_params("parallel", "parallel"),
    )(proj, proj, proj, qg, kg)


def _ret_kernel(q_ref, k_ref, v_ref, g_ref, cos_ref, sin_ref, intra_ref, kdec_ref, qdec_ref,
                cdm_ref, mask_ref, gain_ref, o_ref, *, n_ch):
    c = RET_CHUNK
    lane = _lane_iota((c, LANES))
    lo = lane < HEAD_DIM
    first_half = (lane % HEAD_DIM) < (HEAD_DIM // 2)
    gain = gain_ref[...]
    intra0 = intra_ref[0, 0]
    intra1 = intra_ref[0, 1]
    kdec = kdec_ref[0]
    qdec = qdec_ref[0]
    cdm = cdm_ref[0]
    mask = mask_ref[...]

    def rope(t, cos_t, sin_t):
        partner = jnp.where(first_half, pltpu.roll(t, LANES - HEAD_DIM // 2, 1),
                            pltpu.roll(t, HEAD_DIM // 2, 1))
        return t * cos_t + partner * sin_t

    def chunk(i, state):
        rows = pl.ds(pl.multiple_of(i * c, c), c)
        cos_t = cos_ref[rows, :]
        sin_t = sin_ref[rows, :]
        q = rope(q_ref[rows, :], cos_t, sin_t)
        k = rope(k_ref[rows, :], cos_t, sin_t) * QK_SCALE
        v = v_ref[rows, :]
        kb = k.astype(BF16)
        s0 = lax.dot_general(jnp.where(lo, q, 0.0).astype(BF16), kb, _NT,
                             preferred_element_type=F32) * intra0
        s1 = lax.dot_general(jnp.where(lo, 0.0, q).astype(BF16), kb, _NT,
                             preferred_element_type=F32) * intra1
        sc = jnp.concatenate([s0, s1], axis=1).astype(BF16)
        vv = jnp.concatenate([jnp.where(lo, v, 0.0), jnp.where(lo, 0.0, v)], axis=0).astype(BF16)
        o = jnp.dot(sc, vv, preferred_element_type=F32)
        o = o + jnp.dot(q.astype(BF16), state.astype(BF16), preferred_element_type=F32) * qdec
        kd = (k * kdec).astype(BF16)
        kv = lax.dot_general(kd, v.astype(BF16), _TN, preferred_element_type=F32)
        state = cdm * state + mask * kv
        o2 = o * o
        m0 = jnp.sum(jnp.where(lo, o2, 0.0), axis=-1, keepdims=True)
        m1 = jnp.sum(jnp.where(lo, 0.0, o2), axis=-1, keepdims=True)
        r = jnp.where(lo, lax.rsqrt(m0 / HEAD_DIM + EPS), lax.rsqrt(m1 / HEAD_DIM + EPS))
        gt = g_ref[rows, :]
        o_ref[rows, :] = (gt * jax.nn.sigmoid(gt)) * (o * r * gain)
        return state

    lax.fori_loop(0, n_ch, chunk, jnp.zeros((LANES, LANES), F32))


def _ret_tables(seq):
    c = RET_CHUNK
    half = HEAD_DIM // 2
    inv_freq = 1.0 / (RET_ROPE_BASE ** jnp.linspace(0.0, 1.0, half, dtype=F32))
    ang = jnp.arange(seq, dtype=F32)[:, None] * inv_freq[None, :]
    cos, sin = jnp.cos(ang), jnp.sin(ang)
    cos_t = jnp.tile(cos, (1, LANES // half))
    sin_t = jnp.tile(jnp.concatenate([-sin, sin], axis=1), (1, LANES // HEAD_DIM))
    log_gamma = jnp.log1p(-jnp.exp2(-5.0 - jnp.arange(RET_HEADS, dtype=F32)))
    idx = jnp.arange(c, dtype=F32)
    diff = idx[:, None] - idx[None, :]
    intra = jnp.where(diff >= 0, jnp.exp(log_gamma[:, None, None] * jnp.maximum(diff, 0.0)), 0.0)
    k_decay = jnp.exp(log_gamma[:, None] * (c - 1.0 - idx)[None, :])
    q_decay = jnp.exp(log_gamma[:, None] * (idx + 1.0)[None, :])
    chunk_decay = jnp.exp(log_gamma * c)
    n_pairs = RET_HEADS // 2

    def lanes(t):
        return jnp.repeat(t.reshape(n_pairs, 2, c).transpose(0, 2, 1), HEAD_DIM, axis=2)

    blockdiag = (jnp.arange(LANES)[:, None] // HEAD_DIM) == (jnp.arange(LANES)[None, :] // HEAD_DIM)
    mask = blockdiag.astype(F32)
    cdm = jnp.repeat(chunk_decay.reshape(n_pairs, 2), HEAD_DIM, axis=1)[:, :, None] * mask[None]
    return (cos_t, sin_t, intra.reshape(n_pairs, 2, c, c), lanes(k_decay), lanes(q_decay), cdm, mask)


def _retention(proj, out_gain, batch, seq):
    c = RET_CHUNK
    assert seq % c == 0
    n_pairs = RET_HEADS // 2
    base = 3 * MOBA_WIDTH // LANES
    cos_t, sin_t, intra, kdec, qdec, cdm, mask = _ret_tables(seq)
    blk = lambda off: pl.BlockSpec((seq, LANES), lambda b, p: (b, base + off + p))
    tab = pl.BlockSpec((seq, LANES), lambda b, p: (0, 0))
    return pl.pallas_call(
        functools.partial(_ret_kernel, n_ch=seq // c),
        grid=(batch, n_pairs),
        in_specs=[blk(0), blk(n_pairs), blk(2 * n_pairs), blk(3 * n_pairs), tab, tab,
                  pl.BlockSpec((1, 2, c, c), lambda b, p: (p, 0, 0, 0)),
                  pl.BlockSpec((1, c, LANES), lambda b, p: (p, 0, 0)),
                  pl.BlockSpec((1, c, LANES), lambda b, p: (p, 0, 0)),
                  pl.BlockSpec((1, LANES, LANES), lambda b, p: (p, 0, 0)),
                  pl.BlockSpec((LANES, LANES), lambda b, p: (0, 0)),
                  pl.BlockSpec((1, LANES), lambda b, p: (0, p))],
        out_specs=pl.BlockSpec((seq, LANES), lambda b, p: (b, p)),
        out_shape=jax.ShapeDtypeStruct((batch * seq, RET_WIDTH), F32),
        compiler_params=_params("parallel", "parallel"),
    )(proj, proj, proj, proj, cos_t, sin_t, intra, kdec, qdec, cdm, mask,
      out_gain.reshape(1, RET_WIDTH))


def _mem_kv_kernel(m_ref, g_ref, w_ref, kg_ref, k_ref, v_ref):
    x = m_ref[0]
    ms = jnp.mean(x * x, axis=-1, keepdims=True)
    h = (x * lax.rsqrt(ms + EPS) * g_ref[...]).astype(BF16)
    kv = jnp.dot(h, w_ref[...], preferred_element_type=F32)
    lane = _lane_iota((x.shape[0], LANES))
    lo = lane < HEAD_DIM
    for p in range(MEM_WIDTH // LANES):
        kt = kv[:, p * LANES:(p + 1) * LANES]
        k_ref[0, :, p * LANES:(p + 1) * LANES] = _pair_rms(kt, kg_ref[...], lo).astype(BF16)
    v_ref[0] = kv[:, MEM_WIDTH:].astype(BF16)


def _mem_kv(mem, g_mem, w_mem_kv, k_gain):
    b, m, d = mem.shape
    return pl.pallas_call(
        _mem_kv_kernel,
        grid=(b,),
        in_specs=[pl.BlockSpec((1, m, d), lambda i: (i, 0, 0)),
                  pl.BlockSpec((1, d), lambda i: (0, 0)),
                  pl.BlockSpec((d, 2 * MEM_WIDTH), lambda i: (0, 0)),
                  pl.BlockSpec((1, LANES), lambda i: (0, 0))],
        out_specs=[pl.BlockSpec((1, m, MEM_WIDTH), lambda i: (i, 0, 0)),
                   pl.BlockSpec((1, m, MEM_WIDTH), lambda i: (i, 0, 0))],
        out_shape=[jax.ShapeDtypeStruct((b, m, MEM_WIDTH), BF16),
                   jax.ShapeDtypeStruct((b, m, MEM_WIDTH), BF16)],
        compiler_params=_params("parallel"),
    )(mem, g_mem.reshape(1, d), w_mem_kv.astype(BF16), jnp.tile(k_gain, 2).reshape(1, LANES))


def _mem_attn_kernel(q_ref, k_ref, v_ref, qg_ref, o_ref):
    tq = q_ref.shape[0]
    lane = _lane_iota((tq, LANES))
    lo = lane < HEAD_DIM
    for p in range(MEM_WIDTH // LANES):
        cols = slice(p * LANES, (p + 1) * LANES)
        qn = _pair_rms(q_ref[:, cols], qg_ref[...], lo) * QK_SCALE
        kt = k_ref[0, :, cols]
        vt = v_ref[0, :, cols]
        outs = []
        for h in range(2):
            qh = jnp.where(lo, qn, 0.0) if h == 0 else jnp.where(lo, 0.0, qn)
            s = lax.dot_general(qh.astype(BF16), kt, _NT, preferred_element_type=F32)
            m = jnp.max(s, axis=-1, keepdims=True)
            e = jnp.exp(s - m)
            pr = e / jnp.sum(e, axis=-1, keepdims=True)
            outs.append(jnp.dot(pr.astype(BF16), vt, preferred_element_type=F32))
        o_ref[:, cols] = jnp.where(lo, outs[0], outs[1])


def _mem_attn(proj, km, vm, q_gain, batch, seq, tq):
    m = km.shape[1]
    qcol = (IN_WIDTH - MEM_WIDTH) // MEM_WIDTH
    assert qcol * MEM_WIDTH == IN_WIDTH - MEM_WIDTH
    nq = seq // tq
    return pl.pallas_call(
        _mem_attn_kernel,
        grid=(batch, nq),
        in_specs=[pl.BlockSpec((tq, MEM_WIDTH), lambda b, i: (b * nq + i, qcol)),
                  pl.BlockSpec((1, m, MEM_WIDTH), lambda b, i: (b, 0, 0)),
                  pl.BlockSpec((1, m, MEM_WIDTH), lambda b, i: (b, 0, 0)),
                  pl.BlockSpec((1, LANES), lambda b, i: (0, 0))],
        out_specs=pl.BlockSpec((tq, MEM_WIDTH), lambda b, i: (b * nq + i, 0)),
        out_shape=jax.ShapeDtypeStruct((batch * seq, MEM_WIDTH), F32),
        compiler_params=_params("parallel", "parallel"),
    )(proj, km, vm, jnp.tile(q_gain, 2).reshape(1, LANES))


def _out_router_kernel(oa_ref, or_ref, om_ref, x_ref, wo_ref, g_ref, wr_ref, br_ref,
                       x2_ref, h2_ref, aux_ref, cnt_ref, run_ref):
    tm = x_ref.shape[0]

    @pl.when(pl.program_id(0) == 0)
    def _():
        run_ref[...] = jnp.zeros_like(run_ref)

    y = jnp.dot(oa_ref[...].astype(BF16), wo_ref[0:MOBA_WIDTH, :], preferred_element_type=F32)
    y += jnp.dot(or_ref[...].astype(BF16), wo_ref[MOBA_WIDTH:MOBA_WIDTH + RET_WIDTH, :],
                 preferred_element_type=F32)
    y += jnp.dot(om_ref[...].astype(BF16), wo_ref[MOBA_WIDTH + RET_WIDTH:, :],
                 preferred_element_type=F32)
    x2 = x_ref[...] + y
    x2_ref[...] = x2
    ms = jnp.mean(x2 * x2, axis=-1, keepdims=True)
    h2 = x2 * lax.rsqrt(ms + EPS) * g_ref[...]
    h2b = h2.astype(BF16)
    h2_ref[...] = h2b
    logits = jnp.dot(h2b, wr_ref[...], preferred_element_type=F32) + br_ref[...]
    lane = _lane_iota((tm, LANES))
    lane_f = lane.astype(F32)
    lg = jnp.where(lane < N_EXPERTS, logits, -jnp.inf)
    vals, picks = [], []
    for _ in range(TOP_K):
        m = jnp.max(lg, axis=-1, keepdims=True)
        first = jnp.min(jnp.where(lg == m, lane_f, jnp.inf), axis=-1, keepdims=True)
        pick = lane_f == first
        vals.append(m)
        picks.append(pick)
        lg = jnp.where(pick, -jnp.inf, lg)
    exps = [jnp.exp(v - vals[0]) for v in vals]
    denom = exps[0] + exps[1] + exps[2] + exps[3]
    sel = jnp.zeros((tm, LANES), F32)
    for pick in picks:
        sel = sel + jnp.where(pick, 1.0, 0.0)
    strict = (lax.broadcasted_iota(jnp.int32, (tm, tm), 0)
              > lax.broadcasted_iota(jnp.int32, (tm, tm), 1))
    pos = run_ref[0:1, :] + jnp.dot(jnp.where(strict, 1.0, 0.0).astype(BF16), sel.astype(BF16),
                                    preferred_element_type=F32)
    aux = jnp.zeros((tm, LANES), F32)
    for k in range(TOP_K):
        eid = jnp.sum(jnp.where(picks[k], lane_f, 0.0), axis=-1, keepdims=True)
        rank = jnp.sum(jnp.where(picks[k], pos, 0.0), axis=-1, keepdims=True)
        aux = aux + jnp.where(lane == k, eid, 0.0)
        aux = aux + jnp.where(lane == TOP_K + k, exps[k] / denom, 0.0)
        aux = aux + jnp.where(lane == 2 * TOP_K + k, rank, 0.0)
    aux_ref[...] = aux
    run = run_ref[0:1, :] + jnp.sum(sel, axis=0, keepdims=True)
    run_ref[...] = jnp.broadcast_to(run, run_ref.shape)
    cnt_ref[...] = jnp.broadcast_to(run, cnt_ref.shape)


def _out_router(oa, orr, om, x2d, w_out, g_ffn, w_router, b_router, tm):
    t, d = x2d.shape
    wr = jnp.zeros((d, LANES), BF16).at[:, :N_EXPERTS].set(w_router.astype(BF16))
    br = jnp.zeros((1, LANES), F32).at[0, :N_EXPERTS].set(b_router)
    row = lambda w: pl.BlockSpec((tm, w), lambda i: (i, 0))
    const = lambda r, c: pl.BlockSpec((r, c), lambda i: (0, 0))
    return pl.pallas_call(
        _out_router_kernel,
        grid=(t // tm,),
        in_specs=[row(MOBA_WIDTH), row(RET_WIDTH), row(MEM_WIDTH), row(d),
                  const(d, d), const(1, d), const(d, LANES), const(1, LANES)],
        out_specs=[row(d), row(d), row(LANES), const(8, LANES)],
        out_shape=[jax.ShapeDtypeStruct((t, d), F32), jax.ShapeDtypeStruct((t, d), BF16),
                   jax.ShapeDtypeStruct((t, LANES), F32), jax.ShapeDtypeStruct((8, LANES), F32)],
        scratch_shapes=[pltpu.VMEM((8, LANES), F32)],
        compiler_params=_params("arbitrary"),
    )(oa, orr, om, x2d, w_out.astype(BF16), g_ffn.reshape(1, d), wr, br)


MXU_COLS = 256


def _moe_kernel(be_ref, nu_ref, xs_ref, w1_ref, b1g_ref, b1l_ref, w2_ref, b2_ref, o_ref,
                w1p_ref, w2b_ref, act_ref):
    i = pl.program_id(0)
    half = MXU_COLS // 2
    n_chunks = w1_ref.shape[2] // MXU_COLS

    @pl.when(i < nu_ref[0])
    def _():
        @pl.when(jnp.logical_or(i == 0, be_ref[i] != be_ref[jnp.maximum(i - 1, 0)]))
        def _():
            r = lax.broadcasted_iota(jnp.int32, (MXU_COLS, MXU_COLS), 0)
            c = lax.broadcasted_iota(jnp.int32, (MXU_COLS, MXU_COLS), 1)
            src = jnp.where(c < half, 2 * c, 2 * (c - half) + 1)
            perm = jnp.where(r == src, 1.0, 0.0).astype(BF16)
            for ch in range(n_chunks):
                cols = slice(ch * MXU_COLS, (ch + 1) * MXU_COLS)
                w = w1_ref[0, :, cols].astype(BF16)
                w1p_ref[:, cols] = jnp.dot(w, perm, preferred_element_type=F32).astype(BF16)
            w2b_ref[...] = w2_ref[0].astype(BF16)

        x = xs_ref[...]
        for ch in range(n_chunks):
            z = jnp.dot(x, w1p_ref[:, ch * MXU_COLS:(ch + 1) * MXU_COLS], preferred_element_type=F32)
            hs = slice(ch * half, (ch + 1) * half)
            xg = jnp.minimum(z[:, :half] + b1g_ref[0][:, hs], SWIGLU_LIMIT)
            xl = jnp.clip(z[:, half:] + b1l_ref[0][:, hs], -SWIGLU_LIMIT, SWIGLU_LIMIT)
            act_ref[:, hs] = (xg * jax.nn.sigmoid(SWIGLU_ALPHA * xg) * (xl + 1.0)).astype(BF16)
        o_ref[...] = jnp.dot(act_ref[...], w2b_ref[...], preferred_element_type=F32) + b2_ref[0]


def _moe_experts(xs, blk_exp, n_used, w1, b1g, b1l, w2, b2):
    p_rows, d = xs.shape
    d_ff = w2.shape[1]
    assert w1.shape[2] == 2 * d_ff and (2 * d_ff) % MXU_COLS == 0
    n_blocks = p_rows // MOE_BLOCK
    rows = lambda i, be, nu: (jnp.minimum(i, nu[0] - 1), 0)
    wsel = lambda i, be, nu: (be[i], 0, 0)
    grid_spec = pltpu.PrefetchScalarGridSpec(
        num_scalar_prefetch=2,
        grid=(n_blocks,),
        in_specs=[pl.BlockSpec((MOE_BLOCK, d), rows),
                  pl.BlockSpec((1, d, 2 * d_ff), wsel),
                  pl.BlockSpec((1, 1, d_ff), wsel), pl.BlockSpec((1, 1, d_ff), wsel),
                  pl.BlockSpec((1, d_ff, d), wsel), pl.BlockSpec((1, 1, d), wsel)],
        out_specs=pl.BlockSpec((MOE_BLOCK, d), rows),
        scratch_shapes=[pltpu.VMEM((d, 2 * d_ff), BF16), pltpu.VMEM((d_ff, d), BF16),
                        pltpu.VMEM((MOE_BLOCK, d_ff), BF16)],
    )
    return pl.pallas_call(
        _moe_kernel,
        grid_spec=grid_spec,
        out_shape=jax.ShapeDtypeStruct((p_rows, d), F32),
        compiler_params=pltpu.CompilerParams(dimension_semantics=("arbitrary",),
                                             vmem_limit_bytes=MOE_VMEM_LIMIT),
    )(blk_exp, n_used, xs, w1, b1g, b1l, w2, b2)


def _combine_kernel(x2_ref, aux_ref, yg_ref, o_ref):
    out = x2_ref[...]
    aux = aux_ref[...]
    for k in range(TOP_K):
        out = out + aux[:, TOP_K + k:TOP_K + k + 1] * yg_ref[k]
    o_ref[...] = out


def _combine(x2, aux, yg, tm):
    t, d = x2.shape
    return pl.pallas_call(
        _combine_kernel,
        grid=(t // tm,),
        in_specs=[pl.BlockSpec((tm, d), lambda i: (i, 0)),
                  pl.BlockSpec((tm, LANES), lambda i: (i, 0)),
                  pl.BlockSpec((TOP_K, tm, d), lambda i: (0, i, 0))],
        out_specs=pl.BlockSpec((tm, d), lambda i: (i, 0)),
        out_shape=jax.ShapeDtypeStruct((t, d), F32),
        compiler_params=_params("parallel"),
    )(x2, aux, yg)


def _layer(x, mem, g_attn, w_in, moba_q_gain, moba_k_gain, ret_out_gain, g_mem, w_mem_kv,
           mem_q_gain, mem_k_gain, w_out, g_ffn, w_router, b_router, w1, b1, w2, b2):
    batch, seq, d = x.shape
    t = batch * seq
    x2d = x.reshape(t, d)
    tm = min(512, t)

    proj = _rms_proj(x2d, g_attn, w_in.astype(BF16), tm)
    oa = _moba(proj, moba_q_gain, moba_k_gain, batch, seq)
    orr = _retention(proj, ret_out_gain, batch, seq)
    km, vm = _mem_kv(mem, g_mem, w_mem_kv, mem_k_gain)
    om = _mem_attn(proj, km, vm, mem_q_gain, batch, seq, min(512, seq))
    x2, h2, aux, counts = _out_router(oa, orr, om, x2d, w_out, g_ffn, w_router, b_router, tm)

    eidx = aux[:, 0:TOP_K].astype(jnp.int32)
    rank = aux[:, 2 * TOP_K:3 * TOP_K].astype(jnp.int32)
    cnt = counts[0, :N_EXPERTS].astype(jnp.int32)
    padded = ((cnt + MOE_BLOCK - 1) // MOE_BLOCK) * MOE_BLOCK
    pends = jnp.cumsum(padded)
    pstarts = pends - padded
    dest = pstarts[eidx] + rank
    a = t * TOP_K
    n_blocks = -(-(a + N_EXPERTS * (MOE_BLOCK - 1)) // MOE_BLOCK)
    p_rows = n_blocks * MOE_BLOCK
    blk_start = jnp.arange(n_blocks, dtype=jnp.int32) * MOE_BLOCK
    blk_exp = jnp.minimum(jnp.sum(blk_start[:, None] >= pends[None, :], axis=1),
                          N_EXPERTS - 1).astype(jnp.int32)
    n_used = (pends[-1] // MOE_BLOCK).astype(jnp.int32).reshape(1)

    row_tok = jnp.zeros((p_rows,), jnp.int32).at[dest.reshape(a)].set(
        jnp.arange(a, dtype=jnp.int32) // TOP_K)
    xs = jnp.take(h2, row_tok, axis=0)

    d_ff = w2.shape[1]
    b1g = b1[:, 0::2].reshape(N_EXPERTS, 1, d_ff)
    b1l = b1[:, 1::2].reshape(N_EXPERTS, 1, d_ff)
    ys = _moe_experts(xs, blk_exp, n_used, w1, b1g, b1l, w2, b2.reshape(N_EXPERTS, 1, d))
    yg = jnp.take(ys, dest.T.reshape(a), axis=0).reshape(TOP_K, t, d)
    out = _combine(x2, aux, yg, min(256, t))
    return out.reshape(batch, seq, d)


def kernel(x, mem, g_attn, w_in, moba_q_gain, moba_k_gain, ret_out_gain, g_mem, w_mem_kv,
           mem_q_gain, mem_k_gain, w_out, g_ffn, w_router, b_router, w1, b1, w2, b2):
    for l in range(g_attn.shape[0]):
        x = _layer(x, mem, g_attn[l], w_in[l], moba_q_gain[l], moba_k_gain[l], ret_out_gain[l],
                   g_mem[l], w_mem_kv[l], mem_q_gain[l], mem_k_gain[l], w_out[l], g_ffn[l],
                   w_router[l], b_router[l], w1[l], b1[l], w2[l], b2[l])
    return x
```

```python
import functools

import jax
import jax.numpy as jnp
from jax import lax
from jax.experimental import pallas as pl
from jax.experimental.pallas import tpu as pltpu

F32 = jnp.float32
BF16 = jnp.bfloat16

LANES = 128
HEAD_DIM = 64
MOBA_HEADS = 8
RET_HEADS = 4
MEM_HEADS = 4
MOBA_WIDTH = MOBA_HEADS * HEAD_DIM
RET_WIDTH = RET_HEADS * HEAD_DIM
MEM_WIDTH = MEM_HEADS * HEAD_DIM
IN_WIDTH = 3 * MOBA_WIDTH + 4 * RET_WIDTH + MEM_WIDTH
MOBA_BLOCK = 256
MOBA_TOPK = 3
RET_CHUNK = 128
RET_ROPE_BASE = 10000.0
N_EXPERTS = 32
TOP_K = 4
SWIGLU_ALPHA = 1.702
SWIGLU_LIMIT = 7.0
MOE_BLOCK = 512
EPS = 1e-6
NEG = -1e30
QK_SCALE = HEAD_DIM ** -0.5

VMEM_LIMIT = 48 * 1024 * 1024
MOE_VMEM_LIMIT = 56 * 1024 * 1024

_NT = (((1,), (1,)), ((), ()))
_TN = (((0,), (0,)), ((), ()))


def _params(*sem):
    return pltpu.CompilerParams(dimension_semantics=sem, vmem_limit_bytes=VMEM_LIMIT)


def _lane_iota(shape):
    return lax.broadcasted_iota(jnp.int32, shape, len(shape) - 1)


def _pair_rms(t, gain, lo):
    t2 = t * t
    s0 = jnp.sum(jnp.where(lo, t2, 0.0), axis=-1, keepdims=True)
    s1 = jnp.sum(jnp.where(lo, 0.0, t2), axis=-1, keepdims=True)
    r = jnp.where(lo, lax.rsqrt(s0 / HEAD_DIM + EPS), lax.rsqrt(s1 / HEAD_DIM + EPS))
    return t * r * gain


def _rms_proj_kernel(x_ref, g_ref, w_ref, o_ref):
    x = x_ref[...]
    ms = jnp.mean(x * x, axis=-1, keepdims=True)
    h = (x * lax.rsqrt(ms + EPS) * g_ref[...]).astype(BF16)
    o_ref[...] = jnp.dot(h, w_ref[...], preferred_element_type=F32)


def _rms_proj(x2d, gain, w, tm):
    t, d = x2d.shape
    n = w.shape[1]
    return pl.pallas_call(
        _rms_proj_kernel,
        grid=(t // tm,),
        in_specs=[pl.BlockSpec((tm, d), lambda i: (i, 0)),
                  pl.BlockSpec((1, d), lambda i: (0, 0)),
                  pl.BlockSpec((d, n), lambda i: (0, 0))],
        out_specs=pl.BlockSpec((tm, n), lambda i: (i, 0)),
        out_shape=jax.ShapeDtypeStruct((t, n), F32),
        compiler_params=_params("parallel"),
    )(x2d, gain.reshape(1, d), w)


SHIFT_SAFE = 80.0
MOBA_GROUP = 4


def _moba_kernel(tab_ref, q_ref, k_ref, v_ref, qg_ref, kg_ref, m_ref, o_ref,
                 ka_ref, va_ref, qa_ref, gt_ref, acc_ref, pa_ref, pb_ref, *, n_blk, n_trips):
    blk = MOBA_BLOCK
    grp = MOBA_GROUP
    nb8 = -(-n_blk // 8) * 8
    lane = _lane_iota((blk, LANES))
    lo = lane < HEAD_DIM
    lane1 = _lane_iota((1, LANES))
    qg = qg_ref[...]
    kg = kg_ref[...]
    tri = (lax.broadcasted_iota(jnp.int32, (blk, blk), 0)
           >= lax.broadcasted_iota(jnp.int32, (blk, blk), 1))
    row_t = lax.broadcasted_iota(jnp.int32, (nb8, blk), 0)
    row_tf = row_t.astype(F32)
    fill_row = lax.broadcasted_iota(jnp.int32, (HEAD_DIM - nb8, blk), 0)
    filler = jnp.where(fill_row == HEAD_DIM - nb8 - 1, -m_ref[...], 0.0)

    def rows_of(j):
        return pl.ds(pl.multiple_of(j * blk, blk), blk)

    gt_ref[...] = jnp.zeros_like(gt_ref)

    def prep_k(j, carry):
        rows = rows_of(j)
        kn = _pair_rms(k_ref[rows, :], kg, lo)
        km = jnp.mean(kn, axis=0, keepdims=True)
        gt_ref[pl.ds(HEAD_DIM + j, 1), :] = jnp.where(lane1 < HEAD_DIM, km, 0.0)
        gt_ref[pl.ds(j, 1), :] = jnp.where(lane1 < HEAD_DIM, 0.0, km)
        tag0 = jnp.where(jnp.logical_or(lane == HEAD_DIM + j, lane == LANES - 1), 1.0, 0.0)
        tag1 = jnp.where(jnp.logical_or(lane == j, lane == HEAD_DIM - 1), 1.0, 0.0)
        ka_ref[0, rows, :] = jnp.where(lo, kn, tag0).astype(BF16)
        ka_ref[1, rows, :] = jnp.where(lo, tag1, kn).astype(BF16)
        v = v_ref[rows, :]
        va_ref[0, rows, :] = jnp.where(lo, v, jnp.where(lane == HEAD_DIM, 1.0, 0.0)).astype(BF16)
        va_ref[1, rows, :] = jnp.where(lo, jnp.where(lane == 0, 1.0, 0.0), v).astype(BF16)
        return carry

    lax.fori_loop(0, n_blk, prep_k, 0)

    def block_bias(g, n):
        valid = row_t < n
        g = jnp.where(valid, g, NEG)
        sel = jnp.zeros((nb8, blk), jnp.bool_)
        for _ in range(MOBA_TOPK):
            m = jnp.max(g, axis=0, keepdims=True)
            first = jnp.min(jnp.where(g == m, row_tf, jnp.inf), axis=0, keepdims=True)
            pick = row_tf == first
            sel = jnp.logical_or(sel, pick)
            g = jnp.where(pick, -jnp.inf, g)
        keep = jnp.logical_or(jnp.logical_and(sel, valid), row_t == n)
        return jnp.where(keep, 0.0, NEG)

    def prep_q(n, carry):
        rows_n = rows_of(n)
        qn = _pair_rms(q_ref[rows_n, :], qg, lo)
        gate_t = lax.dot_general(gt_ref[...].astype(BF16), qn.astype(BF16), _NT,
                                 preferred_element_type=F32)
        bias_t = jnp.concatenate([block_bias(gate_t[0:nb8], n), filler,
                                  block_bias(gate_t[HEAD_DIM:HEAD_DIM + nb8], n), filler], axis=0)
        bias = bias_t.T
        qs = qn * QK_SCALE
        qa_ref[0, rows_n, :] = jnp.where(lo, qs, bias).astype(BF16)
        qa_ref[1, rows_n, :] = jnp.where(lo, bias, qs).astype(BF16)
        return carry

    lax.fori_loop(0, n_blk, prep_q, 0)

    def scores(h, n, j):
        s = lax.dot_general(qa_ref[h, rows_of(n), :], ka_ref[h, rows_of(j), :], _NT,
                            preferred_element_type=F32)
        return jnp.where(jnp.logical_or(tri, j != n), s, NEG)

    def normalise(n, a0, a1):
        o_ref[rows_of(n), :] = jnp.where(lo, a0 / a0[:, HEAD_DIM:HEAD_DIM + 1], a1 / a1[:, 0:1])

    @pl.when(tab_ref[0, 0] == 1)
    def _():
        acc_ref[...] = jnp.zeros_like(acc_ref)

        def make_probs(trip, dst_ref):
            for g in range(grp):
                n = tab_ref[1, trip * grp + g]
                j = tab_ref[2, trip * grp + g]
                for h in range(2):
                    dst_ref[g, h] = jnp.exp(scores(h, n, j)).astype(BF16)

        def apply_probs(trip, src_ref):
            for g in range(grp):
                n = tab_ref[1, trip * grp + g]
                j = tab_ref[2, trip * grp + g]
                for h in range(2):
                    acc_ref[h, rows_of(n), :] += jnp.dot(src_ref[g, h], va_ref[h, rows_of(j), :],
                                                         preferred_element_type=F32)

        make_probs(0, pa_ref)

        def two_trips(i, carry):
            apply_probs(2 * i, pa_ref)
            make_probs(2 * i + 1, pb_ref)
            apply_probs(2 * i + 1, pb_ref)
            make_probs(2 * i + 2, pa_ref)
            return carry

        lax.fori_loop(0, n_trips // 2, two_trips, 0)

        def fin(n, carry):
            normalise(n, acc_ref[0, rows_of(n), :], acc_ref[1, rows_of(n), :])
            return carry

        lax.fori_loop(0, n_blk, fin, 0)

    @pl.when(tab_ref[0, 0] != 1)
    def _():
        def qblock(n, carry):
            def one(j, st):
                out = []
                for h in range(2):
                    m, acc = st[2 * h], st[2 * h + 1]
                    s = scores(h, n, j)
                    m_new = jnp.maximum(m, jnp.max(s, axis=-1, keepdims=True))
                    p = jnp.exp(s - m_new).astype(BF16)
                    acc = jnp.exp(m - m_new) * acc + jnp.dot(p, va_ref[h, rows_of(j), :],
                                                             preferred_element_type=F32)
                    out += [m_new, acc]
                return tuple(out)

            zero = jnp.zeros((blk, LANES), F32)
            ninf = jnp.full((blk, 1), -jnp.inf, F32)
            st = lax.fori_loop(0, n + 1, one, (ninf, zero, ninf, zero))
            normalise(n, st[1], st[3])
            return carry

        lax.fori_loop(0, n_blk, qblock, 0)


def _moba_tiles(n_blk):
    tiles = [(n, j) for n in range(n_blk) for j in range(n + 1)]
    per_two = 2 * MOBA_GROUP
    n_trips = 2 * (-(-len(tiles) // per_two))
    tiles += [(0, 1)] * ((n_trips + 1) * MOBA_GROUP - len(tiles))
    return n_trips, tiles


def _moba(proj, q_gain, k_gain, batch, seq):
    n_blk = seq // MOBA_BLOCK
    assert seq % MOBA_BLOCK == 0 and MOBA_TOPK <= n_blk <= HEAD_DIM - 8
    n_pairs = MOBA_WIDTH // LANES
    qg = jnp.tile(q_gain, 2).reshape(1, LANES)
    kg = jnp.tile(k_gain, 2).reshape(1, LANES)
    shift = HEAD_DIM * QK_SCALE * jnp.max(jnp.abs(q_gain)) * jnp.max(jnp.abs(k_gain))
    flag = (2.0 * shift <= SHIFT_SAFE).astype(jnp.int32)
    n_trips, tiles = _moba_tiles(n_blk)
    tab = jnp.stack([jnp.full((len(tiles),), flag, jnp.int32),
                     jnp.asarray([t[0] for t in tiles], jnp.int32),
                     jnp.asarray([t[1] for t in tiles], jnp.int32)])
    mrow = jnp.full((1, MOBA_BLOCK), shift, F32)
    blk = lambda off: pl.BlockSpec((seq, LANES), lambda b, p, f: (b, off + p))
    vec = lambda w: pl.BlockSpec((1, w), lambda b, p, f: (0, 0))
    prob = pltpu.VMEM((MOBA_GROUP, 2, MOBA_BLOCK, MOBA_BLOCK), BF16)
    grid_spec = pltpu.PrefetchScalarGridSpec(
        num_scalar_prefetch=1,
        grid=(batch, n_pairs),
        in_specs=[blk(0), blk(n_pairs), blk(2 * n_pairs), vec(LANES), vec(LANES), vec(MOBA_BLOCK)],
        out_specs=pl.BlockSpec((seq, LANES), lambda b, p, f: (b, p)),
        scratch_shapes=[pltpu.VMEM((2, seq, LANES), BF16),
                        pltpu.VMEM((2, seq, LANES), BF16),
                        pltpu.VMEM((2, seq, LANES), BF16),
                        pltpu.VMEM((LANES, LANES), F32),
                        pltpu.VMEM((2, seq, LANES), F32),
                        prob, prob],
    )
    return pl.pallas_call(
        functools.partial(_moba_kernel, n_blk=n_blk, n_trips=n_trips),
        grid_spec=grid_spec,
        out_shape=jax.ShapeDtypeStruct((batch * seq, MOBA_WIDTH), F32),
        compiler_params=_params("parallel", "parallel"),
    )(tab, proj, proj, proj, qg, kg, mrow)


def _ret_kernel(q_ref, k_ref, v_ref, g_ref, cos_ref, sin_ref, intra_ref, kdec_ref, qdec_ref,
                cdm_ref, mask_ref, gain_ref, o_ref, *, n_ch):
    c = RET_CHUNK
    lane = _lane_iota((c, LANES))
    lo = lane < HEAD_DIM
    first_half = (lane % HEAD_DIM) < (HEAD_DIM // 2)
    gain = gain_ref[...]
    intra0 = intra_ref[0, 0]
    intra1 = intra_ref[0, 1]
    kdec = kdec_ref[0]
    qdec = qdec_ref[0]
    cdm = cdm_ref[0]
    mask = mask_ref[...]

    def rope(t, cos_t, sin_t):
        partner = jnp.where(first_half, pltpu.roll(t, LANES - HEAD_DIM // 2, 1),
                            pltpu.roll(t, HEAD_DIM // 2, 1))
        return t * cos_t + partner * sin_t

    def chunk(i, state):
        rows = pl.ds(pl.multiple_of(i * c, c), c)
        cos_t = cos_ref[rows, :]
        sin_t = sin_ref[rows, :]
        q = rope(q_ref[rows, :], cos_t, sin_t)
        k = rope(k_ref[rows, :], cos_t, sin_t) * QK_SCALE
        v = v_ref[rows, :]
        kb = k.astype(BF16)
        s0 = lax.dot_general(jnp.where(lo, q, 0.0).astype(BF16), kb, _NT,
                             preferred_element_type=F32) * intra0
        s1 = lax.dot_general(jnp.where(lo, 0.0, q).astype(BF16), kb, _NT,
                             preferred_element_type=F32) * intra1
        sc = jnp.concatenate([s0, s1], axis=1).astype(BF16)
        vv = jnp.concatenate([jnp.where(lo, v, 0.0), jnp.where(lo, 0.0, v)], axis=0).astype(BF16)
        o = jnp.dot(sc, vv, preferred_element_type=F32)
        o = o + jnp.dot(q.astype(BF16), state.astype(BF16), preferred_element_type=F32) * qdec
        kd = (k * kdec).astype(BF16)
        kv = lax.dot_general(kd, v.astype(BF16), _TN, preferred_element_type=F32)
        state = cdm * state + mask * kv
        o2 = o * o
        m0 = jnp.sum(jnp.where(lo, o2, 0.0), axis=-1, keepdims=True)
        m1 = jnp.sum(jnp.where(lo, 0.0, o2), axis=-1, keepdims=True)
        r = jnp.where(lo, lax.rsqrt(m0 / HEAD_DIM + EPS), lax.rsqrt(m1 / HEAD_DIM + EPS))
        gt = g_ref[rows, :]
        o_ref[rows, :] = (gt * jax.nn.sigmoid(gt)) * (o * r * gain)
        return state

    lax.fori_loop(0, n_ch, chunk, jnp.zeros((LANES, LANES), F32))


def _ret_tables(seq):
    c = RET_CHUNK
    half = HEAD_DIM // 2
    inv_freq = 1.0 / (RET_ROPE_BASE ** jnp.linspace(0.0, 1.0, half, dtype=F32))
    ang = jnp.arange(seq, dtype=F32)[:, None] * inv_freq[None, :]
    cos, sin = jnp.cos(ang), jnp.sin(ang)
    cos_t = jnp.tile(cos, (1, LANES // half))
    sin_t = jnp.tile(jnp.concatenate([-sin, sin], axis=1), (1, LANES // HEAD_DIM))
    log_gamma = jnp.log1p(-jnp.exp2(-5.0 - jnp.arange(RET_HEADS, dtype=F32)))
    idx = jnp.arange(c, dtype=F32)
    diff = idx[:, None] - idx[None, :]
    intra = jnp.where(diff >= 0, jnp.exp(log_gamma[:, None, None] * jnp.maximum(diff, 0.0)), 0.0)
    k_decay = jnp.exp(log_gamma[:, None] * (c - 1.0 - idx)[None, :])
    q_decay = jnp.exp(log_gamma[:, None] * (idx + 1.0)[None, :])
    chunk_decay = jnp.exp(log_gamma * c)
    n_pairs = RET_HEADS // 2

    def lanes(t):
        return jnp.repeat(t.reshape(n_pairs, 2, c).transpose(0, 2, 1), HEAD_DIM, axis=2)

    blockdiag = (jnp.arange(LANES)[:, None] // HEAD_DIM) == (jnp.arange(LANES)[None, :] // HEAD_DIM)
    mask = blockdiag.astype(F32)
    cdm = jnp.repeat(chunk_decay.reshape(n_pairs, 2), HEAD_DIM, axis=1)[:, :, None] * mask[None]
    return (cos_t, sin_t, intra.reshape(n_pairs, 2, c, c), lanes(k_decay), lanes(q_decay), cdm, mask)


def _retention(proj, out_gain, batch, seq):
    c = RET_CHUNK
    assert seq % c == 0
    n_pairs = RET_HEADS // 2
    base = 3 * MOBA_WIDTH // LANES
    cos_t, sin_t, intra, kdec, qdec, cdm, mask = _ret_tables(seq)
    blk = lambda off: pl.BlockSpec((seq, LANES), lambda b, p: (b, base + off + p))
    tab = pl.BlockSpec((seq, LANES), lambda b, p: (0, 0))
    return pl.pallas_call(
        functools.partial(_ret_kernel, n_ch=seq // c),
        grid=(batch, n_pairs),
        in_specs=[blk(0), blk(n_pairs), blk(2 * n_pairs), blk(3 * n_pairs), tab, tab,
                  pl.BlockSpec((1, 2, c, c), lambda b, p: (p, 0, 0, 0)),
                  pl.BlockSpec((1, c, LANES), lambda b, p: (p, 0, 0)),
                  pl.BlockSpec((1, c, LANES), lambda b, p: (p, 0, 0)),
                  pl.BlockSpec((1, LANES, LANES), lambda b, p: (p, 0, 0)),
                  pl.BlockSpec((LANES, LANES), lambda b, p: (0, 0)),
                  pl.BlockSpec((1, LANES), lambda b, p: (0, p))],
        out_specs=pl.BlockSpec((seq, LANES), lambda b, p: (b, p)),
        out_shape=jax.ShapeDtypeStruct((batch * seq, RET_WIDTH), F32),
        compiler_params=_params("parallel", "parallel"),
    )(proj, proj, proj, proj, cos_t, sin_t, intra, kdec, qdec, cdm, mask,
      out_gain.reshape(1, RET_WIDTH))


def _mem_kv_kernel(m_ref, g_ref, w_ref, kg_ref, k_ref, v_ref):
    x = m_ref[0]
    ms = jnp.mean(x * x, axis=-1, keepdims=True)
    h = (x * lax.rsqrt(ms + EPS) * g_ref[...]).astype(BF16)
    kv = jnp.dot(h, w_ref[...], preferred_element_type=F32)
    lane = _lane_iota((x.shape[0], LANES))
    lo = lane < HEAD_DIM
    for p in range(MEM_WIDTH // LANES):
        kt = kv[:, p * LANES:(p + 1) * LANES]
        k_ref[0, :, p * LANES:(p + 1) * LANES] = _pair_rms(kt, kg_ref[...], lo).astype(BF16)
    v_ref[0] = kv[:, MEM_WIDTH:].astype(BF16)


def _mem_kv(mem, g_mem, w_mem_kv, k_gain):
    b, m, d = mem.shape
    return pl.pallas_call(
        _mem_kv_kernel,
        grid=(b,),
        in_specs=[pl.BlockSpec((1, m, d), lambda i: (i, 0, 0)),
                  pl.BlockSpec((1, d), lambda i: (0, 0)),
                  pl.BlockSpec((d, 2 * MEM_WIDTH), lambda i: (0, 0)),
                  pl.BlockSpec((1, LANES), lambda i: (0, 0))],
        out_specs=[pl.BlockSpec((1, m, MEM_WIDTH), lambda i: (i, 0, 0)),
                   pl.BlockSpec((1, m, MEM_WIDTH), lambda i: (i, 0, 0))],
        out_shape=[jax.ShapeDtypeStruct((b, m, MEM_WIDTH), BF16),
                   jax.ShapeDtypeStruct((b, m, MEM_WIDTH), BF16)],
        compiler_params=_params("parallel"),
    )(mem, g_mem.reshape(1, d), w_mem_kv.astype(BF16), jnp.tile(k_gain, 2).reshape(1, LANES))


def _mem_attn_kernel(q_ref, k_ref, v_ref, qg_ref, o_ref):
    tq = q_ref.shape[0]
    lane = _lane_iota((tq, LANES))
    lo = lane < HEAD_DIM
    for p in range(MEM_WIDTH // LANES):
        cols = slice(p * LANES, (p + 1) * LANES)
        qn = _pair_rms(q_ref[:, cols], qg_ref[...], lo) * QK_SCALE
        kt = k_ref[0, :, cols]
        vt = v_ref[0, :, cols]
        outs = []
        for h in range(2):
            qh = jnp.where(lo, qn, 0.0) if h == 0 else jnp.where(lo, 0.0, qn)
            s = lax.dot_general(qh.astype(BF16), kt, _NT, preferred_element_type=F32)
            m = jnp.max(s, axis=-1, keepdims=True)
            e = jnp.exp(s - m)
            pr = e / jnp.sum(e, axis=-1, keepdims=True)
            outs.append(jnp.dot(pr.astype(BF16), vt, preferred_element_type=F32))
        o_ref[:, cols] = jnp.where(lo, outs[0], outs[1])


def _mem_attn(proj, km, vm, q_gain, batch, seq, tq):
    m = km.shape[1]
    qcol = (IN_WIDTH - MEM_WIDTH) // MEM_WIDTH
    assert qcol * MEM_WIDTH == IN_WIDTH - MEM_WIDTH
    nq = seq // tq
    return pl.pallas_call(
        _mem_attn_kernel,
        grid=(batch, nq),
        in_specs=[pl.BlockSpec((tq, MEM_WIDTH), lambda b, i: (b * nq + i, qcol)),
                  pl.BlockSpec((1, m, MEM_WIDTH), lambda b, i: (b, 0, 0)),
                  pl.BlockSpec((1, m, MEM_WIDTH), lambda b, i: (b, 0, 0)),
                  pl.BlockSpec((1, LANES), lambda b, i: (0, 0))],
        out_specs=pl.BlockSpec((tq, MEM_WIDTH), lambda b, i: (b * nq + i, 0)),
        out_shape=jax.ShapeDtypeStruct((batch * seq, MEM_WIDTH), F32),
        compiler_params=_params("parallel", "parallel"),
    )(proj, km, vm, jnp.tile(q_gain, 2).reshape(1, LANES))


def _out_router_kernel(oa_ref, or_ref, om_ref, x_ref, wo_ref, g_ref, wr_ref, br_ref,
                       x2_ref, h2_ref, aux_ref, cnt_ref, run_ref):
    tm = x_ref.shape[0]

    @pl.when(pl.program_id(0) == 0)
    def _():
        run_ref[...] = jnp.zeros_like(run_ref)

    y = jnp.dot(oa_ref[...].astype(BF16), wo_ref[0:MOBA_WIDTH, :], preferred_element_type=F32)
    y += jnp.dot(or_ref[...].astype(BF16), wo_ref[MOBA_WIDTH:MOBA_WIDTH + RET_WIDTH, :],
                 preferred_element_type=F32)
    y += jnp.dot(om_ref[...].astype(BF16), wo_ref[MOBA_WIDTH + RET_WIDTH:, :],
                 preferred_element_type=F32)
    x2 = x_ref[...] + y
    x2_ref[...] = x2
    ms = jnp.mean(x2 * x2, axis=-1, keepdims=True)
    h2 = x2 * lax.rsqrt(ms + EPS) * g_ref[...]
    h2b = h2.astype(BF16)
    h2_ref[...] = h2b
    logits = jnp.dot(h2b, wr_ref[...], preferred_element_type=F32) + br_ref[...]
    lane = _lane_iota((tm, LANES))
    lane_f = lane.astype(F32)
    lg = jnp.where(lane < N_EXPERTS, logits, -jnp.inf)
    vals, picks = [], []
    for _ in range(TOP_K):
        m = jnp.max(lg, axis=-1, keepdims=True)
        first = jnp.min(jnp.where(lg == m, lane_f, jnp.inf), axis=-1, keepdims=True)
        pick = lane_f == first
        vals.append(m)
        picks.append(pick)
        lg = jnp.where(pick, -jnp.inf, lg)
    exps = [jnp.exp(v - vals[0]) for v in vals]
    denom = exps[0] + exps[1] + exps[2] + exps[3]
    sel = jnp.zeros((tm, LANES), F32)
    for pick in picks:
        sel = sel + jnp.where(pick, 1.0, 0.0)
    strict = (lax.broadcasted_iota(jnp.int32, (tm, tm), 0)
              > lax.broadcasted_iota(jnp.int32, (tm, tm), 1))
    pos = run_ref[0:1, :] + jnp.dot(jnp.where(strict, 1.0, 0.0).astype(BF16), sel.astype(BF16),
                                    preferred_element_type=F32)
    aux = jnp.zeros((tm, LANES), F32)
    for k in range(TOP_K):
        eid = jnp.sum(jnp.where(picks[k], lane_f, 0.0), axis=-1, keepdims=True)
        rank = jnp.sum(jnp.where(picks[k], pos, 0.0), axis=-1, keepdims=True)
        aux = aux + jnp.where(lane == k, eid, 0.0)
        aux = aux + jnp.where(lane == TOP_K + k, exps[k] / denom, 0.0)
        aux = aux + jnp.where(lane == 2 * TOP_K + k, rank, 0.0)
    aux_ref[...] = aux
    run = run_ref[0:1, :] + jnp.sum(sel, axis=0, keepdims=True)
    run_ref[...] = jnp.broadcast_to(run, run_ref.shape)
    cnt_ref[...] = jnp.broadcast_to(run, cnt_ref.shape)


def _out_router(oa, orr, om, x2d, w_out, g_ffn, w_router, b_router, tm):
    t, d = x2d.shape
    wr = jnp.zeros((d, LANES), BF16).at[:, :N_EXPERTS].set(w_router.astype(BF16))
    br = jnp.zeros((1, LANES), F32).at[0, :N_EXPERTS].set(b_router)
    row = lambda w: pl.BlockSpec((tm, w), lambda i: (i, 0))
    const = lambda r, c: pl.BlockSpec((r, c), lambda i: (0, 0))
    return pl.pallas_call(
        _out_router_kernel,
        grid=(t // tm,),
        in_specs=[row(MOBA_WIDTH), row(RET_WIDTH), row(MEM_WIDTH), row(d),
                  const(d, d), const(1, d), const(d, LANES), const(1, LANES)],
        out_specs=[row(d), row(d), row(LANES), const(8, LANES)],
        out_shape=[jax.ShapeDtypeStruct((t, d), F32), jax.ShapeDtypeStruct((t, d), BF16),
                   jax.ShapeDtypeStruct((t, LANES), F32), jax.ShapeDtypeStruct((8, LANES), F32)],
        scratch_shapes=[pltpu.VMEM((8, LANES), F32)],
        compiler_params=_params("arbitrary"),
    )(oa, orr, om, x2d, w_out.astype(BF16), g_ffn.reshape(1, d), wr, br)


MXU_COLS = 256


def _moe_kernel(be_ref, nu_ref, xs_ref, w1_ref, b1g_ref, b1l_ref, w2_ref, b2_ref, o_ref,
                w1p_ref, w2b_ref, act_ref):
    i = pl.program_id(0)
    half = MXU_COLS // 2
    n_chunks = w1_ref.shape[2] // MXU_COLS

    @pl.when(i < nu_ref[0])
    def _():
        @pl.when(jnp.logical_or(i == 0, be_ref[i] != be_ref[jnp.maximum(i - 1, 0)]))
        def _():
            r = lax.broadcasted_iota(jnp.int32, (MXU_COLS, MXU_COLS), 0)
            c = lax.broadcasted_iota(jnp.int32, (MXU_COLS, MXU_COLS), 1)
            src = jnp.where(c < half, 2 * c, 2 * (c - half) + 1)
            perm = jnp.where(r == src, 1.0, 0.0).astype(BF16)
            for ch in range(n_chunks):
                cols = slice(ch * MXU_COLS, (ch + 1) * MXU_COLS)
                w = w1_ref[0, :, cols].astype(BF16)
                w1p_ref[:, cols] = jnp.dot(w, perm, preferred_element_type=F32).astype(BF16)
            w2b_ref[...] = w2_ref[0].astype(BF16)

        x = xs_ref[...]
        for ch in range(n_chunks):
            z = jnp.dot(x, w1p_ref[:, ch * MXU_COLS:(ch + 1) * MXU_COLS], preferred_element_type=F32)
            hs = slice(ch * half, (ch + 1) * half)
            xg = jnp.minimum(z[:, :half] + b1g_ref[0][:, hs], SWIGLU_LIMIT)
            xl = jnp.clip(z[:, half:] + b1l_ref[0][:, hs], -SWIGLU_LIMIT, SWIGLU_LIMIT)
            act_ref[:, hs] = (xg * jax.nn.sigmoid(SWIGLU_ALPHA * xg) * (xl + 1.0)).astype(BF16)
        o_ref[...] = jnp.dot(act_ref[...], w2b_ref[...], preferred_element_type=F32) + b2_ref[0]


def _moe_experts(xs, blk_exp, n_used, w1, b1g, b1l, w2, b2):
    p_rows, d = xs.shape
    d_ff = w2.shape[1]
    assert w1.shape[2] == 2 * d_ff and (2 * d_ff) % MXU_COLS == 0
    n_blocks = p_rows // MOE_BLOCK
    rows = lambda i, be, nu: (jnp.minimum(i, nu[0] - 1), 0)
    wsel = lambda i, be, nu: (be[i], 0, 0)
    grid_spec = pltpu.PrefetchScalarGridSpec(
        num_scalar_prefetch=2,
        grid=(n_blocks,),
        in_specs=[pl.BlockSpec((MOE_BLOCK, d), rows),
                  pl.BlockSpec((1, d, 2 * d_ff), wsel),
                  pl.BlockSpec((1, 1, d_ff), wsel), pl.BlockSpec((1, 1, d_ff), wsel),
                  pl.BlockSpec((1, d_ff, d), wsel), pl.BlockSpec((1, 1, d), wsel)],
        out_specs=pl.BlockSpec((MOE_BLOCK, d), rows),
        scratch_shapes=[pltpu.VMEM((d, 2 * d_ff), BF16), pltpu.VMEM((d_ff, d), BF16),
                        pltpu.VMEM((MOE_BLOCK, d_ff), BF16)],
    )
    return pl.pallas_call(
        _moe_kernel,
        grid_spec=grid_spec,
        out_shape=jax.ShapeDtypeStruct((p_rows, d), F32),
        compiler_params=pltpu.CompilerParams(dimension_semantics=("arbitrary",),
                                             vmem_limit_bytes=MOE_VMEM_LIMIT),
    )(blk_exp, n_used, xs, w1, b1g, b1l, w2, b2)


def _combine_kernel(x2_ref, aux_ref, yg_ref, o_ref):
    out = x2_ref[...]
    aux = aux_ref[...]
    for k in range(TOP_K):
        out = out + aux[:, TOP_K + k:TOP_K + k + 1] * yg_ref[k]
    o_ref[...] = out


def _combine(x2, aux, yg, tm):
    t, d = x2.shape
    return pl.pallas_call(
        _combine_kernel,
        grid=(t // tm,),
        in_specs=[pl.BlockSpec((tm, d), lambda i: (i, 0)),
                  pl.BlockSpec((tm, LANES), lambda i: (i, 0)),
                  pl.BlockSpec((TOP_K, tm, d), lambda i: (0, i, 0))],
        out_specs=pl.BlockSpec((tm, d), lambda i: (i, 0)),
        out_shape=jax.ShapeDtypeStruct((t, d), F32),
        compiler_params=_params("parallel"),
    )(x2, aux, yg)


def _layer(x, mem, g_attn, w_in, moba_q_gain, moba_k_gain, ret_out_gain, g_mem, w_mem_kv,
           mem_q_gain, mem_k_gain, w_out, g_ffn, w_router, b_router, w1, b1, w2, b2):
    batch, seq, d = x.shape
    t = batch * seq
    x2d = x.reshape(t, d)
    tm = min(512, t)

    proj = _rms_proj(x2d, g_attn, w_in.astype(BF16), tm)
    oa = _moba(proj, moba_q_gain, moba_k_gain, batch, seq)
    orr = _retention(proj, ret_out_gain, batch, seq)
    km, vm = _mem_kv(mem, g_mem, w_mem_kv, mem_k_gain)
    om = _mem_attn(proj, km, vm, mem_q_gain, batch, seq, min(512, seq))
    x2, h2, aux, counts = _out_router(oa, orr, om, x2d, w_out, g_ffn, w_router, b_router, tm)

    eidx = aux[:, 0:TOP_K].astype(jnp.int32)
    rank = aux[:, 2 * TOP_K:3 * TOP_K].astype(jnp.int32)
    cnt = counts[0, :N_EXPERTS].astype(jnp.int32)
    padded = ((cnt + MOE_BLOCK - 1) // MOE_BLOCK) * MOE_BLOCK
    pends = jnp.cumsum(padded)
    pstarts = pends - padded
    dest = pstarts[eidx] + rank
    a = t * TOP_K
    n_blocks = -(-(a + N_EXPERTS * (MOE_BLOCK - 1)) // MOE_BLOCK)
    p_rows = n_blocks * MOE_BLOCK
    blk_start = jnp.arange(n_blocks, dtype=jnp.int32) * MOE_BLOCK
    blk_exp = jnp.minimum(jnp.sum(blk_start[:, None] >= pends[None, :], axis=1),
                          N_EXPERTS - 1).astype(jnp.int32)
    n_used = (pends[-1] // MOE_BLOCK).astype(jnp.int32).reshape(1)

    row_tok = jnp.zeros((p_rows,), jnp.int32).at[dest.reshape(a)].set(
        jnp.arange(a, dtype=jnp.int32) // TOP_K)
    xs = jnp.take(h2, row_tok, axis=0)

    d_ff = w2.shape[1]
    b1g = b1[:, 0::2].reshape(N_EXPERTS, 1, d_ff)
    b1l = b1[:, 1::2].reshape(N_EXPERTS, 1, d_ff)
    ys = _moe_experts(xs, blk_exp, n_used, w1, b1g, b1l, w2, b2.reshape(N_EXPERTS, 1, d))
    yg = jnp.take(ys, dest.T.reshape(a), axis=0).reshape(TOP_K, t, d)
    out = _combine(x2, aux, yg, min(256, t))
    return out.reshape(batch, seq, d)


def kernel(x, mem, g_attn, w_in, moba_q_gain, moba_k_gain, ret_out_gain, g_mem, w_mem_kv,
           mem_q_gain, mem_k_gain, w_out, g_ffn, w_router, b_router, w1, b1, w2, b2):
    for l in range(g_attn.shape[0]):
        x = _layer(x, mem, g_attn[l], w_in[l], moba_q_gain[l], moba_k_gain[l], ret_out_gain[l],
                   g_mem[l], w_mem_kv[l], mem_q_gain[l], mem_k_gain[l], w_out[l], g_ffn[l],
                   w_router[l], b_router[l], w1[l], b1[l], w2[l], b2[l])
    return x
```

```python
import functools

import jax
import jax.numpy as jnp
from jax import lax
from jax.experimental import pallas as pl
from jax.experimental.pallas import tpu as pltpu

F32 = jnp.float32
BF16 = jnp.bfloat16

LANES = 128
HEAD_DIM = 64
MOBA_HEADS = 8
RET_HEADS = 4
MEM_HEADS = 4
MOBA_WIDTH = MOBA_HEADS * HEAD_DIM
RET_WIDTH = RET_HEADS * HEAD_DIM
MEM_WIDTH = MEM_HEADS * HEAD_DIM
IN_WIDTH = 3 * MOBA_WIDTH + 4 * RET_WIDTH + MEM_WIDTH
MOBA_BLOCK = 256
MOBA_TOPK = 3
RET_CHUNK = 128
RET_ROPE_BASE = 10000.0
N_EXPERTS = 32
TOP_K = 4
SWIGLU_ALPHA = 1.702
SWIGLU_LIMIT = 7.0
MOE_BLOCK = 512
EPS = 1e-6
NEG = -1e30
QK_SCALE = HEAD_DIM ** -0.5

VMEM_LIMIT = 48 * 1024 * 1024
MOE_VMEM_LIMIT = 56 * 1024 * 1024
MIX_DTYPE = BF16
MOE_OUT_DTYPE = BF16

_NT = (((1,), (1,)), ((), ()))
_TN = (((0,), (0,)), ((), ()))


def _params(*sem):
    return pltpu.CompilerParams(dimension_semantics=sem, vmem_limit_bytes=VMEM_LIMIT)


def _lane_iota(shape):
    return lax.broadcasted_iota(jnp.int32, shape, len(shape) - 1)


def _pair_rms(t, gain, lo):
    t2 = t * t
    s0 = jnp.sum(jnp.where(lo, t2, 0.0), axis=-1, keepdims=True)
    s1 = jnp.sum(jnp.where(lo, 0.0, t2), axis=-1, keepdims=True)
    r = jnp.where(lo, lax.rsqrt(s0 / HEAD_DIM + EPS), lax.rsqrt(s1 / HEAD_DIM + EPS))
    return t * r * gain


def _rms_proj_kernel(x_ref, g_ref, w_ref, o_ref):
    x = x_ref[...]
    ms = jnp.mean(x * x, axis=-1, keepdims=True)
    h = (x * lax.rsqrt(ms + EPS) * g_ref[...]).astype(BF16)
    o_ref[...] = jnp.dot(h, w_ref[...], preferred_element_type=F32)


def _rms_proj(x2d, gain, w, tm):
    t, d = x2d.shape
    n = w.shape[1]
    return pl.pallas_call(
        _rms_proj_kernel,
        grid=(t // tm,),
        in_specs=[pl.BlockSpec((tm, d), lambda i: (i, 0)),
                  pl.BlockSpec((1, d), lambda i: (0, 0)),
                  pl.BlockSpec((d, n), lambda i: (0, 0))],
        out_specs=pl.BlockSpec((tm, n), lambda i: (i, 0)),
        out_shape=jax.ShapeDtypeStruct((t, n), F32),
        compiler_params=_params("parallel"),
    )(x2d, gain.reshape(1, d), w)


SHIFT_SAFE = 80.0
MOBA_GROUP = 4


def _moba_kernel(tab_ref, q_ref, k_ref, v_ref, qg_ref, kg_ref, m_ref, o_ref,
                 ka_ref, va_ref, qa_ref, gt_ref, acc_ref, pa_ref, pb_ref, *, n_blk, n_trips):
    blk = MOBA_BLOCK
    grp = MOBA_GROUP
    nb8 = -(-n_blk // 8) * 8
    lane = _lane_iota((blk, LANES))
    lo = lane < HEAD_DIM
    lane1 = _lane_iota((1, LANES))
    qg = qg_ref[...]
    kg = kg_ref[...]
    tri = (lax.broadcasted_iota(jnp.int32, (blk, blk), 0)
           >= lax.broadcasted_iota(jnp.int32, (blk, blk), 1))
    row_t = lax.broadcasted_iota(jnp.int32, (nb8, blk), 0)
    row_tf = row_t.astype(F32)
    fill_row = lax.broadcasted_iota(jnp.int32, (HEAD_DIM - nb8, blk), 0)
    filler = jnp.where(fill_row == HEAD_DIM - nb8 - 1, -m_ref[...], 0.0)

    def rows_of(j):
        return pl.ds(pl.multiple_of(j * blk, blk), blk)

    gt_ref[...] = jnp.zeros_like(gt_ref)

    def prep_k(j, carry):
        rows = rows_of(j)
        kn = _pair_rms(k_ref[rows, :], kg, lo)
        km = jnp.mean(kn, axis=0, keepdims=True)
        gt_ref[pl.ds(HEAD_DIM + j, 1), :] = jnp.where(lane1 < HEAD_DIM, km, 0.0)
        gt_ref[pl.ds(j, 1), :] = jnp.where(lane1 < HEAD_DIM, 0.0, km)
        tag0 = jnp.where(jnp.logical_or(lane == HEAD_DIM + j, lane == LANES - 1), 1.0, 0.0)
        tag1 = jnp.where(jnp.logical_or(lane == j, lane == HEAD_DIM - 1), 1.0, 0.0)
        ka_ref[0, rows, :] = jnp.where(lo, kn, tag0).astype(BF16)
        ka_ref[1, rows, :] = jnp.where(lo, tag1, kn).astype(BF16)
        v = v_ref[rows, :]
        va_ref[0, rows, :] = jnp.where(lo, v, jnp.where(lane == HEAD_DIM, 1.0, 0.0)).astype(BF16)
        va_ref[1, rows, :] = jnp.where(lo, jnp.where(lane == 0, 1.0, 0.0), v).astype(BF16)
        return carry

    lax.fori_loop(0, n_blk, prep_k, 0)

    def block_bias(g, n):
        valid = row_t < n
        g = jnp.where(valid, g, NEG)
        sel = jnp.zeros((nb8, blk), jnp.bool_)
        for _ in range(MOBA_TOPK):
            m = jnp.max(g, axis=0, keepdims=True)
            first = jnp.min(jnp.where(g == m, row_tf, jnp.inf), axis=0, keepdims=True)
            pick = row_tf == first
            sel = jnp.logical_or(sel, pick)
            g = jnp.where(pick, -jnp.inf, g)
        keep = jnp.logical_or(jnp.logical_and(sel, valid), row_t == n)
        return jnp.where(keep, 0.0, NEG)

    def prep_q(n, carry):
        rows_n = rows_of(n)
        qn = _pair_rms(q_ref[rows_n, :], qg, lo)
        gate_t = lax.dot_general(gt_ref[...].astype(BF16), qn.astype(BF16), _NT,
                                 preferred_element_type=F32)
        bias_t = jnp.concatenate([block_bias(gate_t[0:nb8], n), filler,
                                  block_bias(gate_t[HEAD_DIM:HEAD_DIM + nb8], n), filler], axis=0)
        bias = bias_t.T
        qs = qn * QK_SCALE
        qa_ref[0, rows_n, :] = jnp.where(lo, qs, bias).astype(BF16)
        qa_ref[1, rows_n, :] = jnp.where(lo, bias, qs).astype(BF16)
        return carry

    lax.fori_loop(0, n_blk, prep_q, 0)

    def scores(h, n, j):
        s = lax.dot_general(qa_ref[h, rows_of(n), :], ka_ref[h, rows_of(j), :], _NT,
                            preferred_element_type=F32)
        return jnp.where(jnp.logical_or(tri, j != n), s, NEG)

    def normalise(n, a0, a1):
        o_ref[rows_of(n), :] = jnp.where(lo, a0 / a0[:, HEAD_DIM:HEAD_DIM + 1],
                                         a1 / a1[:, 0:1]).astype(o_ref.dtype)

    @pl.when(tab_ref[0, 0] == 1)
    def _():
        acc_ref[...] = jnp.zeros_like(acc_ref)

        def make_probs(trip, dst_ref):
            for g in range(grp):
                n = tab_ref[1, trip * grp + g]
                j = tab_ref[2, trip * grp + g]
                for h in range(2):
                    dst_ref[g, h] = jnp.exp(scores(h, n, j)).astype(BF16)

        def apply_probs(trip, src_ref):
            for g in range(grp):
                n = tab_ref[1, trip * grp + g]
                j = tab_ref[2, trip * grp + g]
                for h in range(2):
                    acc_ref[h, rows_of(n), :] += jnp.dot(src_ref[g, h], va_ref[h, rows_of(j), :],
                                                         preferred_element_type=F32)

        make_probs(0, pa_ref)

        def two_trips(i, carry):
            apply_probs(2 * i, pa_ref)
            make_probs(2 * i + 1, pb_ref)
            apply_probs(2 * i + 1, pb_ref)
            make_probs(2 * i + 2, pa_ref)
            return carry

        lax.fori_loop(0, n_trips // 2, two_trips, 0)

        def fin(n, carry):
            normalise(n, acc_ref[0, rows_of(n), :], acc_ref[1, rows_of(n), :])
            return carry

        lax.fori_loop(0, n_blk, fin, 0)

    @pl.when(tab_ref[0, 0] != 1)
    def _():
        def qblock(n, carry):
            def one(j, st):
                out = []
                for h in range(2):
                    m, acc = st[2 * h], st[2 * h + 1]
                    s = scores(h, n, j)
                    m_new = jnp.maximum(m, jnp.max(s, axis=-1, keepdims=True))
                    p = jnp.exp(s - m_new).astype(BF16)
                    acc = jnp.exp(m - m_new) * acc + jnp.dot(p, va_ref[h, rows_of(j), :],
                                                             preferred_element_type=F32)
                    out += [m_new, acc]
                return tuple(out)

            zero = jnp.zeros((blk, LANES), F32)
            ninf = jnp.full((blk, 1), -jnp.inf, F32)
            st = lax.fori_loop(0, n + 1, one, (ninf, zero, ninf, zero))
            normalise(n, st[1], st[3])
            return carry

        lax.fori_loop(0, n_blk, qblock, 0)


def _moba_tiles(n_blk):
    tiles = [(n, j) for n in range(n_blk) for j in range(n + 1)]
    per_two = 2 * MOBA_GROUP
    n_trips = 2 * (-(-len(tiles) // per_two))
    tiles += [(0, 1)] * ((n_trips + 1) * MOBA_GROUP - len(tiles))
    return n_trips, tiles


def _moba(proj, q_gain, k_gain, batch, seq):
    n_blk = seq // MOBA_BLOCK
    assert seq % MOBA_BLOCK == 0 and MOBA_TOPK <= n_blk <= HEAD_DIM - 8
    n_pairs = MOBA_WIDTH // LANES
    qg = jnp.tile(q_gain, 2).reshape(1, LANES)
    kg = jnp.tile(k_gain, 2).reshape(1, LANES)
    shift = HEAD_DIM * QK_SCALE * jnp.max(jnp.abs(q_gain)) * jnp.max(jnp.abs(k_gain))
    flag = (2.0 * shift <= SHIFT_SAFE).astype(jnp.int32)
    n_trips, tiles = _moba_tiles(n_blk)
    tab = jnp.stack([jnp.full((len(tiles),), flag, jnp.int32),
                     jnp.asarray([t[0] for t in tiles], jnp.int32),
                     jnp.asarray([t[1] for t in tiles], jnp.int32)])
    mrow = jnp.full((1, MOBA_BLOCK), shift, F32)
    blk = lambda off: pl.BlockSpec((seq, LANES), lambda b, p, f: (b, off + p))
    vec = lambda w: pl.BlockSpec((1, w), lambda b, p, f: (0, 0))
    prob = pltpu.VMEM((MOBA_GROUP, 2, MOBA_BLOCK, MOBA_BLOCK), BF16)
    grid_spec = pltpu.PrefetchScalarGridSpec(
        num_scalar_prefetch=1,
        grid=(batch, n_pairs),
        in_specs=[blk(0), blk(n_pairs), blk(2 * n_pairs), vec(LANES), vec(LANES), vec(MOBA_BLOCK)],
        out_specs=pl.BlockSpec((seq, LANES), lambda b, p, f: (b, p)),
        scratch_shapes=[pltpu.VMEM((2, seq, LANES), BF16),
                        pltpu.VMEM((2, seq, LANES), BF16),
                        pltpu.VMEM((2, seq, LANES), BF16),
                        pltpu.VMEM((LANES, LANES), F32),
                        pltpu.VMEM((2, seq, LANES), F32),
                        prob, prob],
    )
    return pl.pallas_call(
        functools.partial(_moba_kernel, n_blk=n_blk, n_trips=n_trips),
        grid_spec=grid_spec,
        out_shape=jax.ShapeDtypeStruct((batch * seq, MOBA_WIDTH), MIX_DTYPE),
        compiler_params=_params("parallel", "parallel"),
    )(tab, proj, proj, proj, qg, kg, mrow)


def _ret_kernel(q_ref, k_ref, v_ref, g_ref, cos_ref, sin_ref, intra_ref, kdec_ref, qdec_ref,
                cdm_ref, mask_ref, gain_ref, o_ref, *, n_ch):
    c = RET_CHUNK
    lane = _lane_iota((c, LANES))
    lo = lane < HEAD_DIM
    first_half = (lane % HEAD_DIM) < (HEAD_DIM // 2)
    gain = gain_ref[...]
    intra0 = intra_ref[0, 0]
    intra1 = intra_ref[0, 1]
    kdec = kdec_ref[0]
    qdec = qdec_ref[0]
    cdm = cdm_ref[0]
    mask = mask_ref[...]

    def rope(t, cos_t, sin_t):
        partner = jnp.where(first_half, pltpu.roll(t, LANES - HEAD_DIM // 2, 1),
                            pltpu.roll(t, HEAD_DIM // 2, 1))
        return t * cos_t + partner * sin_t

    def chunk(i, state):
        rows = pl.ds(pl.multiple_of(i * c, c), c)
        cos_t = cos_ref[rows, :]
        sin_t = sin_ref[rows, :]
        q = rope(q_ref[rows, :], cos_t, sin_t)
        k = rope(k_ref[rows, :], cos_t, sin_t) * QK_SCALE
        v = v_ref[rows, :]
        kb = k.astype(BF16)
        s0 = lax.dot_general(jnp.where(lo, q, 0.0).astype(BF16), kb, _NT,
                             preferred_element_type=F32) * intra0
        s1 = lax.dot_general(jnp.where(lo, 0.0, q).astype(BF16), kb, _NT,
                             preferred_element_type=F32) * intra1
        sc = jnp.concatenate([s0, s1], axis=1).astype(BF16)
        vv = jnp.concatenate([jnp.where(lo, v, 0.0), jnp.where(lo, 0.0, v)], axis=0).astype(BF16)
        o = jnp.dot(sc, vv, preferred_element_type=F32)
        o = o + jnp.dot(q.astype(BF16), state.astype(BF16), preferred_element_type=F32) * qdec
        kd = (k * kdec).astype(BF16)
        kv = lax.dot_general(kd, v.astype(BF16), _TN, preferred_element_type=F32)
        state = cdm * state + mask * kv
        o2 = o * o
        m0 = jnp.sum(jnp.where(lo, o2, 0.0), axis=-1, keepdims=True)
        m1 = jnp.sum(jnp.where(lo, 0.0, o2), axis=-1, keepdims=True)
        r = jnp.where(lo, lax.rsqrt(m0 / HEAD_DIM + EPS), lax.rsqrt(m1 / HEAD_DIM + EPS))
        gt = g_ref[rows, :]
        o_ref[rows, :] = ((gt * jax.nn.sigmoid(gt)) * (o * r * gain)).astype(o_ref.dtype)
        return state

    lax.fori_loop(0, n_ch, chunk, jnp.zeros((LANES, LANES), F32))


def _ret_tables(seq):
    c = RET_CHUNK
    half = HEAD_DIM // 2
    inv_freq = 1.0 / (RET_ROPE_BASE ** jnp.linspace(0.0, 1.0, half, dtype=F32))
    ang = jnp.arange(seq, dtype=F32)[:, None] * inv_freq[None, :]
    cos, sin = jnp.cos(ang), jnp.sin(ang)
    cos_t = jnp.tile(cos, (1, LANES // half))
    sin_t = jnp.tile(jnp.concatenate([-sin, sin], axis=1), (1, LANES // HEAD_DIM))
    log_gamma = jnp.log1p(-jnp.exp2(-5.0 - jnp.arange(RET_HEADS, dtype=F32)))
    idx = jnp.arange(c, dtype=F32)
    diff = idx[:, None] - idx[None, :]
    intra = jnp.where(diff >= 0, jnp.exp(log_gamma[:, None, None] * jnp.maximum(diff, 0.0)), 0.0)
    k_decay = jnp.exp(log_gamma[:, None] * (c - 1.0 - idx)[None, :])
    q_decay = jnp.exp(log_gamma[:, None] * (idx + 1.0)[None, :])
    chunk_decay = jnp.exp(log_gamma * c)
    n_pairs = RET_HEADS // 2

    def lanes(t):
        return jnp.repeat(t.reshape(n_pairs, 2, c).transpose(0, 2, 1), HEAD_DIM, axis=2)

    blockdiag = (jnp.arange(LANES)[:, None] // HEAD_DIM) == (jnp.arange(LANES)[None, :] // HEAD_DIM)
    mask = blockdiag.astype(F32)
    cdm = jnp.repeat(chunk_decay.reshape(n_pairs, 2), HEAD_DIM, axis=1)[:, :, None] * mask[None]
    return (cos_t, sin_t, intra.reshape(n_pairs, 2, c, c), lanes(k_decay), lanes(q_decay), cdm, mask)


def _retention(proj, out_gain, batch, seq):
    c = RET_CHUNK
    assert seq % c == 0
    n_pairs = RET_HEADS // 2
    base = 3 * MOBA_WIDTH // LANES
    cos_t, sin_t, intra, kdec, qdec, cdm, mask = _ret_tables(seq)
    blk = lambda off: pl.BlockSpec((seq, LANES), lambda b, p: (b, base + off + p))
    tab = pl.BlockSpec((seq, LANES), lambda b, p: (0, 0))
    return pl.pallas_call(
        functools.partial(_ret_kernel, n_ch=seq // c),
        grid=(batch, n_pairs),
        in_specs=[blk(0), blk(n_pairs), blk(2 * n_pairs), blk(3 * n_pairs), tab, tab,
                  pl.BlockSpec((1, 2, c, c), lambda b, p: (p, 0, 0, 0)),
                  pl.BlockSpec((1, c, LANES), lambda b, p: (p, 0, 0)),
                  pl.BlockSpec((1, c, LANES), lambda b, p: (p, 0, 0)),
                  pl.BlockSpec((1, LANES, LANES), lambda b, p: (p, 0, 0)),
                  pl.BlockSpec((LANES, LANES), lambda b, p: (0, 0)),
                  pl.BlockSpec((1, LANES), lambda b, p: (0, p))],
        out_specs=pl.BlockSpec((seq, LANES), lambda b, p: (b, p)),
        out_shape=jax.ShapeDtypeStruct((batch * seq, RET_WIDTH), MIX_DTYPE),
        compiler_params=_params("parallel", "parallel"),
    )(proj, proj, proj, proj, cos_t, sin_t, intra, kdec, qdec, cdm, mask,
      out_gain.reshape(1, RET_WIDTH))


def _mem_kv_kernel(m_ref, g_ref, w_ref, kg_ref, k_ref, v_ref):
    x = m_ref[0]
    ms = jnp.mean(x * x, axis=-1, keepdims=True)
    h = (x * lax.rsqrt(ms + EPS) * g_ref[...]).astype(BF16)
    kv = jnp.dot(h, w_ref[...], preferred_element_type=F32)
    lane = _lane_iota((x.shape[0], LANES))
    lo = lane < HEAD_DIM
    for p in range(MEM_WIDTH // LANES):
        kt = kv[:, p * LANES:(p + 1) * LANES]
        k_ref[0, :, p * LANES:(p + 1) * LANES] = _pair_rms(kt, kg_ref[...], lo).astype(BF16)
    v_ref[0] = kv[:, MEM_WIDTH:].astype(BF16)


def _mem_kv(mem, g_mem, w_mem_kv, k_gain):
    b, m, d = mem.shape
    return pl.pallas_call(
        _mem_kv_kernel,
        grid=(b,),
        in_specs=[pl.BlockSpec((1, m, d), lambda i: (i, 0, 0)),
                  pl.BlockSpec((1, d), lambda i: (0, 0)),
                  pl.BlockSpec((d, 2 * MEM_WIDTH), lambda i: (0, 0)),
                  pl.BlockSpec((1, LANES), lambda i: (0, 0))],
        out_specs=[pl.BlockSpec((1, m, MEM_WIDTH), lambda i: (i, 0, 0)),
                   pl.BlockSpec((1, m, MEM_WIDTH), lambda i: (i, 0, 0))],
        out_shape=[jax.ShapeDtypeStruct((b, m, MEM_WIDTH), BF16),
                   jax.ShapeDtypeStruct((b, m, MEM_WIDTH), BF16)],
        compiler_params=_params("parallel"),
    )(mem, g_mem.reshape(1, d), w_mem_kv.astype(BF16), jnp.tile(k_gain, 2).reshape(1, LANES))


def _mem_attn_kernel(q_ref, k_ref, v_ref, qg_ref, o_ref):
    tq = q_ref.shape[0]
    lane = _lane_iota((tq, LANES))
    lo = lane < HEAD_DIM
    for p in range(MEM_WIDTH // LANES):
        cols = slice(p * LANES, (p + 1) * LANES)
        qn = _pair_rms(q_ref[:, cols], qg_ref[...], lo) * QK_SCALE
        kt = k_ref[0, :, cols]
        vt = v_ref[0, :, cols]
        outs = []
        for h in range(2):
            qh = jnp.where(lo, qn, 0.0) if h == 0 else jnp.where(lo, 0.0, qn)
            s = lax.dot_general(qh.astype(BF16), kt, _NT, preferred_element_type=F32)
            m = jnp.max(s, axis=-1, keepdims=True)
            e = jnp.exp(s - m)
            pr = e / jnp.sum(e, axis=-1, keepdims=True)
            outs.append(jnp.dot(pr.astype(BF16), vt, preferred_element_type=F32))
        o_ref[:, cols] = jnp.where(lo, outs[0], outs[1]).astype(o_ref.dtype)


def _mem_attn(proj, km, vm, q_gain, batch, seq, tq):
    m = km.shape[1]
    qcol = (IN_WIDTH - MEM_WIDTH) // MEM_WIDTH
    assert qcol * MEM_WIDTH == IN_WIDTH - MEM_WIDTH
    nq = seq // tq
    return pl.pallas_call(
        _mem_attn_kernel,
        grid=(batch, nq),
        in_specs=[pl.BlockSpec((tq, MEM_WIDTH), lambda b, i: (b * nq + i, qcol)),
                  pl.BlockSpec((1, m, MEM_WIDTH), lambda b, i: (b, 0, 0)),
                  pl.BlockSpec((1, m, MEM_WIDTH), lambda b, i: (b, 0, 0)),
                  pl.BlockSpec((1, LANES), lambda b, i: (0, 0))],
        out_specs=pl.BlockSpec((tq, MEM_WIDTH), lambda b, i: (b * nq + i, 0)),
        out_shape=jax.ShapeDtypeStruct((batch * seq, MEM_WIDTH), MIX_DTYPE),
        compiler_params=_params("parallel", "parallel"),
    )(proj, km, vm, jnp.tile(q_gain, 2).reshape(1, LANES))


def _out_router_kernel(oa_ref, or_ref, om_ref, x_ref, wo_ref, g_ref, wr_ref, br_ref,
                       x2_ref, h2_ref, aux_ref, cnt_ref, run_ref):
    tm = x_ref.shape[0]

    @pl.when(pl.program_id(0) == 0)
    def _():
        run_ref[...] = jnp.zeros_like(run_ref)

    y = jnp.dot(oa_ref[...].astype(BF16), wo_ref[0:MOBA_WIDTH, :], preferred_element_type=F32)
    y += jnp.dot(or_ref[...].astype(BF16), wo_ref[MOBA_WIDTH:MOBA_WIDTH + RET_WIDTH, :],
                 preferred_element_type=F32)
    y += jnp.dot(om_ref[...].astype(BF16), wo_ref[MOBA_WIDTH + RET_WIDTH:, :],
                 preferred_element_type=F32)
    x2 = x_ref[...] + y
    x2_ref[...] = x2
    ms = jnp.mean(x2 * x2, axis=-1, keepdims=True)
    h2 = x2 * lax.rsqrt(ms + EPS) * g_ref[...]
    h2b = h2.astype(BF16)
    h2_ref[...] = h2b
    logits = jnp.dot(h2b, wr_ref[...], preferred_element_type=F32) + br_ref[...]
    lane = _lane_iota((tm, LANES))
    lane_f = lane.astype(F32)
    lg = jnp.where(lane < N_EXPERTS, logits, -jnp.inf)
    vals, picks = [], []
    for _ in range(TOP_K):
        m = jnp.max(lg, axis=-1, keepdims=True)
        first = jnp.min(jnp.where(lg == m, lane_f, jnp.inf), axis=-1, keepdims=True)
        pick = lane_f == first
        vals.append(m)
        picks.append(pick)
        lg = jnp.where(pick, -jnp.inf, lg)
    exps = [jnp.exp(v - vals[0]) for v in vals]
    denom = exps[0] + exps[1] + exps[2] + exps[3]
    sel = jnp.zeros((tm, LANES), F32)
    for pick in picks:
        sel = sel + jnp.where(pick, 1.0, 0.0)
    strict = (lax.broadcasted_iota(jnp.int32, (tm, tm), 0)
              > lax.broadcasted_iota(jnp.int32, (tm, tm), 1))
    pos = run_ref[0:1, :] + jnp.dot(jnp.where(strict, 1.0, 0.0).astype(BF16), sel.astype(BF16),
                                    preferred_element_type=F32)
    aux = jnp.zeros((tm, LANES), F32)
    for k in range(TOP_K):
        eid = jnp.sum(jnp.where(picks[k], lane_f, 0.0), axis=-1, keepdims=True)
        rank = jnp.sum(jnp.where(picks[k], pos, 0.0), axis=-1, keepdims=True)
        aux = aux + jnp.where(lane == k, eid, 0.0)
        aux = aux + jnp.where(lane == TOP_K + k, exps[k] / denom, 0.0)
        aux = aux + jnp.where(lane == 2 * TOP_K + k, rank, 0.0)
    aux_ref[...] = aux
    run = run_ref[0:1, :] + jnp.sum(sel, axis=0, keepdims=True)
    run_ref[...] = jnp.broadcast_to(run, run_ref.shape)
    cnt_ref[...] = jnp.broadcast_to(run, cnt_ref.shape)


def _out_router(oa, orr, om, x2d, w_out, g_ffn, w_router, b_router, tm):
    t, d = x2d.shape
    wr = jnp.zeros((d, LANES), BF16).at[:, :N_EXPERTS].set(w_router.astype(BF16))
    br = jnp.zeros((1, LANES), F32).at[0, :N_EXPERTS].set(b_router)
    row = lambda w: pl.BlockSpec((tm, w), lambda i: (i, 0))
    const = lambda r, c: pl.BlockSpec((r, c), lambda i: (0, 0))
    return pl.pallas_call(
        _out_router_kernel,
        grid=(t // tm,),
        in_specs=[row(MOBA_WIDTH), row(RET_WIDTH), row(MEM_WIDTH), row(d),
                  const(d, d), const(1, d), const(d, LANES), const(1, LANES)],
        out_specs=[row(d), row(d), row(LANES), const(8, LANES)],
        out_shape=[jax.ShapeDtypeStruct((t, d), F32), jax.ShapeDtypeStruct((t, d), BF16),
                   jax.ShapeDtypeStruct((t, LANES), F32), jax.ShapeDtypeStruct((8, LANES), F32)],
        scratch_shapes=[pltpu.VMEM((8, LANES), F32)],
        compiler_params=_params("arbitrary"),
    )(oa, orr, om, x2d, w_out.astype(BF16), g_ffn.reshape(1, d), wr, br)


MXU_COLS = 256


def _moe_kernel(be_ref, nu_ref, xs_ref, w1_ref, b1g_ref, b1l_ref, w2_ref, b2_ref, o_ref,
                w1p_ref, w2b_ref, act_ref):
    i = pl.program_id(0)
    half = MXU_COLS // 2
    n_chunks = w1_ref.shape[2] // MXU_COLS

    @pl.when(i < nu_ref[0])
    def _():
        @pl.when(jnp.logical_or(i == 0, be_ref[i] != be_ref[jnp.maximum(i - 1, 0)]))
        def _():
            r = lax.broadcasted_iota(jnp.int32, (MXU_COLS, MXU_COLS), 0)
            c = lax.broadcasted_iota(jnp.int32, (MXU_COLS, MXU_COLS), 1)
            src = jnp.where(c < half, 2 * c, 2 * (c - half) + 1)
            perm = jnp.where(r == src, 1.0, 0.0).astype(BF16)
            for ch in range(n_chunks):
                cols = slice(ch * MXU_COLS, (ch + 1) * MXU_COLS)
                w = w1_ref[0, :, cols].astype(BF16)
                w1p_ref[:, cols] = jnp.dot(w, perm, preferred_element_type=F32).astype(BF16)
            w2b_ref[...] = w2_ref[0].astype(BF16)

        x = xs_ref[...]
        for ch in range(n_chunks):
            z = jnp.dot(x, w1p_ref[:, ch * MXU_COLS:(ch + 1) * MXU_COLS], preferred_element_type=F32)
            hs = slice(ch * half, (ch + 1) * half)
            xg = jnp.minimum(z[:, :half] + b1g_ref[0][:, hs], SWIGLU_LIMIT)
            xl = jnp.clip(z[:, half:] + b1l_ref[0][:, hs], -SWIGLU_LIMIT, SWIGLU_LIMIT)
            act_ref[:, hs] = (xg * jax.nn.sigmoid(SWIGLU_ALPHA * xg) * (xl + 1.0)).astype(BF16)
        y = jnp.dot(act_ref[...], w2b_ref[...], preferred_element_type=F32) + b2_ref[0]
        o_ref[...] = y.astype(o_ref.dtype)

    @pl.when(i >= nu_ref[0])
    def _():
        o_ref[...] = jnp.zeros_like(o_ref)


def _moe_experts(xs, blk_exp, n_used, w1, b1g, b1l, w2, b2):
    p_rows, d = xs.shape
    d_ff = w2.shape[1]
    assert w1.shape[2] == 2 * d_ff and (2 * d_ff) % MXU_COLS == 0
    n_blocks = p_rows // MOE_BLOCK
    rows = lambda i, be, nu: (jnp.minimum(i, nu[0] - 1), 0)
    wsel = lambda i, be, nu: (be[i], 0, 0)
    grid_spec = pltpu.PrefetchScalarGridSpec(
        num_scalar_prefetch=2,
        grid=(n_blocks,),
        in_specs=[pl.BlockSpec((MOE_BLOCK, d), rows),
                  pl.BlockSpec((1, d, 2 * d_ff), wsel),
                  pl.BlockSpec((1, 1, d_ff), wsel), pl.BlockSpec((1, 1, d_ff), wsel),
                  pl.BlockSpec((1, d_ff, d), wsel), pl.BlockSpec((1, 1, d), wsel)],
        out_specs=pl.BlockSpec((MOE_BLOCK, d), lambda i, be, nu: (i, 0)),
        scratch_shapes=[pltpu.VMEM((d, 2 * d_ff), BF16), pltpu.VMEM((d_ff, d), BF16),
                        pltpu.VMEM((MOE_BLOCK, d_ff), BF16)],
    )
    return pl.pallas_call(
        _moe_kernel,
        grid_spec=grid_spec,
        out_shape=jax.ShapeDtypeStruct((p_rows, d), MOE_OUT_DTYPE),
        compiler_params=pltpu.CompilerParams(dimension_semantics=("arbitrary",),
                                             vmem_limit_bytes=MOE_VMEM_LIMIT),
    )(blk_exp, n_used, xs, w1, b1g, b1l, w2, b2)


def _combine_kernel(x2_ref, aux_ref, yg_ref, o_ref):
    out = x2_ref[...]
    aux = aux_ref[...]
    for k in range(TOP_K):
        out = out + aux[:, TOP_K + k:TOP_K + k + 1] * yg_ref[k].astype(F32)
    o_ref[...] = out


def _combine(x2, aux, yg, tm):
    t, d = x2.shape
    return pl.pallas_call(
        _combine_kernel,
        grid=(t // tm,),
        in_specs=[pl.BlockSpec((tm, d), lambda i: (i, 0)),
                  pl.BlockSpec((tm, LANES), lambda i: (i, 0)),
                  pl.BlockSpec((TOP_K, tm, d), lambda i: (0, i, 0))],
        out_specs=pl.BlockSpec((tm, d), lambda i: (i, 0)),
        out_shape=jax.ShapeDtypeStruct((t, d), F32),
        compiler_params=_params("parallel"),
    )(x2, aux, yg)


def _layer(x, mem, g_attn, w_in, moba_q_gain, moba_k_gain, ret_out_gain, g_mem, w_mem_kv,
           mem_q_gain, mem_k_gain, w_out, g_ffn, w_router, b_router, w1, b1, w2, b2):
    batch, seq, d = x.shape
    t = batch * seq
    x2d = x.reshape(t, d)
    tm = min(512, t)

    proj = _rms_proj(x2d, g_attn, w_in.astype(BF16), tm)
    oa = _moba(proj, moba_q_gain, moba_k_gain, batch, seq)
    orr = _retention(proj, ret_out_gain, batch, seq)
    km, vm = _mem_kv(mem, g_mem, w_mem_kv, mem_k_gain)
    om = _mem_attn(proj, km, vm, mem_q_gain, batch, seq, min(512, seq))
    x2, h2, aux, counts = _out_router(oa, orr, om, x2d, w_out, g_ffn, w_router, b_router, tm)

    eidx = aux[:, 0:TOP_K].astype(jnp.int32)
    rank = aux[:, 2 * TOP_K:3 * TOP_K].astype(jnp.int32)
    cnt = counts[0, :N_EXPERTS].astype(jnp.int32)
    padded = ((cnt + MOE_BLOCK - 1) // MOE_BLOCK) * MOE_BLOCK
    pends = jnp.cumsum(padded)
    pstarts = pends - padded
    dest = pstarts[eidx] + rank
    a = t * TOP_K
    n_blocks = -(-(a + N_EXPERTS * (MOE_BLOCK - 1)) // MOE_BLOCK)
    p_rows = n_blocks * MOE_BLOCK
    blk_start = jnp.arange(n_blocks, dtype=jnp.int32) * MOE_BLOCK
    blk_exp = jnp.minimum(jnp.sum(blk_start[:, None] >= pends[None, :], axis=1),
                          N_EXPERTS - 1).astype(jnp.int32)
    n_used = (pends[-1] // MOE_BLOCK).astype(jnp.int32).reshape(1)

    row_tok = jnp.zeros((p_rows,), jnp.int32).at[dest.reshape(a)].set(
        jnp.arange(a, dtype=jnp.int32) // TOP_K, unique_indices=True, mode="promise_in_bounds")
    xs = h2.at[row_tok].get(mode="promise_in_bounds")

    d_ff = w2.shape[1]
    b1g = b1[:, 0::2].reshape(N_EXPERTS, 1, d_ff)
    b1l = b1[:, 1::2].reshape(N_EXPERTS, 1, d_ff)
    ys = _moe_experts(xs, blk_exp, n_used, w1, b1g, b1l, w2, b2.reshape(N_EXPERTS, 1, d))
    yg = ys.at[dest.T.reshape(a)].get(mode="promise_in_bounds").reshape(TOP_K, t, d)
    out = _combine(x2, aux, yg, min(256, t))
    return out.reshape(batch, seq, d)


def kernel(x, mem, g_attn, w_in, moba_q_gain, moba_k_gain, ret_out_gain, g_mem, w_mem_kv,
           mem_q_gain, mem_k_gain, w_out, g_ffn, w_router, b_router, w1, b1, w2, b2):
    for l in range(g_attn.shape[0]):
        x = _layer(x, mem, g_attn[l], w_in[l], moba_q_gain[l], moba_k_gain[l], ret_out_gain[l],
                   g_mem[l], w_mem_kv[l], mem_q_gain[l], mem_k_gain[l], w_out[l], g_ffn[l],
                   w_router[l], b_router[l], w1[l], b1[l], w2[l], b2[l])
    return x
```

```python
import functools

import jax
import jax.numpy as jnp
from jax import lax
from jax.experimental import pallas as pl
from jax.experimental.pallas import tpu as pltpu
from jax.experimental.pallas import tpu_sc as plsc

F32 = jnp.float32
BF16 = jnp.bfloat16

LANES = 128
HEAD_DIM = 64
MOBA_HEADS = 8
RET_HEADS = 4
MEM_HEADS = 4
MOBA_WIDTH = MOBA_HEADS * HEAD_DIM
RET_WIDTH = RET_HEADS * HEAD_DIM
MEM_WIDTH = MEM_HEADS * HEAD_DIM
IN_WIDTH = 3 * MOBA_WIDTH + 4 * RET_WIDTH + MEM_WIDTH
MOBA_BLOCK = 256
MOBA_TOPK = 3
RET_CHUNK = 128
RET_ROPE_BASE = 10000.0
N_EXPERTS = 32
TOP_K = 4
SWIGLU_ALPHA = 1.702
SWIGLU_LIMIT = 7.0
MOE_BLOCK = 512
EPS = 1e-6
NEG = -1e30
QK_SCALE = HEAD_DIM ** -0.5

VMEM_LIMIT = 48 * 1024 * 1024
MOE_VMEM_LIMIT = 56 * 1024 * 1024
MIX_DTYPE = BF16
SC_CORES = 2
SC_SUBCORES = 16
SC_GATHER_ROWS = 128

_NT = (((1,), (1,)), ((), ()))
_TN = (((0,), (0,)), ((), ()))


def _params(*sem):
    return pltpu.CompilerParams(dimension_semantics=sem, vmem_limit_bytes=VMEM_LIMIT)


def _lane_iota(shape):
    return lax.broadcasted_iota(jnp.int32, shape, len(shape) - 1)


def _pair_rms(t, gain, lo):
    t2 = t * t
    s0 = jnp.sum(jnp.where(lo, t2, 0.0), axis=-1, keepdims=True)
    s1 = jnp.sum(jnp.where(lo, 0.0, t2), axis=-1, keepdims=True)
    r = jnp.where(lo, lax.rsqrt(s0 / HEAD_DIM + EPS), lax.rsqrt(s1 / HEAD_DIM + EPS))
    return t * r * gain


def _rms_proj_kernel(x_ref, g_ref, w_ref, o_ref):
    x = x_ref[...]
    ms = jnp.mean(x * x, axis=-1, keepdims=True)
    h = (x * lax.rsqrt(ms + EPS) * g_ref[...]).astype(BF16)
    o_ref[...] = jnp.dot(h, w_ref[...], preferred_element_type=F32)


def _rms_proj(x2d, gain, w, tm):
    t, d = x2d.shape
    n = w.shape[1]
    return pl.pallas_call(
        _rms_proj_kernel,
        grid=(t // tm,),
        in_specs=[pl.BlockSpec((tm, d), lambda i: (i, 0)),
                  pl.BlockSpec((1, d), lambda i: (0, 0)),
                  pl.BlockSpec((d, n), lambda i: (0, 0))],
        out_specs=pl.BlockSpec((tm, n), lambda i: (i, 0)),
        out_shape=jax.ShapeDtypeStruct((t, n), F32),
        compiler_params=_params("parallel"),
    )(x2d, gain.reshape(1, d), w)


SHIFT_SAFE = 80.0
MOBA_GROUP = 4


def _moba_kernel(tab_ref, q_ref, k_ref, v_ref, qg_ref, kg_ref, m_ref, o_ref,
                 ka_ref, va_ref, qa_ref, gt_ref, acc_ref, pa_ref, pb_ref, *, n_blk, n_trips):
    blk = MOBA_BLOCK
    grp = MOBA_GROUP
    nb8 = -(-n_blk // 8) * 8
    lane = _lane_iota((blk, LANES))
    lo = lane < HEAD_DIM
    lane1 = _lane_iota((1, LANES))
    qg = qg_ref[...]
    kg = kg_ref[...]
    tri = (lax.broadcasted_iota(jnp.int32, (blk, blk), 0)
           >= lax.broadcasted_iota(jnp.int32, (blk, blk), 1))
    row_t = lax.broadcasted_iota(jnp.int32, (nb8, blk), 0)
    row_tf = row_t.astype(F32)
    fill_row = lax.broadcasted_iota(jnp.int32, (HEAD_DIM - nb8, blk), 0)
    filler = jnp.where(fill_row == HEAD_DIM - nb8 - 1, -m_ref[...], 0.0)

    def rows_of(j):
        return pl.ds(pl.multiple_of(j * blk, blk), blk)

    gt_ref[...] = jnp.zeros_like(gt_ref)

    def prep_k(j, carry):
        rows = rows_of(j)
        kn = _pair_rms(k_ref[rows, :], kg, lo)
        km = jnp.mean(kn, axis=0, keepdims=True)
        gt_ref[pl.ds(HEAD_DIM + j, 1), :] = jnp.where(lane1 < HEAD_DIM, km, 0.0)
        gt_ref[pl.ds(j, 1), :] = jnp.where(lane1 < HEAD_DIM, 0.0, km)
        tag0 = jnp.where(jnp.logical_or(lane == HEAD_DIM + j, lane == LANES - 1), 1.0, 0.0)
        tag1 = jnp.where(jnp.logical_or(lane == j, lane == HEAD_DIM - 1), 1.0, 0.0)
        ka_ref[0, rows, :] = jnp.where(lo, kn, tag0).astype(BF16)
        ka_ref[1, rows, :] = jnp.where(lo, tag1, kn).astype(BF16)
        v = v_ref[rows, :]
        va_ref[0, rows, :] = jnp.where(lo, v, jnp.where(lane == HEAD_DIM, 1.0, 0.0)).astype(BF16)
        va_ref[1, rows, :] = jnp.where(lo, jnp.where(lane == 0, 1.0, 0.0), v).astype(BF16)
        return carry

    lax.fori_loop(0, n_blk, prep_k, 0)

    def block_bias(g, n):
        valid = row_t < n
        g = jnp.where(valid, g, NEG)
        sel = jnp.zeros((nb8, blk), jnp.bool_)
        for _ in range(MOBA_TOPK):
            m = jnp.max(g, axis=0, keepdims=True)
            first = jnp.min(jnp.where(g == m, row_tf, jnp.inf), axis=0, keepdims=True)
            pick = row_tf == first
            sel = jnp.logical_or(sel, pick)
            g = jnp.where(pick, -jnp.inf, g)
        keep = jnp.logical_or(jnp.logical_and(sel, valid), row_t == n)
        return jnp.where(keep, 0.0, NEG)

    def prep_q(n, carry):
        rows_n = rows_of(n)
        qn = _pair_rms(q_ref[rows_n, :], qg, lo)
        gate_t = lax.dot_general(gt_ref[...].astype(BF16), qn.astype(BF16), _NT,
                                 preferred_element_type=F32)
        bias_t = jnp.concatenate([block_bias(gate_t[0:nb8], n), filler,
                                  block_bias(gate_t[HEAD_DIM:HEAD_DIM + nb8], n), filler], axis=0)
        bias = bias_t.T
        qs = qn * QK_SCALE
        qa_ref[0, rows_n, :] = jnp.where(lo, qs, bias).astype(BF16)
        qa_ref[1, rows_n, :] = jnp.where(lo, bias, qs).astype(BF16)
        return carry

    lax.fori_loop(0, n_blk, prep_q, 0)

    def scores(h, n, j):
        s = lax.dot_general(qa_ref[h, rows_of(n), :], ka_ref[h, rows_of(j), :], _NT,
                            preferred_element_type=F32)
        return jnp.where(jnp.logical_or(tri, j != n), s, NEG)

    def normalise(n, a0, a1):
        o_ref[rows_of(n), :] = jnp.where(lo, a0 / a0[:, HEAD_DIM:HEAD_DIM + 1],
                                         a1 / a1[:, 0:1]).astype(o_ref.dtype)

    @pl.when(tab_ref[0, 0] == 1)
    def _():
        acc_ref[...] = jnp.zeros_like(acc_ref)

        def make_probs(trip, dst_ref):
            for g in range(grp):
                n = tab_ref[1, trip * grp + g]
                j = tab_ref[2, trip * grp + g]
                for h in range(2):
                    dst_ref[g, h] = jnp.exp(scores(h, n, j)).astype(BF16)

        def apply_probs(trip, src_ref):
            for g in range(grp):
                n = tab_ref[1, trip * grp + g]
                j = tab_ref[2, trip * grp + g]
                for h in range(2):
                    acc_ref[h, rows_of(n), :] += jnp.dot(src_ref[g, h], va_ref[h, rows_of(j), :],
                                                         preferred_element_type=F32)

        make_probs(0, pa_ref)

        def two_trips(i, carry):
            apply_probs(2 * i, pa_ref)
            make_probs(2 * i + 1, pb_ref)
            apply_probs(2 * i + 1, pb_ref)
            make_probs(2 * i + 2, pa_ref)
            return carry

        lax.fori_loop(0, n_trips // 2, two_trips, 0)

        def fin(n, carry):
            normalise(n, acc_ref[0, rows_of(n), :], acc_ref[1, rows_of(n), :])
            return carry

        lax.fori_loop(0, n_blk, fin, 0)

    @pl.when(tab_ref[0, 0] != 1)
    def _():
        def qblock(n, carry):
            def one(j, st):
                out = []
                for h in range(2):
                    m, acc = st[2 * h], st[2 * h + 1]
                    s = scores(h, n, j)
                    m_new = jnp.maximum(m, jnp.max(s, axis=-1, keepdims=True))
                    p = jnp.exp(s - m_new).astype(BF16)
                    acc = jnp.exp(m - m_new) * acc + jnp.dot(p, va_ref[h, rows_of(j), :],
                                                             preferred_element_type=F32)
                    out += [m_new, acc]
                return tuple(out)

            zero = jnp.zeros((blk, LANES), F32)
            ninf = jnp.full((blk, 1), -jnp.inf, F32)
            st = lax.fori_loop(0, n + 1, one, (ninf, zero, ninf, zero))
            normalise(n, st[1], st[3])
            return carry

        lax.fori_loop(0, n_blk, qblock, 0)


def _moba_tiles(n_blk):
    tiles = [(n, j) for n in range(n_blk) for j in range(n + 1)]
    per_two = 2 * MOBA_GROUP
    n_trips = 2 * (-(-len(tiles) // per_two))
    tiles += [(0, 1)] * ((n_trips + 1) * MOBA_GROUP - len(tiles))
    return n_trips, tiles


def _moba(proj, q_gain, k_gain, batch, seq):
    n_blk = seq // MOBA_BLOCK
    assert seq % MOBA_BLOCK == 0 and MOBA_TOPK <= n_blk <= HEAD_DIM - 8
    n_pairs = MOBA_WIDTH // LANES
    qg = jnp.tile(q_gain, 2).reshape(1, LANES)
    kg = jnp.tile(k_gain, 2).reshape(1, LANES)
    shift = HEAD_DIM * QK_SCALE * jnp.max(jnp.abs(q_gain)) * jnp.max(jnp.abs(k_gain))
    flag = (2.0 * shift <= SHIFT_SAFE).astype(jnp.int32)
    n_trips, tiles = _moba_tiles(n_blk)
    tab = jnp.stack([jnp.full((len(tiles),), flag, jnp.int32),
                     jnp.asarray([t[0] for t in tiles], jnp.int32),
                     jnp.asarray([t[1] for t in tiles], jnp.int32)])
    mrow = jnp.full((1, MOBA_BLOCK), shift, F32)
    blk = lambda off: pl.BlockSpec((seq, LANES), lambda b, p, f: (b, off + p))
    vec = lambda w: pl.BlockSpec((1, w), lambda b, p, f: (0, 0))
    prob = pltpu.VMEM((MOBA_GROUP, 2, MOBA_BLOCK, MOBA_BLOCK), BF16)
    grid_spec = pltpu.PrefetchScalarGridSpec(
        num_scalar_prefetch=1,
        grid=(batch, n_pairs),
        in_specs=[blk(0), blk(n_pairs), blk(2 * n_pairs), vec(LANES), vec(LANES), vec(MOBA_BLOCK)],
        out_specs=pl.BlockSpec((seq, LANES), lambda b, p, f: (b, p)),
        scratch_shapes=[pltpu.VMEM((2, seq, LANES), BF16),
                        pltpu.VMEM((2, seq, LANES), BF16),
                        pltpu.VMEM((2, seq, LANES), BF16),
                        pltpu.VMEM((LANES, LANES), F32),
                        pltpu.VMEM((2, seq, LANES), F32),
                        prob, prob],
    )
    return pl.pallas_call(
        functools.partial(_moba_kernel, n_blk=n_blk, n_trips=n_trips),
        grid_spec=grid_spec,
        out_shape=jax.ShapeDtypeStruct((batch * seq, MOBA_WIDTH), MIX_DTYPE),
        compiler_params=_params("parallel", "parallel"),
    )(tab, proj, proj, proj, qg, kg, mrow)


def _ret_kernel(q_ref, k_ref, v_ref, g_ref, cos_ref, sin_ref, intra_ref, kdec_ref, qdec_ref,
                cdm_ref, mask_ref, gain_ref, o_ref, *, n_ch):
    c = RET_CHUNK
    lane = _lane_iota((c, LANES))
    lo = lane < HEAD_DIM
    first_half = (lane % HEAD_DIM) < (HEAD_DIM // 2)
    gain = gain_ref[...]
    intra0 = intra_ref[0, 0]
    intra1 = intra_ref[0, 1]
    kdec = kdec_ref[0]
    qdec = qdec_ref[0]
    cdm = cdm_ref[0]
    mask = mask_ref[...]

    def rope(t, cos_t, sin_t):
        partner = jnp.where(first_half, pltpu.roll(t, LANES - HEAD_DIM // 2, 1),
                            pltpu.roll(t, HEAD_DIM // 2, 1))
        return t * cos_t + partner * sin_t

    def chunk(i, state):
        rows = pl.ds(pl.multiple_of(i * c, c), c)
        cos_t = cos_ref[rows, :]
        sin_t = sin_ref[rows, :]
        q = rope(q_ref[rows, :], cos_t, sin_t)
        k = rope(k_ref[rows, :], cos_t, sin_t) * QK_SCALE
        v = v_ref[rows, :]
        kb = k.astype(BF16)
        s0 = lax.dot_general(jnp.where(lo, q, 0.0).astype(BF16), kb, _NT,
                             preferred_element_type=F32) * intra0
        s1 = lax.dot_general(jnp.where(lo, 0.0, q).astype(BF16), kb, _NT,
                             preferred_element_type=F32) * intra1
        sc = jnp.concatenate([s0, s1], axis=1).astype(BF16)
        vv = jnp.concatenate([jnp.where(lo, v, 0.0), jnp.where(lo, 0.0, v)], axis=0).astype(BF16)
        o = jnp.dot(sc, vv, preferred_element_type=F32)
        o = o + jnp.dot(q.astype(BF16), state.astype(BF16), preferred_element_type=F32) * qdec
        kd = (k * kdec).astype(BF16)
        kv = lax.dot_general(kd, v.astype(BF16), _TN, preferred_element_type=F32)
        state = cdm * state + mask * kv
        o2 = o * o
        m0 = jnp.sum(jnp.where(lo, o2, 0.0), axis=-1, keepdims=True)
        m1 = jnp.sum(jnp.where(lo, 0.0, o2), axis=-1, keepdims=True)
        r = jnp.where(lo, lax.rsqrt(m0 / HEAD_DIM + EPS), lax.rsqrt(m1 / HEAD_DIM + EPS))
        gt = g_ref[rows, :]
        o_ref[rows, :] = ((gt * jax.nn.sigmoid(gt)) * (o * r * gain)).astype(o_ref.dtype)
        return state

    lax.fori_loop(0, n_ch, chunk, jnp.zeros((LANES, LANES), F32))


def _ret_tables(seq):
    c = RET_CHUNK
    half = HEAD_DIM // 2
    inv_freq = 1.0 / (RET_ROPE_BASE ** jnp.linspace(0.0, 1.0, half, dtype=F32))
    ang = jnp.arange(seq, dtype=F32)[:, None] * inv_freq[None, :]
    cos, sin = jnp.cos(ang), jnp.sin(ang)
    cos_t = jnp.tile(cos, (1, LANES // half))
    sin_t = jnp.tile(jnp.concatenate([-sin, sin], axis=1), (1, LANES // HEAD_DIM))
    log_gamma = jnp.log1p(-jnp.exp2(-5.0 - jnp.arange(RET_HEADS, dtype=F32)))
    idx = jnp.arange(c, dtype=F32)
    diff = idx[:, None] - idx[None, :]
    intra = jnp.where(diff >= 0, jnp.exp(log_gamma[:, None, None] * jnp.maximum(diff, 0.0)), 0.0)
    k_decay = jnp.exp(log_gamma[:, None] * (c - 1.0 - idx)[None, :])
    q_decay = jnp.exp(log_gamma[:, None] * (idx + 1.0)[None, :])
    chunk_decay = jnp.exp(log_gamma * c)
    n_pairs = RET_HEADS // 2

    def lanes(t):
        return jnp.repeat(t.reshape(n_pairs, 2, c).transpose(0, 2, 1), HEAD_DIM, axis=2)

    blockdiag = (jnp.arange(LANES)[:, None] // HEAD_DIM) == (jnp.arange(LANES)[None, :] // HEAD_DIM)
    mask = blockdiag.astype(F32)
    cdm = jnp.repeat(chunk_decay.reshape(n_pairs, 2), HEAD_DIM, axis=1)[:, :, None] * mask[None]
    return (cos_t, sin_t, intra.reshape(n_pairs, 2, c, c), lanes(k_decay), lanes(q_decay), cdm, mask)


def _retention(proj, out_gain, batch, seq):
    c = RET_CHUNK
    assert seq % c == 0
    n_pairs = RET_HEADS // 2
    base = 3 * MOBA_WIDTH // LANES
    cos_t, sin_t, intra, kdec, qdec, cdm, mask = _ret_tables(seq)
    blk = lambda off: pl.BlockSpec((seq, LANES), lambda b, p: (b, base + off + p))
    tab = pl.BlockSpec((seq, LANES), lambda b, p: (0, 0))
    return pl.pallas_call(
        functools.partial(_ret_kernel, n_ch=seq // c),
        grid=(batch, n_pairs),
        in_specs=[blk(0), blk(n_pairs), blk(2 * n_pairs), blk(3 * n_pairs), tab, tab,
                  pl.BlockSpec((1, 2, c, c), lambda b, p: (p, 0, 0, 0)),
                  pl.BlockSpec((1, c, LANES), lambda b, p: (p, 0, 0)),
                  pl.BlockSpec((1, c, LANES), lambda b, p: (p, 0, 0)),
                  pl.BlockSpec((1, LANES, LANES), lambda b, p: (p, 0, 0)),
                  pl.BlockSpec((LANES, LANES), lambda b, p: (0, 0)),
                  pl.BlockSpec((1, LANES), lambda b, p: (0, p))],
        out_specs=pl.BlockSpec((seq, LANES), lambda b, p: (b, p)),
        out_shape=jax.ShapeDtypeStruct((batch * seq, RET_WIDTH), MIX_DTYPE),
        compiler_params=_params("parallel", "parallel"),
    )(proj, proj, proj, proj, cos_t, sin_t, intra, kdec, qdec, cdm, mask,
      out_gain.reshape(1, RET_WIDTH))


def _mem_kv_kernel(m_ref, g_ref, w_ref, kg_ref, k_ref, v_ref):
    x = m_ref[0]
    ms = jnp.mean(x * x, axis=-1, keepdims=True)
    h = (x * lax.rsqrt(ms + EPS) * g_ref[...]).astype(BF16)
    kv = jnp.dot(h, w_ref[...], preferred_element_type=F32)
    lane = _lane_iota((x.shape[0], LANES))
    lo = lane < HEAD_DIM
    for p in range(MEM_WIDTH // LANES):
        kt = kv[:, p * LANES:(p + 1) * LANES]
        k_ref[0, :, p * LANES:(p + 1) * LANES] = _pair_rms(kt, kg_ref[...], lo).astype(BF16)
    v_ref[0] = kv[:, MEM_WIDTH:].astype(BF16)


def _mem_kv(mem, g_mem, w_mem_kv, k_gain):
    b, m, d = mem.shape
    return pl.pallas_call(
        _mem_kv_kernel,
        grid=(b,),
        in_specs=[pl.BlockSpec((1, m, d), lambda i: (i, 0, 0)),
                  pl.BlockSpec((1, d), lambda i: (0, 0)),
                  pl.BlockSpec((d, 2 * MEM_WIDTH), lambda i: (0, 0)),
                  pl.BlockSpec((1, LANES), lambda i: (0, 0))],
        out_specs=[pl.BlockSpec((1, m, MEM_WIDTH), lambda i: (i, 0, 0)),
                   pl.BlockSpec((1, m, MEM_WIDTH), lambda i: (i, 0, 0))],
        out_shape=[jax.ShapeDtypeStruct((b, m, MEM_WIDTH), BF16),
                   jax.ShapeDtypeStruct((b, m, MEM_WIDTH), BF16)],
        compiler_params=_params("parallel"),
    )(mem, g_mem.reshape(1, d), w_mem_kv.astype(BF16), jnp.tile(k_gain, 2).reshape(1, LANES))


def _mem_attn_kernel(q_ref, k_ref, v_ref, qg_ref, o_ref):
    tq = q_ref.shape[0]
    lane = _lane_iota((tq, LANES))
    lo = lane < HEAD_DIM
    for p in range(MEM_WIDTH // LANES):
        cols = slice(p * LANES, (p + 1) * LANES)
        qn = _pair_rms(q_ref[:, cols], qg_ref[...], lo) * QK_SCALE
        kt = k_ref[0, :, cols]
        vt = v_ref[0, :, cols]
        outs = []
        for h in range(2):
            qh = jnp.where(lo, qn, 0.0) if h == 0 else jnp.where(lo, 0.0, qn)
            s = lax.dot_general(qh.astype(BF16), kt, _NT, preferred_element_type=F32)
            m = jnp.max(s, axis=-1, keepdims=True)
            e = jnp.exp(s - m)
            pr = e / jnp.sum(e, axis=-1, keepdims=True)
            outs.append(jnp.dot(pr.astype(BF16), vt, preferred_element_type=F32))
        o_ref[:, cols] = jnp.where(lo, outs[0], outs[1]).astype(o_ref.dtype)


def _mem_attn(proj, km, vm, q_gain, batch, seq, tq):
    m = km.shape[1]
    qcol = (IN_WIDTH - MEM_WIDTH) // MEM_WIDTH
    assert qcol * MEM_WIDTH == IN_WIDTH - MEM_WIDTH
    nq = seq // tq
    return pl.pallas_call(
        _mem_attn_kernel,
        grid=(batch, nq),
        in_specs=[pl.BlockSpec((tq, MEM_WIDTH), lambda b, i: (b * nq + i, qcol)),
                  pl.BlockSpec((1, m, MEM_WIDTH), lambda b, i: (b, 0, 0)),
                  pl.BlockSpec((1, m, MEM_WIDTH), lambda b, i: (b, 0, 0)),
                  pl.BlockSpec((1, LANES), lambda b, i: (0, 0))],
        out_specs=pl.BlockSpec((tq, MEM_WIDTH), lambda b, i: (b * nq + i, 0)),
        out_shape=jax.ShapeDtypeStruct((batch * seq, MEM_WIDTH), MIX_DTYPE),
        compiler_params=_params("parallel", "parallel"),
    )(proj, km, vm, jnp.tile(q_gain, 2).reshape(1, LANES))


def _out_router_kernel(oa_ref, or_ref, om_ref, x_ref, wo_ref, g_ref, wr_ref, br_ref,
                       x2_ref, h2_ref, aux_ref, cnt_ref, run_ref):
    tm = x_ref.shape[0]

    @pl.when(pl.program_id(0) == 0)
    def _():
        run_ref[...] = jnp.zeros_like(run_ref)

    y = jnp.dot(oa_ref[...].astype(BF16), wo_ref[0:MOBA_WIDTH, :], preferred_element_type=F32)
    y += jnp.dot(or_ref[...].astype(BF16), wo_ref[MOBA_WIDTH:MOBA_WIDTH + RET_WIDTH, :],
                 preferred_element_type=F32)
    y += jnp.dot(om_ref[...].astype(BF16), wo_ref[MOBA_WIDTH + RET_WIDTH:, :],
                 preferred_element_type=F32)
    x2 = x_ref[...] + y
    x2_ref[...] = x2
    ms = jnp.mean(x2 * x2, axis=-1, keepdims=True)
    h2 = x2 * lax.rsqrt(ms + EPS) * g_ref[...]
    h2b = h2.astype(BF16)
    h2_ref[...] = h2b
    logits = jnp.dot(h2b, wr_ref[...], preferred_element_type=F32) + br_ref[...]
    lane = _lane_iota((tm, LANES))
    lane_f = lane.astype(F32)
    lg = jnp.where(lane < N_EXPERTS, logits, -jnp.inf)
    vals, picks = [], []
    for _ in range(TOP_K):
        m = jnp.max(lg, axis=-1, keepdims=True)
        first = jnp.min(jnp.where(lg == m, lane_f, jnp.inf), axis=-1, keepdims=True)
        pick = lane_f == first
        vals.append(m)
        picks.append(pick)
        lg = jnp.where(pick, -jnp.inf, lg)
    exps = [jnp.exp(v - vals[0]) for v in vals]
    denom = exps[0] + exps[1] + exps[2] + exps[3]
    sel = jnp.zeros((tm, LANES), F32)
    for pick in picks:
        sel = sel + jnp.where(pick, 1.0, 0.0)
    strict = (lax.broadcasted_iota(jnp.int32, (tm, tm), 0)
              > lax.broadcasted_iota(jnp.int32, (tm, tm), 1))
    pos = run_ref[0:1, :] + jnp.dot(jnp.where(strict, 1.0, 0.0).astype(BF16), sel.astype(BF16),
                                    preferred_element_type=F32)
    aux = jnp.zeros((tm, LANES), F32)
    for k in range(TOP_K):
        eid = jnp.sum(jnp.where(picks[k], lane_f, 0.0), axis=-1, keepdims=True)
        rank = jnp.sum(jnp.where(picks[k], pos, 0.0), axis=-1, keepdims=True)
        aux = aux + jnp.where(lane == k, eid, 0.0)
        aux = aux + jnp.where(lane == TOP_K + k, exps[k] / denom, 0.0)
        aux = aux + jnp.where(lane == 2 * TOP_K + k, rank, 0.0)
    aux_ref[...] = aux
    run = run_ref[0:1, :] + jnp.sum(sel, axis=0, keepdims=True)
    run_ref[...] = jnp.broadcast_to(run, run_ref.shape)
    cnt_ref[...] = jnp.broadcast_to(run, cnt_ref.shape)


def _out_router(oa, orr, om, x2d, w_out, g_ffn, w_router, b_router, tm):
    t, d = x2d.shape
    wr = jnp.zeros((d, LANES), BF16).at[:, :N_EXPERTS].set(w_router.astype(BF16))
    br = jnp.zeros((1, LANES), F32).at[0, :N_EXPERTS].set(b_router)
    row = lambda w: pl.BlockSpec((tm, w), lambda i: (i, 0))
    const = lambda r, c: pl.BlockSpec((r, c), lambda i: (0, 0))
    return pl.pallas_call(
        _out_router_kernel,
        grid=(t // tm,),
        in_specs=[row(MOBA_WIDTH), row(RET_WIDTH), row(MEM_WIDTH), row(d),
                  const(d, d), const(1, d), const(d, LANES), const(1, LANES)],
        out_specs=[row(d), row(d), row(LANES), const(8, LANES)],
        out_shape=[jax.ShapeDtypeStruct((t, d), F32), jax.ShapeDtypeStruct((t, d), BF16),
                   jax.ShapeDtypeStruct((t, LANES), F32), jax.ShapeDtypeStruct((8, LANES), F32)],
        scratch_shapes=[pltpu.VMEM((8, LANES), F32)],
        compiler_params=_params("arbitrary"),
    )(oa, orr, om, x2d, w_out.astype(BF16), g_ffn.reshape(1, d), wr, br)


MXU_COLS = 256


def _moe_kernel(be_ref, nu_ref, xs_ref, w1_ref, b1g_ref, b1l_ref, w2_ref, b2_ref, o_ref,
                w1p_ref, w2b_ref, act_ref):
    i = pl.program_id(0)
    half = MXU_COLS // 2
    n_chunks = w1_ref.shape[2] // MXU_COLS

    @pl.when(i < nu_ref[0])
    def _():
        @pl.when(jnp.logical_or(i == 0, be_ref[i] != be_ref[jnp.maximum(i - 1, 0)]))
        def _():
            r = lax.broadcasted_iota(jnp.int32, (MXU_COLS, MXU_COLS), 0)
            c = lax.broadcasted_iota(jnp.int32, (MXU_COLS, MXU_COLS), 1)
            src = jnp.where(c < half, 2 * c, 2 * (c - half) + 1)
            perm = jnp.where(r == src, 1.0, 0.0).astype(BF16)
            for ch in range(n_chunks):
                cols = slice(ch * MXU_COLS, (ch + 1) * MXU_COLS)
                w = w1_ref[0, :, cols].astype(BF16)
                w1p_ref[:, cols] = jnp.dot(w, perm, preferred_element_type=F32).astype(BF16)
            w2b_ref[...] = w2_ref[0].astype(BF16)

        x = xs_ref[...]
        for ch in range(n_chunks):
            z = jnp.dot(x, w1p_ref[:, ch * MXU_COLS:(ch + 1) * MXU_COLS], preferred_element_type=F32)
            hs = slice(ch * half, (ch + 1) * half)
            xg = jnp.minimum(z[:, :half] + b1g_ref[0][:, hs], SWIGLU_LIMIT)
            xl = jnp.clip(z[:, half:] + b1l_ref[0][:, hs], -SWIGLU_LIMIT, SWIGLU_LIMIT)
            act_ref[:, hs] = (xg * jax.nn.sigmoid(SWIGLU_ALPHA * xg) * (xl + 1.0)).astype(BF16)
        y = jnp.dot(act_ref[...], w2b_ref[...], preferred_element_type=F32) + b2_ref[0]
        o_ref[...] = _pack_bf16_pairs(y)

    @pl.when(i >= nu_ref[0])
    def _():
        o_ref[...] = jnp.zeros_like(o_ref)


def _moe_experts(xs, blk_exp, n_used, w1, b1g, b1l, w2, b2):
    p_rows, d = xs.shape
    d_ff = w2.shape[1]
    assert w1.shape[2] == 2 * d_ff and (2 * d_ff) % MXU_COLS == 0
    n_blocks = p_rows // MOE_BLOCK
    rows = lambda i, be, nu: (jnp.minimum(i, nu[0] - 1), 0)
    wsel = lambda i, be, nu: (be[i], 0, 0)
    grid_spec = pltpu.PrefetchScalarGridSpec(
        num_scalar_prefetch=2,
        grid=(n_blocks,),
        in_specs=[pl.BlockSpec((MOE_BLOCK, d), rows),
                  pl.BlockSpec((1, d, 2 * d_ff), wsel),
                  pl.BlockSpec((1, 1, d_ff), wsel), pl.BlockSpec((1, 1, d_ff), wsel),
                  pl.BlockSpec((1, d_ff, d), wsel), pl.BlockSpec((1, 1, d), wsel)],
        out_specs=pl.BlockSpec((MOE_BLOCK, d // 2), lambda i, be, nu: (i, 0)),
        scratch_shapes=[pltpu.VMEM((d, 2 * d_ff), BF16), pltpu.VMEM((d_ff, d), BF16),
                        pltpu.VMEM((MOE_BLOCK, d_ff), BF16)],
    )
    return pl.pallas_call(
        _moe_kernel,
        grid_spec=grid_spec,
        out_shape=jax.ShapeDtypeStruct((p_rows, d // 2), jnp.uint32),
        compiler_params=pltpu.CompilerParams(dimension_semantics=("arbitrary",),
                                             vmem_limit_bytes=MOE_VMEM_LIMIT),
    )(blk_exp, n_used, xs, w1, b1g, b1l, w2, b2)


def _pack_bf16_pairs(y):
    n = y.shape[1] // 2
    lo = lax.bitcast_convert_type(y[:, :n].astype(BF16).astype(F32), jnp.uint32)
    hi = lax.bitcast_convert_type(y[:, n:].astype(BF16).astype(F32), jnp.uint32)
    return (lo >> 16) | hi


def _unpack_bf16_pairs(w):
    lo = lax.bitcast_convert_type(w << 16, F32)
    hi = lax.bitcast_convert_type(w & jnp.uint32(0xFFFF0000), F32)
    return lo, hi


def _sc_gather_rows(src, idx):
    n, w = src.shape
    m = idx.shape[0]
    workers = SC_CORES * SC_SUBCORES
    chunk = SC_GATHER_ROWS
    assert m % (workers * chunk) == 0
    per_worker = m // workers
    mesh = plsc.VectorSubcoreMesh(core_axis_name="c", subcore_axis_name="s",
                                  num_cores=SC_CORES, num_subcores=SC_SUBCORES)

    @functools.partial(
        pl.kernel, mesh=mesh, out_type=jax.ShapeDtypeStruct((m, w), src.dtype),
        scratch_types=[pltpu.VMEM((chunk,), jnp.int32), pltpu.VMEM((chunk, w), src.dtype),
                       pltpu.SemaphoreType.DMA])
    def gather(src_hbm, idx_hbm, out_hbm, idx_v, rows_v, sem):
        base = (lax.axis_index("s") * SC_CORES + lax.axis_index("c")) * per_worker

        @pl.loop(0, per_worker // chunk)
        def _(c):
            off = base + c * chunk
            pltpu.sync_copy(idx_hbm.at[pl.ds(off, chunk)], idx_v)
            pltpu.async_copy(src_hbm.at[idx_v], rows_v, sem).wait()
            pltpu.sync_copy(rows_v, out_hbm.at[pl.ds(off, chunk)])

    return gather(src, idx)


def _combine_kernel(x2_ref, aux_ref, yg_ref, o_ref):
    x2 = x2_ref[...]
    aux = aux_ref[...]
    half = x2.shape[1] // 2
    out_lo, out_hi = x2[:, :half], x2[:, half:]
    for k in range(TOP_K):
        lo, hi = _unpack_bf16_pairs(yg_ref[k])
        wk = aux[:, TOP_K + k:TOP_K + k + 1]
        out_lo = out_lo + wk * lo
        out_hi = out_hi + wk * hi
    o_ref[:, :half] = out_lo
    o_ref[:, half:] = out_hi


def _combine(x2, aux, yg, tm):
    t, d = x2.shape
    return pl.pallas_call(
        _combine_kernel,
        grid=(t // tm,),
        in_specs=[pl.BlockSpec((tm, d), lambda i: (i, 0)),
                  pl.BlockSpec((tm, LANES), lambda i: (i, 0)),
                  pl.BlockSpec((TOP_K, tm, d // 2), lambda i: (0, i, 0))],
        out_specs=pl.BlockSpec((tm, d), lambda i: (i, 0)),
        out_shape=jax.ShapeDtypeStruct((t, d), F32),
        compiler_params=_params("parallel"),
    )(x2, aux, yg)


def _layer(x, mem, g_attn, w_in, moba_q_gain, moba_k_gain, ret_out_gain, g_mem, w_mem_kv,
           mem_q_gain, mem_k_gain, w_out, g_ffn, w_router, b_router, w1, b1, w2, b2):
    batch, seq, d = x.shape
    t = batch * seq
    x2d = x.reshape(t, d)
    tm = min(512, t)

    proj = _rms_proj(x2d, g_attn, w_in.astype(BF16), tm)
    oa = _moba(proj, moba_q_gain, moba_k_gain, batch, seq)
    orr = _retention(proj, ret_out_gain, batch, seq)
    km, vm = _mem_kv(mem, g_mem, w_mem_kv, mem_k_gain)
    om = _mem_attn(proj, km, vm, mem_q_gain, batch, seq, min(512, seq))
    x2, h2, aux, counts = _out_router(oa, orr, om, x2d, w_out, g_ffn, w_router, b_router, tm)

    eidx = aux[:, 0:TOP_K].astype(jnp.int32)
    rank = aux[:, 2 * TOP_K:3 * TOP_K].astype(jnp.int32)
    cnt = counts[0, :N_EXPERTS].astype(jnp.int32)
    padded = ((cnt + MOE_BLOCK - 1) // MOE_BLOCK) * MOE_BLOCK
    pends = jnp.cumsum(padded)
    pstarts = pends - padded
    dest = pstarts[eidx] + rank
    a = t * TOP_K
    n_blocks = -(-(a + N_EXPERTS * (MOE_BLOCK - 1)) // MOE_BLOCK)
    p_rows = n_blocks * MOE_BLOCK
    blk_start = jnp.arange(n_blocks, dtype=jnp.int32) * MOE_BLOCK
    blk_exp = jnp.minimum(jnp.sum(blk_start[:, None] >= pends[None, :], axis=1),
                          N_EXPERTS - 1).astype(jnp.int32)
    n_used = (pends[-1] // MOE_BLOCK).astype(jnp.int32).reshape(1)

    row_tok = jnp.zeros((p_rows,), jnp.int32).at[dest.reshape(a)].set(
        jnp.arange(a, dtype=jnp.int32) // TOP_K, unique_indices=True, mode="promise_in_bounds")
    xs = h2.at[row_tok].get(mode="promise_in_bounds")

    d_ff = w2.shape[1]
    b1g = b1[:, 0::2].reshape(N_EXPERTS, 1, d_ff)
    b1l = b1[:, 1::2].reshape(N_EXPERTS, 1, d_ff)
    ys = _moe_experts(xs, blk_exp, n_used, w1, b1g, b1l, w2, b2.reshape(N_EXPERTS, 1, d))
    yg = _sc_gather_rows(ys, dest.T.reshape(a)).reshape(TOP_K, t, d // 2)
    out = _combine(x2, aux, yg, min(256, t))
    return out.reshape(batch, seq, d)


def kernel(x, mem, g_attn, w_in, moba_q_gain, moba_k_gain, ret_out_gain, g_mem, w_mem_kv,
           mem_q_gain, mem_k_gain, w_out, g_ffn, w_router, b_router, w1, b1, w2, b2):
    for l in range(g_attn.shape[0]):
        x = _layer(x, mem, g_attn[l], w_in[l], moba_q_gain[l], moba_k_gain[l], ret_out_gain[l],
                   g_mem[l], w_mem_kv[l], mem_q_gain[l], mem_k_gain[l], w_out[l], g_ffn[l],
                   w_router[l], b_router[l], w1[l], b1[l], w2[l], b2[l])
    return x
```

```python
import functools

import jax
import jax.numpy as jnp
from jax import lax
from jax.experimental import pallas as pl
from jax.experimental.pallas import tpu as pltpu
from jax.experimental.pallas import tpu_sc as plsc

F32 = jnp.float32
BF16 = jnp.bfloat16

LANES = 128
HEAD_DIM = 64
MOBA_HEADS = 8
RET_HEADS = 4
MEM_HEADS = 4
MOBA_WIDTH = MOBA_HEADS * HEAD_DIM
RET_WIDTH = RET_HEADS * HEAD_DIM
MEM_WIDTH = MEM_HEADS * HEAD_DIM
IN_WIDTH = 3 * MOBA_WIDTH + 4 * RET_WIDTH + MEM_WIDTH
MOBA_BLOCK = 256
MOBA_TOPK = 3
RET_CHUNK = 128
RET_ROPE_BASE = 10000.0
N_EXPERTS = 32
TOP_K = 4
SWIGLU_ALPHA = 1.702
SWIGLU_LIMIT = 7.0
MOE_BLOCK = 512
EPS = 1e-6
NEG = -1e30
QK_SCALE = HEAD_DIM ** -0.5

VMEM_LIMIT = 48 * 1024 * 1024
MOE_VMEM_LIMIT = 56 * 1024 * 1024
MIX_DTYPE = BF16
SC_CORES = 2
SC_SUBCORES = 16
SC_GATHER_ROWS = 128

_NT = (((1,), (1,)), ((), ()))
_TN = (((0,), (0,)), ((), ()))


def _params(*sem):
    return pltpu.CompilerParams(dimension_semantics=sem, vmem_limit_bytes=VMEM_LIMIT)


def _lane_iota(shape):
    return lax.broadcasted_iota(jnp.int32, shape, len(shape) - 1)


def _pair_rms(t, gain, lo):
    t2 = t * t
    s0 = jnp.sum(jnp.where(lo, t2, 0.0), axis=-1, keepdims=True)
    s1 = jnp.sum(jnp.where(lo, 0.0, t2), axis=-1, keepdims=True)
    r = jnp.where(lo, lax.rsqrt(s0 / HEAD_DIM + EPS), lax.rsqrt(s1 / HEAD_DIM + EPS))
    return t * r * gain


def _rms_proj_kernel(x_ref, g_ref, w_ref, o_ref):
    x = x_ref[...]
    ms = jnp.mean(x * x, axis=-1, keepdims=True)
    h = (x * lax.rsqrt(ms + EPS) * g_ref[...]).astype(BF16)
    o_ref[...] = jnp.dot(h, w_ref[...], preferred_element_type=F32)


def _rms_proj(x2d, gain, w, tm):
    t, d = x2d.shape
    n = w.shape[1]
    return pl.pallas_call(
        _rms_proj_kernel,
        grid=(t // tm,),
        in_specs=[pl.BlockSpec((tm, d), lambda i: (i, 0)),
                  pl.BlockSpec((1, d), lambda i: (0, 0)),
                  pl.BlockSpec((d, n), lambda i: (0, 0))],
        out_specs=pl.BlockSpec((tm, n), lambda i: (i, 0)),
        out_shape=jax.ShapeDtypeStruct((t, n), F32),
        compiler_params=_params("parallel"),
    )(x2d, gain.reshape(1, d), w)


SHIFT_SAFE = 80.0
MOBA_GROUP = 4


def _moba_kernel(tab_ref, q_ref, k_ref, v_ref, qg_ref, kg_ref, m_ref, o_ref,
                 ka_ref, va_ref, qa_ref, gt_ref, acc_ref, pa_ref, pb_ref, *, n_blk, n_trips):
    blk = MOBA_BLOCK
    grp = MOBA_GROUP
    nb8 = -(-n_blk // 8) * 8
    lane = _lane_iota((blk, LANES))
    lo = lane < HEAD_DIM
    lane1 = _lane_iota((1, LANES))
    qg = qg_ref[...]
    kg = kg_ref[...]
    tri = (lax.broadcasted_iota(jnp.int32, (blk, blk), 0)
           >= lax.broadcasted_iota(jnp.int32, (blk, blk), 1))
    row_t = lax.broadcasted_iota(jnp.int32, (nb8, blk), 0)
    row_tf = row_t.astype(F32)
    fill_row = lax.broadcasted_iota(jnp.int32, (HEAD_DIM - nb8, blk), 0)
    filler = jnp.where(fill_row == HEAD_DIM - nb8 - 1, -m_ref[...], 0.0)

    def rows_of(j):
        return pl.ds(pl.multiple_of(j * blk, blk), blk)

    gt_ref[...] = jnp.zeros_like(gt_ref)

    def prep_k(j, carry):
        rows = rows_of(j)
        kn = _pair_rms(k_ref[rows, :], kg, lo)
        km = jnp.mean(kn, axis=0, keepdims=True)
        gt_ref[pl.ds(HEAD_DIM + j, 1), :] = jnp.where(lane1 < HEAD_DIM, km, 0.0)
        gt_ref[pl.ds(j, 1), :] = jnp.where(lane1 < HEAD_DIM, 0.0, km)
        tag0 = jnp.where(jnp.logical_or(lane == HEAD_DIM + j, lane == LANES - 1), 1.0, 0.0)
        tag1 = jnp.where(jnp.logical_or(lane == j, lane == HEAD_DIM - 1), 1.0, 0.0)
        ka_ref[0, rows, :] = jnp.where(lo, kn, tag0).astype(BF16)
        ka_ref[1, rows, :] = jnp.where(lo, tag1, kn).astype(BF16)
        v = v_ref[rows, :]
        va_ref[0, rows, :] = jnp.where(lo, v, jnp.where(lane == HEAD_DIM, 1.0, 0.0)).astype(BF16)
        va_ref[1, rows, :] = jnp.where(lo, jnp.where(lane == 0, 1.0, 0.0), v).astype(BF16)
        return carry

    lax.fori_loop(0, n_blk, prep_k, 0)

    def block_bias(g, n):
        valid = row_t < n
        g = jnp.where(valid, g, NEG)
        sel = jnp.zeros((nb8, blk), jnp.bool_)
        for _ in range(MOBA_TOPK):
            m = jnp.max(g, axis=0, keepdims=True)
            first = jnp.min(jnp.where(g == m, row_tf, jnp.inf), axis=0, keepdims=True)
            pick = row_tf == first
            sel = jnp.logical_or(sel, pick)
            g = jnp.where(pick, -jnp.inf, g)
        keep = jnp.logical_or(jnp.logical_and(sel, valid), row_t == n)
        return jnp.where(keep, 0.0, NEG)

    def prep_q(n, carry):
        rows_n = rows_of(n)
        qn = _pair_rms(q_ref[rows_n, :], qg, lo)
        gate_t = lax.dot_general(gt_ref[...].astype(BF16), qn.astype(BF16), _NT,
                                 preferred_element_type=F32)
        bias_t = jnp.concatenate([block_bias(gate_t[0:nb8], n), filler,
                                  block_bias(gate_t[HEAD_DIM:HEAD_DIM + nb8], n), filler], axis=0)
        bias = bias_t.T
        qs = qn * QK_SCALE
        qa_ref[0, rows_n, :] = jnp.where(lo, qs, bias).astype(BF16)
        qa_ref[1, rows_n, :] = jnp.where(lo, bias, qs).astype(BF16)
        return carry

    lax.fori_loop(0, n_blk, prep_q, 0)

    def scores(h, n, j):
        s = lax.dot_general(qa_ref[h, rows_of(n), :], ka_ref[h, rows_of(j), :], _NT,
                            preferred_element_type=F32)
        return jnp.where(jnp.logical_or(tri, j != n), s, NEG)

    def normalise(n, a0, a1):
        o_ref[rows_of(n), :] = jnp.where(lo, a0 / a0[:, HEAD_DIM:HEAD_DIM + 1],
                                         a1 / a1[:, 0:1]).astype(o_ref.dtype)

    @pl.when(tab_ref[0, 0] == 1)
    def _():
        acc_ref[...] = jnp.zeros_like(acc_ref)

        def make_probs(trip, dst_ref):
            for g in range(grp):
                n = tab_ref[1, trip * grp + g]
                j = tab_ref[2, trip * grp + g]
                for h in range(2):
                    dst_ref[g, h] = jnp.exp(scores(h, n, j)).astype(BF16)

        def apply_probs(trip, src_ref):
            for g in range(grp):
                n = tab_ref[1, trip * grp + g]
                j = tab_ref[2, trip * grp + g]
                for h in range(2):
                    acc_ref[h, rows_of(n), :] += jnp.dot(src_ref[g, h], va_ref[h, rows_of(j), :],
                                                         preferred_element_type=F32)

        make_probs(0, pa_ref)

        def two_trips(i, carry):
            apply_probs(2 * i, pa_ref)
            make_probs(2 * i + 1, pb_ref)
            apply_probs(2 * i + 1, pb_ref)
            make_probs(2 * i + 2, pa_ref)
            return carry

        lax.fori_loop(0, n_trips // 2, two_trips, 0)

        def fin(n, carry):
            normalise(n, acc_ref[0, rows_of(n), :], acc_ref[1, rows_of(n), :])
            return carry

        lax.fori_loop(0, n_blk, fin, 0)

    @pl.when(tab_ref[0, 0] != 1)
    def _():
        def qblock(n, carry):
            def one(j, st):
                out = []
                for h in range(2):
                    m, acc = st[2 * h], st[2 * h + 1]
                    s = scores(h, n, j)
                    m_new = jnp.maximum(m, jnp.max(s, axis=-1, keepdims=True))
                    p = jnp.exp(s - m_new).astype(BF16)
                    acc = jnp.exp(m - m_new) * acc + jnp.dot(p, va_ref[h, rows_of(j), :],
                                                             preferred_element_type=F32)
                    out += [m_new, acc]
                return tuple(out)

            zero = jnp.zeros((blk, LANES), F32)
            ninf = jnp.full((blk, 1), -jnp.inf, F32)
            st = lax.fori_loop(0, n + 1, one, (ninf, zero, ninf, zero))
            normalise(n, st[1], st[3])
            return carry

        lax.fori_loop(0, n_blk, qblock, 0)


def _moba_tiles(n_blk):
    tiles = [(n, j) for n in range(n_blk) for j in range(n + 1)]
    per_two = 2 * MOBA_GROUP
    n_trips = 2 * (-(-len(tiles) // per_two))
    tiles += [(0, 1)] * ((n_trips + 1) * MOBA_GROUP - len(tiles))
    return n_trips, tiles


def _moba(proj, q_gain, k_gain, batch, seq):
    n_blk = seq // MOBA_BLOCK
    assert seq % MOBA_BLOCK == 0 and MOBA_TOPK <= n_blk <= HEAD_DIM - 8
    n_pairs = MOBA_WIDTH // LANES
    qg = jnp.tile(q_gain, 2).reshape(1, LANES)
    kg = jnp.tile(k_gain, 2).reshape(1, LANES)
    shift = HEAD_DIM * QK_SCALE * jnp.max(jnp.abs(q_gain)) * jnp.max(jnp.abs(k_gain))
    flag = (2.0 * shift <= SHIFT_SAFE).astype(jnp.int32)
    n_trips, tiles = _moba_tiles(n_blk)
    tab = jnp.stack([jnp.full((len(tiles),), flag, jnp.int32),
                     jnp.asarray([t[0] for t in tiles], jnp.int32),
                     jnp.asarray([t[1] for t in tiles], jnp.int32)])
    mrow = jnp.full((1, MOBA_BLOCK), shift, F32)
    blk = lambda off: pl.BlockSpec((seq, LANES), lambda b, p, f: (b, off + p))
    vec = lambda w: pl.BlockSpec((1, w), lambda b, p, f: (0, 0))
    prob = pltpu.VMEM((MOBA_GROUP, 2, MOBA_BLOCK, MOBA_BLOCK), BF16)
    grid_spec = pltpu.PrefetchScalarGridSpec(
        num_scalar_prefetch=1,
        grid=(batch, n_pairs),
        in_specs=[blk(0), blk(n_pairs), blk(2 * n_pairs), vec(LANES), vec(LANES), vec(MOBA_BLOCK)],
        out_specs=pl.BlockSpec((seq, LANES), lambda b, p, f: (b, p)),
        scratch_shapes=[pltpu.VMEM((2, seq, LANES), BF16),
                        pltpu.VMEM((2, seq, LANES), BF16),
                        pltpu.VMEM((2, seq, LANES), BF16),
                        pltpu.VMEM((LANES, LANES), F32),
                        pltpu.VMEM((2, seq, LANES), F32),
                        prob, prob],
    )
    return pl.pallas_call(
        functools.partial(_moba_kernel, n_blk=n_blk, n_trips=n_trips),
        grid_spec=grid_spec,
        out_shape=jax.ShapeDtypeStruct((batch * seq, MOBA_WIDTH), MIX_DTYPE),
        compiler_params=_params("parallel", "parallel"),
    )(tab, proj, proj, proj, qg, kg, mrow)


def _ret_kernel(q_ref, k_ref, v_ref, g_ref, cos_ref, sin_ref, intra_ref, kdec_ref, qdec_ref,
                cdm_ref, mask_ref, gain_ref, o_ref, *, n_ch):
    c = RET_CHUNK
    lane = _lane_iota((c, LANES))
    lo = lane < HEAD_DIM
    first_half = (lane % HEAD_DIM) < (HEAD_DIM // 2)
    gain = gain_ref[...]
    intra0 = intra_ref[0, 0]
    intra1 = intra_ref[0, 1]
    kdec = kdec_ref[0]
    qdec = qdec_ref[0]
    cdm = cdm_ref[0]
    mask = mask_ref[...]

    def rope(t, cos_t, sin_t):
        partner = jnp.where(first_half, pltpu.roll(t, LANES - HEAD_DIM // 2, 1),
                            pltpu.roll(t, HEAD_DIM // 2, 1))
        return t * cos_t + partner * sin_t

    def chunk(i, state):
        rows = pl.ds(pl.multiple_of(i * c, c), c)
        cos_t = cos_ref[rows, :]
        sin_t = sin_ref[rows, :]
        q = rope(q_ref[rows, :], cos_t, sin_t)
        k = rope(k_ref[rows, :], cos_t, sin_t) * QK_SCALE
        v = v_ref[rows, :]
        kb = k.astype(BF16)
        s0 = lax.dot_general(jnp.where(lo, q, 0.0).astype(BF16), kb, _NT,
                             preferred_element_type=F32) * intra0
        s1 = lax.dot_general(jnp.where(lo, 0.0, q).astype(BF16), kb, _NT,
                             preferred_element_type=F32) * intra1
        sc = jnp.concatenate([s0, s1], axis=1).astype(BF16)
        vv = jnp.concatenate([jnp.where(lo, v, 0.0), jnp.where(lo, 0.0, v)], axis=0).astype(BF16)
        o = jnp.dot(sc, vv, preferred_element_type=F32)
        o = o + jnp.dot(q.astype(BF16), state.astype(BF16), preferred_element_type=F32) * qdec
        kd = (k * kdec).astype(BF16)
        kv = lax.dot_general(kd, v.astype(BF16), _TN, preferred_element_type=F32)
        state = cdm * state + mask * kv
        o2 = o * o
        m0 = jnp.sum(jnp.where(lo, o2, 0.0), axis=-1, keepdims=True)
        m1 = jnp.sum(jnp.where(lo, 0.0, o2), axis=-1, keepdims=True)
        r = jnp.where(lo, lax.rsqrt(m0 / HEAD_DIM + EPS), lax.rsqrt(m1 / HEAD_DIM + EPS))
        gt = g_ref[rows, :]
        o_ref[rows, :] = ((gt * jax.nn.sigmoid(gt)) * (o * r * gain)).astype(o_ref.dtype)
        return state

    lax.fori_loop(0, n_ch, chunk, jnp.zeros((LANES, LANES), F32))


def _ret_tables(seq):
    c = RET_CHUNK
    half = HEAD_DIM // 2
    inv_freq = 1.0 / (RET_ROPE_BASE ** jnp.linspace(0.0, 1.0, half, dtype=F32))
    ang = jnp.arange(seq, dtype=F32)[:, None] * inv_freq[None, :]
    cos, sin = jnp.cos(ang), jnp.sin(ang)
    cos_t = jnp.tile(cos, (1, LANES // half))
    sin_t = jnp.tile(jnp.concatenate([-sin, sin], axis=1), (1, LANES // HEAD_DIM))
    log_gamma = jnp.log1p(-jnp.exp2(-5.0 - jnp.arange(RET_HEADS, dtype=F32)))
    idx = jnp.arange(c, dtype=F32)
    diff = idx[:, None] - idx[None, :]
    intra = jnp.where(diff >= 0, jnp.exp(log_gamma[:, None, None] * jnp.maximum(diff, 0.0)), 0.0)
    k_decay = jnp.exp(log_gamma[:, None] * (c - 1.0 - idx)[None, :])
    q_decay = jnp.exp(log_gamma[:, None] * (idx + 1.0)[None, :])
    chunk_decay = jnp.exp(log_gamma * c)
    n_pairs = RET_HEADS // 2

    def lanes(t):
        return jnp.repeat(t.reshape(n_pairs, 2, c).transpose(0, 2, 1), HEAD_DIM, axis=2)

    blockdiag = (jnp.arange(LANES)[:, None] // HEAD_DIM) == (jnp.arange(LANES)[None, :] // HEAD_DIM)
    mask = blockdiag.astype(F32)
    cdm = jnp.repeat(chunk_decay.reshape(n_pairs, 2), HEAD_DIM, axis=1)[:, :, None] * mask[None]
    return (cos_t, sin_t, intra.reshape(n_pairs, 2, c, c), lanes(k_decay), lanes(q_decay), cdm, mask)


def _retention(proj, out_gain, batch, seq):
    c = RET_CHUNK
    assert seq % c == 0
    n_pairs = RET_HEADS // 2
    base = 3 * MOBA_WIDTH // LANES
    cos_t, sin_t, intra, kdec, qdec, cdm, mask = _ret_tables(seq)
    blk = lambda off: pl.BlockSpec((seq, LANES), lambda b, p: (b, base + off + p))
    tab = pl.BlockSpec((seq, LANES), lambda b, p: (0, 0))
    return pl.pallas_call(
        functools.partial(_ret_kernel, n_ch=seq // c),
        grid=(batch, n_pairs),
        in_specs=[blk(0), blk(n_pairs), blk(2 * n_pairs), blk(3 * n_pairs), tab, tab,
                  pl.BlockSpec((1, 2, c, c), lambda b, p: (p, 0, 0, 0)),
                  pl.BlockSpec((1, c, LANES), lambda b, p: (p, 0, 0)),
                  pl.BlockSpec((1, c, LANES), lambda b, p: (p, 0, 0)),
                  pl.BlockSpec((1, LANES, LANES), lambda b, p: (p, 0, 0)),
                  pl.BlockSpec((LANES, LANES), lambda b, p: (0, 0)),
                  pl.BlockSpec((1, LANES), lambda b, p: (0, p))],
        out_specs=pl.BlockSpec((seq, LANES), lambda b, p: (b, p)),
        out_shape=jax.ShapeDtypeStruct((batch * seq, RET_WIDTH), MIX_DTYPE),
        compiler_params=_params("parallel", "parallel"),
    )(proj, proj, proj, proj, cos_t, sin_t, intra, kdec, qdec, cdm, mask,
      out_gain.reshape(1, RET_WIDTH))


def _mem_kv_kernel(m_ref, g_ref, w_ref, kg_ref, k_ref, v_ref):
    x = m_ref[0]
    ms = jnp.mean(x * x, axis=-1, keepdims=True)
    h = (x * lax.rsqrt(ms + EPS) * g_ref[...]).astype(BF16)
    kv = jnp.dot(h, w_ref[...], preferred_element_type=F32)
    lane = _lane_iota((x.shape[0], LANES))
    lo = lane < HEAD_DIM
    for p in range(MEM_WIDTH // LANES):
        kt = kv[:, p * LANES:(p + 1) * LANES]
        k_ref[0, :, p * LANES:(p + 1) * LANES] = _pair_rms(kt, kg_ref[...], lo).astype(BF16)
    v_ref[0] = kv[:, MEM_WIDTH:].astype(BF16)


def _mem_kv(mem, g_mem, w_mem_kv, k_gain):
    b, m, d = mem.shape
    return pl.pallas_call(
        _mem_kv_kernel,
        grid=(b,),
        in_specs=[pl.BlockSpec((1, m, d), lambda i: (i, 0, 0)),
                  pl.BlockSpec((1, d), lambda i: (0, 0)),
                  pl.BlockSpec((d, 2 * MEM_WIDTH), lambda i: (0, 0)),
                  pl.BlockSpec((1, LANES), lambda i: (0, 0))],
        out_specs=[pl.BlockSpec((1, m, MEM_WIDTH), lambda i: (i, 0, 0)),
                   pl.BlockSpec((1, m, MEM_WIDTH), lambda i: (i, 0, 0))],
        out_shape=[jax.ShapeDtypeStruct((b, m, MEM_WIDTH), BF16),
                   jax.ShapeDtypeStruct((b, m, MEM_WIDTH), BF16)],
        compiler_params=_params("parallel"),
    )(mem, g_mem.reshape(1, d), w_mem_kv.astype(BF16), jnp.tile(k_gain, 2).reshape(1, LANES))


def _mem_attn_kernel(q_ref, k_ref, v_ref, qg_ref, o_ref):
    tq = q_ref.shape[0]
    lane = _lane_iota((tq, LANES))
    lo = lane < HEAD_DIM
    for p in range(MEM_WIDTH // LANES):
        cols = slice(p * LANES, (p + 1) * LANES)
        qn = _pair_rms(q_ref[:, cols], qg_ref[...], lo) * QK_SCALE
        kt = k_ref[0, :, cols]
        vt = v_ref[0, :, cols]
        outs = []
        for h in range(2):
            qh = jnp.where(lo, qn, 0.0) if h == 0 else jnp.where(lo, 0.0, qn)
            s = lax.dot_general(qh.astype(BF16), kt, _NT, preferred_element_type=F32)
            m = jnp.max(s, axis=-1, keepdims=True)
            e = jnp.exp(s - m)
            pr = e / jnp.sum(e, axis=-1, keepdims=True)
            outs.append(jnp.dot(pr.astype(BF16), vt, preferred_element_type=F32))
        o_ref[:, cols] = jnp.where(lo, outs[0], outs[1]).astype(o_ref.dtype)


def _mem_attn(proj, km, vm, q_gain, batch, seq, tq):
    m = km.shape[1]
    qcol = (IN_WIDTH - MEM_WIDTH) // MEM_WIDTH
    assert qcol * MEM_WIDTH == IN_WIDTH - MEM_WIDTH
    nq = seq // tq
    return pl.pallas_call(
        _mem_attn_kernel,
        grid=(batch, nq),
        in_specs=[pl.BlockSpec((tq, MEM_WIDTH), lambda b, i: (b * nq + i, qcol)),
                  pl.BlockSpec((1, m, MEM_WIDTH), lambda b, i: (b, 0, 0)),
                  pl.BlockSpec((1, m, MEM_WIDTH), lambda b, i: (b, 0, 0)),
                  pl.BlockSpec((1, LANES), lambda b, i: (0, 0))],
        out_specs=pl.BlockSpec((tq, MEM_WIDTH), lambda b, i: (b * nq + i, 0)),
        out_shape=jax.ShapeDtypeStruct((batch * seq, MEM_WIDTH), MIX_DTYPE),
        compiler_params=_params("parallel", "parallel"),
    )(proj, km, vm, jnp.tile(q_gain, 2).reshape(1, LANES))


def _out_router_kernel(oa_ref, or_ref, om_ref, x_ref, wo_ref, g_ref, wr_ref, br_ref,
                       x2_ref, h2_ref, aux_ref, cnt_ref, run_ref):
    tm = x_ref.shape[0]

    @pl.when(pl.program_id(0) == 0)
    def _():
        run_ref[...] = jnp.zeros_like(run_ref)

    y = jnp.dot(oa_ref[...].astype(BF16), wo_ref[0:MOBA_WIDTH, :], preferred_element_type=F32)
    y += jnp.dot(or_ref[...].astype(BF16), wo_ref[MOBA_WIDTH:MOBA_WIDTH + RET_WIDTH, :],
                 preferred_element_type=F32)
    y += jnp.dot(om_ref[...].astype(BF16), wo_ref[MOBA_WIDTH + RET_WIDTH:, :],
                 preferred_element_type=F32)
    x2 = x_ref[...] + y
    x2_ref[...] = x2
    ms = jnp.mean(x2 * x2, axis=-1, keepdims=True)
    h2 = x2 * lax.rsqrt(ms + EPS) * g_ref[...]
    h2b = h2.astype(BF16)
    h2_ref[...] = _pack_bf16_pairs(h2)
    logits = jnp.dot(h2b, wr_ref[...], preferred_element_type=F32) + br_ref[...]
    lane = _lane_iota((tm, LANES))
    lane_f = lane.astype(F32)
    lg = jnp.where(lane < N_EXPERTS, logits, -jnp.inf)
    vals, picks = [], []
    for _ in range(TOP_K):
        m = jnp.max(lg, axis=-1, keepdims=True)
        first = jnp.min(jnp.where(lg == m, lane_f, jnp.inf), axis=-1, keepdims=True)
        pick = lane_f == first
        vals.append(m)
        picks.append(pick)
        lg = jnp.where(pick, -jnp.inf, lg)
    exps = [jnp.exp(v - vals[0]) for v in vals]
    denom = exps[0] + exps[1] + exps[2] + exps[3]
    sel = jnp.zeros((tm, LANES), F32)
    for pick in picks:
        sel = sel + jnp.where(pick, 1.0, 0.0)
    strict = (lax.broadcasted_iota(jnp.int32, (tm, tm), 0)
              > lax.broadcasted_iota(jnp.int32, (tm, tm), 1))
    pos = run_ref[0:1, :] + jnp.dot(jnp.where(strict, 1.0, 0.0).astype(BF16), sel.astype(BF16),
                                    preferred_element_type=F32)
    aux = jnp.zeros((tm, LANES), F32)
    for k in range(TOP_K):
        eid = jnp.sum(jnp.where(picks[k], lane_f, 0.0), axis=-1, keepdims=True)
        rank = jnp.sum(jnp.where(picks[k], pos, 0.0), axis=-1, keepdims=True)
        aux = aux + jnp.where(lane == k, eid, 0.0)
        aux = aux + jnp.where(lane == TOP_K + k, exps[k] / denom, 0.0)
        aux = aux + jnp.where(lane == 2 * TOP_K + k, rank, 0.0)
    aux_ref[...] = aux
    run = run_ref[0:1, :] + jnp.sum(sel, axis=0, keepdims=True)
    run_ref[...] = jnp.broadcast_to(run, run_ref.shape)
    cnt_ref[...] = jnp.broadcast_to(run, cnt_ref.shape)


def _out_router(oa, orr, om, x2d, w_out, g_ffn, w_router, b_router, tm):
    t, d = x2d.shape
    wr = jnp.zeros((d, LANES), BF16).at[:, :N_EXPERTS].set(w_router.astype(BF16))
    br = jnp.zeros((1, LANES), F32).at[0, :N_EXPERTS].set(b_router)
    row = lambda w: pl.BlockSpec((tm, w), lambda i: (i, 0))
    const = lambda r, c: pl.BlockSpec((r, c), lambda i: (0, 0))
    return pl.pallas_call(
        _out_router_kernel,
        grid=(t // tm,),
        in_specs=[row(MOBA_WIDTH), row(RET_WIDTH), row(MEM_WIDTH), row(d),
                  const(d, d), const(1, d), const(d, LANES), const(1, LANES)],
        out_specs=[row(d), row(d // 2), row(LANES), const(8, LANES)],
        out_shape=[jax.ShapeDtypeStruct((t, d), F32), jax.ShapeDtypeStruct((t, d // 2), jnp.uint32),
                   jax.ShapeDtypeStruct((t, LANES), F32), jax.ShapeDtypeStruct((8, LANES), F32)],
        scratch_shapes=[pltpu.VMEM((8, LANES), F32)],
        compiler_params=_params("arbitrary"),
    )(oa, orr, om, x2d, w_out.astype(BF16), g_ffn.reshape(1, d), wr, br)


MXU_COLS = 256


def _moe_kernel(be_ref, nu_ref, nv_ref, xs_ref, w1_ref, b1g_ref, b1l_ref, w2_ref, b2_ref, o_ref,
                w1p_ref, w2b_ref, act_ref):
    i = pl.program_id(0)
    half = MXU_COLS // 2
    n_chunks = w1_ref.shape[2] // MXU_COLS

    @pl.when(i < nu_ref[0])
    def _():
        @pl.when(jnp.logical_or(i == 0, be_ref[i] != be_ref[jnp.maximum(i - 1, 0)]))
        def _():
            r = lax.broadcasted_iota(jnp.int32, (MXU_COLS, MXU_COLS), 0)
            c = lax.broadcasted_iota(jnp.int32, (MXU_COLS, MXU_COLS), 1)
            src = jnp.where(c < half, 2 * c, 2 * (c - half) + 1)
            perm = jnp.where(r == src, 1.0, 0.0).astype(BF16)
            for ch in range(n_chunks):
                cols = slice(ch * MXU_COLS, (ch + 1) * MXU_COLS)
                w = w1_ref[0, :, cols].astype(BF16)
                w1p_ref[:, cols] = jnp.dot(w, perm, preferred_element_type=F32).astype(BF16)
            w2b_ref[...] = w2_ref[0].astype(BF16)

        row = lax.broadcasted_iota(jnp.int32, xs_ref.shape, 0)
        x_lo, x_hi = _unpack_bf16_pairs(jnp.where(row < nv_ref[i], xs_ref[...], jnp.uint32(0)))
        x = jnp.concatenate([x_lo.astype(BF16), x_hi.astype(BF16)], axis=1)
        for ch in range(n_chunks):
            z = jnp.dot(x, w1p_ref[:, ch * MXU_COLS:(ch + 1) * MXU_COLS], preferred_element_type=F32)
            hs = slice(ch * half, (ch + 1) * half)
            xg = jnp.minimum(z[:, :half] + b1g_ref[0][:, hs], SWIGLU_LIMIT)
            xl = jnp.clip(z[:, half:] + b1l_ref[0][:, hs], -SWIGLU_LIMIT, SWIGLU_LIMIT)
            act_ref[:, hs] = (xg * jax.nn.sigmoid(SWIGLU_ALPHA * xg) * (xl + 1.0)).astype(BF16)
        y = jnp.dot(act_ref[...], w2b_ref[...], preferred_element_type=F32) + b2_ref[0]
        o_ref[...] = _pack_bf16_pairs(y)

    @pl.when(i >= nu_ref[0])
    def _():
        o_ref[...] = jnp.zeros_like(o_ref)


def _moe_experts(xs, blk_exp, n_used, n_valid, w1, b1g, b1l, w2, b2):
    p_rows = xs.shape[0]
    d = w1.shape[1]
    d_ff = w2.shape[1]
    assert w1.shape[2] == 2 * d_ff and (2 * d_ff) % MXU_COLS == 0 and xs.shape[1] == d // 2
    n_blocks = p_rows // MOE_BLOCK
    rows = lambda i, be, nu, nv: (jnp.minimum(i, nu[0] - 1), 0)
    wsel = lambda i, be, nu, nv: (be[i], 0, 0)
    grid_spec = pltpu.PrefetchScalarGridSpec(
        num_scalar_prefetch=3,
        grid=(n_blocks,),
        in_specs=[pl.BlockSpec((MOE_BLOCK, d // 2), rows),
                  pl.BlockSpec((1, d, 2 * d_ff), wsel),
                  pl.BlockSpec((1, 1, d_ff), wsel), pl.BlockSpec((1, 1, d_ff), wsel),
                  pl.BlockSpec((1, d_ff, d), wsel), pl.BlockSpec((1, 1, d), wsel)],
        out_specs=pl.BlockSpec((MOE_BLOCK, d // 2), lambda i, be, nu, nv: (i, 0)),
        scratch_shapes=[pltpu.VMEM((d, 2 * d_ff), BF16), pltpu.VMEM((d_ff, d), BF16),
                        pltpu.VMEM((MOE_BLOCK, d_ff), BF16)],
    )
    return pl.pallas_call(
        _moe_kernel,
        grid_spec=grid_spec,
        out_shape=jax.ShapeDtypeStruct((p_rows, d // 2), jnp.uint32),
        compiler_params=pltpu.CompilerParams(dimension_semantics=("arbitrary",),
                                             vmem_limit_bytes=MOE_VMEM_LIMIT),
    )(blk_exp, n_used, n_valid, xs, w1, b1g, b1l, w2, b2)


def _pack_bf16_pairs(y):
    n = y.shape[1] // 2
    lo = lax.bitcast_convert_type(y[:, :n].astype(BF16).astype(F32), jnp.uint32)
    hi = lax.bitcast_convert_type(y[:, n:].astype(BF16).astype(F32), jnp.uint32)
    return (lo >> 16) | hi


def _unpack_bf16_pairs(w):
    lo = lax.bitcast_convert_type(w << 16, F32)
    hi = lax.bitcast_convert_type(w & jnp.uint32(0xFFFF0000), F32)
    return lo, hi


def _sc_gather_rows(src, idx):
    n, w = src.shape
    m = idx.shape[0]
    workers = SC_CORES * SC_SUBCORES
    chunk = SC_GATHER_ROWS
    assert m % (workers * chunk) == 0
    per_worker = m // workers
    mesh = plsc.VectorSubcoreMesh(core_axis_name="c", subcore_axis_name="s",
                                  num_cores=SC_CORES, num_subcores=SC_SUBCORES)

    @functools.partial(
        pl.kernel, mesh=mesh, out_type=jax.ShapeDtypeStruct((m, w), src.dtype),
        scratch_types=[pltpu.VMEM((chunk,), jnp.int32), pltpu.VMEM((chunk, w), src.dtype),
                       pltpu.SemaphoreType.DMA])
    def gather(src_hbm, idx_hbm, out_hbm, idx_v, rows_v, sem):
        base = (lax.axis_index("s") * SC_CORES + lax.axis_index("c")) * per_worker

        @pl.loop(0, per_worker // chunk)
        def _(c):
            off = base + c * chunk
            pltpu.sync_copy(idx_hbm.at[pl.ds(off, chunk)], idx_v)
            pltpu.async_copy(src_hbm.at[idx_v], rows_v, sem).wait()
            pltpu.sync_copy(rows_v, out_hbm.at[pl.ds(off, chunk)])

    return gather(src, idx)


def _sc_scatter_rows(src, idx_t, n_out):
    t, w = src.shape
    fan = idx_t.shape[0]
    workers = SC_CORES * SC_SUBCORES
    chunk = SC_GATHER_ROWS
    assert t % (workers * chunk) == 0 and idx_t.shape[1] == t
    per_worker = t // workers
    mesh = plsc.VectorSubcoreMesh(core_axis_name="c", subcore_axis_name="s",
                                  num_cores=SC_CORES, num_subcores=SC_SUBCORES)

    @functools.partial(
        pl.kernel, mesh=mesh, out_type=jax.ShapeDtypeStruct((n_out, w), src.dtype),
        scratch_types=[pltpu.VMEM((fan, chunk), jnp.int32), pltpu.VMEM((chunk, w), src.dtype)])
    def scatter(src_hbm, idx_hbm, out_hbm, idx_v, rows_v):
        base = (lax.axis_index("s") * SC_CORES + lax.axis_index("c")) * per_worker

        @pl.loop(0, per_worker // chunk)
        def _(c):
            off = base + c * chunk
            pltpu.sync_copy(idx_hbm.at[:, pl.ds(off, chunk)], idx_v)
            pltpu.sync_copy(src_hbm.at[pl.ds(off, chunk)], rows_v)
            for j in range(fan):
                pltpu.sync_copy(rows_v, out_hbm.at[idx_v.at[j]])

    return scatter(src, idx_t)


def _combine_kernel(x2_ref, aux_ref, yg_ref, o_ref):
    x2 = x2_ref[...]
    aux = aux_ref[...]
    half = x2.shape[1] // 2
    out_lo, out_hi = x2[:, :half], x2[:, half:]
    for k in range(TOP_K):
        lo, hi = _unpack_bf16_pairs(yg_ref[k])
        wk = aux[:, TOP_K + k:TOP_K + k + 1]
        out_lo = out_lo + wk * lo
        out_hi = out_hi + wk * hi
    o_ref[:, :half] = out_lo
    o_ref[:, half:] = out_hi


def _combine(x2, aux, yg, tm):
    t, d = x2.shape
    return pl.pallas_call(
        _combine_kernel,
        grid=(t // tm,),
        in_specs=[pl.BlockSpec((tm, d), lambda i: (i, 0)),
                  pl.BlockSpec((tm, LANES), lambda i: (i, 0)),
                  pl.BlockSpec((TOP_K, tm, d // 2), lambda i: (0, i, 0))],
        out_specs=pl.BlockSpec((tm, d), lambda i: (i, 0)),
        out_shape=jax.ShapeDtypeStruct((t, d), F32),
        compiler_params=_params("parallel"),
    )(x2, aux, yg)


def _layer(x, mem, g_attn, w_in, moba_q_gain, moba_k_gain, ret_out_gain, g_mem, w_mem_kv,
           mem_q_gain, mem_k_gain, w_out, g_ffn, w_router, b_router, w1, b1, w2, b2):
    batch, seq, d = x.shape
    t = batch * seq
    x2d = x.reshape(t, d)
    tm = min(512, t)

    proj = _rms_proj(x2d, g_attn, w_in.astype(BF16), tm)
    oa = _moba(proj, moba_q_gain, moba_k_gain, batch, seq)
    orr = _retention(proj, ret_out_gain, batch, seq)
    km, vm = _mem_kv(mem, g_mem, w_mem_kv, mem_k_gain)
    om = _mem_attn(proj, km, vm, mem_q_gain, batch, seq, min(512, seq))
    x2, h2, aux, counts = _out_router(oa, orr, om, x2d, w_out, g_ffn, w_router, b_router, tm)

    eidx = aux[:, 0:TOP_K].astype(jnp.int32)
    rank = aux[:, 2 * TOP_K:3 * TOP_K].astype(jnp.int32)
    cnt = counts[0, :N_EXPERTS].astype(jnp.int32)
    padded = ((cnt + MOE_BLOCK - 1) // MOE_BLOCK) * MOE_BLOCK
    pends = jnp.cumsum(padded)
    pstarts = pends - padded
    dest = pstarts[eidx] + rank
    a = t * TOP_K
    n_blocks = -(-(a + N_EXPERTS * (MOE_BLOCK - 1)) // MOE_BLOCK)
    p_rows = n_blocks * MOE_BLOCK
    blk_start = jnp.arange(n_blocks, dtype=jnp.int32) * MOE_BLOCK
    blk_exp = jnp.minimum(jnp.sum(blk_start[:, None] >= pends[None, :], axis=1),
                          N_EXPERTS - 1).astype(jnp.int32)
    n_used = (pends[-1] // MOE_BLOCK).astype(jnp.int32).reshape(1)

    n_valid = jnp.clip((pstarts + cnt)[blk_exp] - blk_start, 0, MOE_BLOCK).astype(jnp.int32)
    dest_t = dest.T
    xs = _sc_scatter_rows(h2, dest_t, p_rows)

    d_ff = w2.shape[1]
    b1g = b1[:, 0::2].reshape(N_EXPERTS, 1, d_ff)
    b1l = b1[:, 1::2].reshape(N_EXPERTS, 1, d_ff)
    ys = _moe_experts(xs, blk_exp, n_used, n_valid, w1, b1g, b1l, w2,
                      b2.reshape(N_EXPERTS, 1, d))
    yg = _sc_gather_rows(ys, dest_t.reshape(a)).reshape(TOP_K, t, d // 2)
    out = _combine(x2, aux, yg, min(256, t))
    return out.reshape(batch, seq, d)


def kernel(x, mem, g_attn, w_in, moba_q_gain, moba_k_gain, ret_out_gain, g_mem, w_mem_kv,
           mem_q_gain, mem_k_gain, w_out, g_ffn, w_router, b_router, w1, b1, w2, b2):
    for l in range(g_attn.shape[0]):
        x = _layer(x, mem, g_attn[l], w_in[l], moba_q_gain[l], moba_k_gain[l], ret_out_gain[l],
                   g_mem[l], w_mem_kv[l], mem_q_gain[l], mem_k_gain[l], w_out[l], g_ffn[l],
                   w_router[l], b_router[l], w1[l], b1[l], w2[l], b2[l])
    return x
```

```python
import functools

import jax
import jax.numpy as jnp
from jax import lax
from jax.experimental import pallas as pl
from jax.experimental.pallas import tpu as pltpu
from jax.experimental.pallas import tpu_sc as plsc

F32 = jnp.float32
BF16 = jnp.bfloat16

LANES = 128
HEAD_DIM = 64
MOBA_HEADS = 8
RET_HEADS = 4
MEM_HEADS = 4
MOBA_WIDTH = MOBA_HEADS * HEAD_DIM
RET_WIDTH = RET_HEADS * HEAD_DIM
MEM_WIDTH = MEM_HEADS * HEAD_DIM
IN_WIDTH = 3 * MOBA_WIDTH + 4 * RET_WIDTH + MEM_WIDTH
MOBA_BLOCK = 256
MOBA_TOPK = 3
RET_CHUNK = 128
RET_ROPE_BASE = 10000.0
N_EXPERTS = 32
TOP_K = 4
SWIGLU_ALPHA = 1.702
SWIGLU_LIMIT = 7.0
MOE_BLOCK = 1024
PREP_UNROLL = 4
EPS = 1e-6
NEG = -1e30
QK_SCALE = HEAD_DIM ** -0.5

VMEM_LIMIT = 48 * 1024 * 1024
MOE_VMEM_LIMIT = 56 * 1024 * 1024
MIX_DTYPE = BF16
SC_CORES = 2
SC_SUBCORES = 16
SC_GATHER_ROWS = 128

_NT = (((1,), (1,)), ((), ()))
_TN = (((0,), (0,)), ((), ()))


def _params(*sem):
    return pltpu.CompilerParams(dimension_semantics=sem, vmem_limit_bytes=VMEM_LIMIT)


def _lane_iota(shape):
    return lax.broadcasted_iota(jnp.int32, shape, len(shape) - 1)


def _pair_rms(t, gain, lo):
    t2 = t * t
    s0 = jnp.sum(jnp.where(lo, t2, 0.0), axis=-1, keepdims=True)
    s1 = jnp.sum(jnp.where(lo, 0.0, t2), axis=-1, keepdims=True)
    r = jnp.where(lo, lax.rsqrt(s0 / HEAD_DIM + EPS), lax.rsqrt(s1 / HEAD_DIM + EPS))
    return t * r * gain


def _rms_proj_kernel(x_ref, g_ref, w_ref, o_ref):
    x = x_ref[...]
    ms = jnp.mean(x * x, axis=-1, keepdims=True)
    h = (x * lax.rsqrt(ms + EPS) * g_ref[...]).astype(BF16)
    o_ref[...] = jnp.dot(h, w_ref[...], preferred_element_type=F32)


def _rms_proj(x2d, gain, w, tm):
    t, d = x2d.shape
    n = w.shape[1]
    return pl.pallas_call(
        _rms_proj_kernel,
        grid=(t // tm,),
        in_specs=[pl.BlockSpec((tm, d), lambda i: (i, 0)),
                  pl.BlockSpec((1, d), lambda i: (0, 0)),
                  pl.BlockSpec((d, n), lambda i: (0, 0))],
        out_specs=pl.BlockSpec((tm, n), lambda i: (i, 0)),
        out_shape=jax.ShapeDtypeStruct((t, n), F32),
        compiler_params=_params("parallel"),
    )(x2d, gain.reshape(1, d), w)


SHIFT_SAFE = 80.0
MOBA_GROUP = 4


def _moba_kernel(tab_ref, q_ref, k_ref, v_ref, qg_ref, kg_ref, m_ref, o_ref,
                 ka_ref, va_ref, qa_ref, gt_ref, acc_ref, pa_ref, pb_ref, *, n_blk, n_trips):
    blk = MOBA_BLOCK
    grp = MOBA_GROUP
    nb8 = -(-n_blk // 8) * 8
    lane = _lane_iota((blk, LANES))
    lo = lane < HEAD_DIM
    lane1 = _lane_iota((1, LANES))
    qg = qg_ref[...]
    kg = kg_ref[...]
    tri = (lax.broadcasted_iota(jnp.int32, (blk, blk), 0)
           >= lax.broadcasted_iota(jnp.int32, (blk, blk), 1))
    row_t = lax.broadcasted_iota(jnp.int32, (nb8, blk), 0)
    row_tf = row_t.astype(F32)
    fill_row = lax.broadcasted_iota(jnp.int32, (HEAD_DIM - nb8, blk), 0)
    filler = jnp.where(fill_row == HEAD_DIM - nb8 - 1, -m_ref[...], 0.0)

    def rows_of(j):
        return pl.ds(pl.multiple_of(j * blk, blk), blk)

    gt_ref[...] = jnp.zeros_like(gt_ref)

    def prep_k(j, carry):
        rows = rows_of(j)
        kn = _pair_rms(k_ref[rows, :], kg, lo)
        km = jnp.mean(kn, axis=0, keepdims=True)
        gt_ref[pl.ds(HEAD_DIM + j, 1), :] = jnp.where(lane1 < HEAD_DIM, km, 0.0)
        gt_ref[pl.ds(j, 1), :] = jnp.where(lane1 < HEAD_DIM, 0.0, km)
        tag0 = jnp.where(jnp.logical_or(lane == HEAD_DIM + j, lane == LANES - 1), 1.0, 0.0)
        tag1 = jnp.where(jnp.logical_or(lane == j, lane == HEAD_DIM - 1), 1.0, 0.0)
        ka_ref[0, rows, :] = jnp.where(lo, kn, tag0).astype(BF16)
        ka_ref[1, rows, :] = jnp.where(lo, tag1, kn).astype(BF16)
        v = v_ref[rows, :]
        va_ref[0, rows, :] = jnp.where(lo, v, jnp.where(lane == HEAD_DIM, 1.0, 0.0)).astype(BF16)
        va_ref[1, rows, :] = jnp.where(lo, jnp.where(lane == 0, 1.0, 0.0), v).astype(BF16)
        return carry

    lax.fori_loop(0, n_blk, prep_k, 0, unroll=PREP_UNROLL)

    def block_bias(g, n):
        valid = row_t < n
        g = jnp.where(valid, g, NEG)
        sel = jnp.zeros((nb8, blk), jnp.bool_)
        for _ in range(MOBA_TOPK):
            m = jnp.max(g, axis=0, keepdims=True)
            first = jnp.min(jnp.where(g == m, row_tf, jnp.inf), axis=0, keepdims=True)
            pick = row_tf == first
            sel = jnp.logical_or(sel, pick)
            g = jnp.where(pick, -jnp.inf, g)
        keep = jnp.logical_or(jnp.logical_and(sel, valid), row_t == n)
        return jnp.where(keep, 0.0, NEG)

    def prep_q(n, carry):
        rows_n = rows_of(n)
        qn = _pair_rms(q_ref[rows_n, :], qg, lo)
        gate_t = lax.dot_general(gt_ref[...].astype(BF16), qn.astype(BF16), _NT,
                                 preferred_element_type=F32)
        bias_t = jnp.concatenate([block_bias(gate_t[0:nb8], n), filler,
                                  block_bias(gate_t[HEAD_DIM:HEAD_DIM + nb8], n), filler], axis=0)
        bias = bias_t.T
        qs = qn * QK_SCALE
        qa_ref[0, rows_n, :] = jnp.where(lo, qs, bias).astype(BF16)
        qa_ref[1, rows_n, :] = jnp.where(lo, bias, qs).astype(BF16)
        return carry

    lax.fori_loop(0, n_blk, prep_q, 0, unroll=PREP_UNROLL)

    def scores(h, n, j):
        s = lax.dot_general(qa_ref[h, rows_of(n), :], ka_ref[h, rows_of(j), :], _NT,
                            preferred_element_type=F32)
        return jnp.where(jnp.logical_or(tri, j != n), s, NEG)

    def normalise(n, a0, a1):
        o_ref[rows_of(n), :] = jnp.where(lo, a0 / a0[:, HEAD_DIM:HEAD_DIM + 1],
                                         a1 / a1[:, 0:1]).astype(o_ref.dtype)

    @pl.when(tab_ref[0, 0] == 1)
    def _():
        acc_ref[...] = jnp.zeros_like(acc_ref)

        def make_probs(trip, dst_ref):
            for g in range(grp):
                n = tab_ref[1, trip * grp + g]
                j = tab_ref[2, trip * grp + g]
                for h in range(2):
                    dst_ref[g, h] = jnp.exp(scores(h, n, j)).astype(BF16)

        def apply_probs(trip, src_ref):
            for g in range(grp):
                n = tab_ref[1, trip * grp + g]
                j = tab_ref[2, trip * grp + g]
                for h in range(2):
                    acc_ref[h, rows_of(n), :] += jnp.dot(src_ref[g, h], va_ref[h, rows_of(j), :],
                                                         preferred_element_type=F32)

        make_probs(0, pa_ref)

        def two_trips(i, carry):
            apply_probs(2 * i, pa_ref)
            make_probs(2 * i + 1, pb_ref)
            apply_probs(2 * i + 1, pb_ref)
            make_probs(2 * i + 2, pa_ref)
            return carry

        lax.fori_loop(0, n_trips // 2, two_trips, 0)

        def fin(n, carry):
            normalise(n, acc_ref[0, rows_of(n), :], acc_ref[1, rows_of(n), :])
            return carry

        lax.fori_loop(0, n_blk, fin, 0, unroll=PREP_UNROLL)

    @pl.when(tab_ref[0, 0] != 1)
    def _():
        def qblock(n, carry):
            def one(j, st):
                out = []
                for h in range(2):
                    m, acc = st[2 * h], st[2 * h + 1]
                    s = scores(h, n, j)
                    m_new = jnp.maximum(m, jnp.max(s, axis=-1, keepdims=True))
                    p = jnp.exp(s - m_new).astype(BF16)
                    acc = jnp.exp(m - m_new) * acc + jnp.dot(p, va_ref[h, rows_of(j), :],
                                                             preferred_element_type=F32)
                    out += [m_new, acc]
                return tuple(out)

            zero = jnp.zeros((blk, LANES), F32)
            ninf = jnp.full((blk, 1), -jnp.inf, F32)
            st = lax.fori_loop(0, n + 1, one, (ninf, zero, ninf, zero))
            normalise(n, st[1], st[3])
            return carry

        lax.fori_loop(0, n_blk, qblock, 0)


def _moba_tiles(n_blk):
    tiles = [(n, j) for n in range(n_blk) for j in range(n + 1)]
    per_two = 2 * MOBA_GROUP
    n_trips = 2 * (-(-len(tiles) // per_two))
    tiles += [(0, 1)] * ((n_trips + 1) * MOBA_GROUP - len(tiles))
    return n_trips, tiles


def _moba(proj, q_gain, k_gain, batch, seq):
    n_blk = seq // MOBA_BLOCK
    assert seq % MOBA_BLOCK == 0 and MOBA_TOPK <= n_blk <= HEAD_DIM - 8
    n_pairs = MOBA_WIDTH // LANES
    qg = jnp.tile(q_gain, 2).reshape(1, LANES)
    kg = jnp.tile(k_gain, 2).reshape(1, LANES)
    shift = HEAD_DIM * QK_SCALE * jnp.max(jnp.abs(q_gain)) * jnp.max(jnp.abs(k_gain))
    flag = (2.0 * shift <= SHIFT_SAFE).astype(jnp.int32)
    n_trips, tiles = _moba_tiles(n_blk)
    tab = jnp.stack([jnp.full((len(tiles),), flag, jnp.int32),
                     jnp.asarray([t[0] for t in tiles], jnp.int32),
                     jnp.asarray([t[1] for t in tiles], jnp.int32)])
    mrow = jnp.full((1, MOBA_BLOCK), shift, F32)
    blk = lambda off: pl.BlockSpec((seq, LANES), lambda b, p, f: (b, off + p))
    vec = lambda w: pl.BlockSpec((1, w), lambda b, p, f: (0, 0))
    prob = pltpu.VMEM((MOBA_GROUP, 2, MOBA_BLOCK, MOBA_BLOCK), BF16)
    grid_spec = pltpu.PrefetchScalarGridSpec(
        num_scalar_prefetch=1,
        grid=(batch, n_pairs),
        in_specs=[blk(0), blk(n_pairs), blk(2 * n_pairs), vec(LANES), vec(LANES), vec(MOBA_BLOCK)],
        out_specs=pl.BlockSpec((seq, LANES), lambda b, p, f: (b, p)),
        scratch_shapes=[pltpu.VMEM((2, seq, LANES), BF16),
                        pltpu.VMEM((2, seq, LANES), BF16),
                        pltpu.VMEM((2, seq, LANES), BF16),
                        pltpu.VMEM((LANES, LANES), F32),
                        pltpu.VMEM((2, seq, LANES), F32),
                        prob, prob],
    )
    return pl.pallas_call(
        functools.partial(_moba_kernel, n_blk=n_blk, n_trips=n_trips),
        grid_spec=grid_spec,
        out_shape=jax.ShapeDtypeStruct((batch * seq, MOBA_WIDTH), MIX_DTYPE),
        compiler_params=_params("parallel", "parallel"),
    )(tab, proj, proj, proj, qg, kg, mrow)


def _ret_kernel(q_ref, k_ref, v_ref, g_ref, cos_ref, sin_ref, intra_ref, kdec_ref, qdec_ref,
                cdm_ref, mask_ref, gain_ref, o_ref, *, n_ch):
    c = RET_CHUNK
    lane = _lane_iota((c, LANES))
    lo = lane < HEAD_DIM
    first_half = (lane % HEAD_DIM) < (HEAD_DIM // 2)
    gain = gain_ref[...]
    intra0 = intra_ref[0, 0]
    intra1 = intra_ref[0, 1]
    kdec = kdec_ref[0]
    qdec = qdec_ref[0]
    cdm = cdm_ref[0]
    mask = mask_ref[...]

    def rope(t, cos_t, sin_t):
        partner = jnp.where(first_half, pltpu.roll(t, LANES - HEAD_DIM // 2, 1),
                            pltpu.roll(t, HEAD_DIM // 2, 1))
        return t * cos_t + partner * sin_t

    def chunk(i, state):
        rows = pl.ds(pl.multiple_of(i * c, c), c)
        cos_t = cos_ref[rows, :]
        sin_t = sin_ref[rows, :]
        q = rope(q_ref[rows, :], cos_t, sin_t)
        k = rope(k_ref[rows, :], cos_t, sin_t) * QK_SCALE
        v = v_ref[rows, :]
        kb = k.astype(BF16)
        s0 = lax.dot_general(jnp.where(lo, q, 0.0).astype(BF16), kb, _NT,
                             preferred_element_type=F32) * intra0
        s1 = lax.dot_general(jnp.where(lo, 0.0, q).astype(BF16), kb, _NT,
                             preferred_element_type=F32) * intra1
        sc = jnp.concatenate([s0, s1], axis=1).astype(BF16)
        vv = jnp.concatenate([jnp.where(lo, v, 0.0), jnp.where(lo, 0.0, v)], axis=0).astype(BF16)
        o = jnp.dot(sc, vv, preferred_element_type=F32)
        o = o + jnp.dot(q.astype(BF16), state.astype(BF16), preferred_element_type=F32) * qdec
        kd = (k * kdec).astype(BF16)
        kv = lax.dot_general(kd, v.astype(BF16), _TN, preferred_element_type=F32)
        state = cdm * state + mask * kv
        o2 = o * o
        m0 = jnp.sum(jnp.where(lo, o2, 0.0), axis=-1, keepdims=True)
        m1 = jnp.sum(jnp.where(lo, 0.0, o2), axis=-1, keepdims=True)
        r = jnp.where(lo, lax.rsqrt(m0 / HEAD_DIM + EPS), lax.rsqrt(m1 / HEAD_DIM + EPS))
        gt = g_ref[rows, :]
        o_ref[rows, :] = ((gt * jax.nn.sigmoid(gt)) * (o * r * gain)).astype(o_ref.dtype)
        return state

    lax.fori_loop(0, n_ch, chunk, jnp.zeros((LANES, LANES), F32), unroll=PREP_UNROLL)


def _ret_tables(seq):
    c = RET_CHUNK
    half = HEAD_DIM // 2
    inv_freq = 1.0 / (RET_ROPE_BASE ** jnp.linspace(0.0, 1.0, half, dtype=F32))
    ang = jnp.arange(seq, dtype=F32)[:, None] * inv_freq[None, :]
    cos, sin = jnp.cos(ang), jnp.sin(ang)
    cos_t = jnp.tile(cos, (1, LANES // half))
    sin_t = jnp.tile(jnp.concatenate([-sin, sin], axis=1), (1, LANES // HEAD_DIM))
    log_gamma = jnp.log1p(-jnp.exp2(-5.0 - jnp.arange(RET_HEADS, dtype=F32)))
    idx = jnp.arange(c, dtype=F32)
    diff = idx[:, None] - idx[None, :]
    intra = jnp.where(diff >= 0, jnp.exp(log_gamma[:, None, None] * jnp.maximum(diff, 0.0)), 0.0)
    k_decay = jnp.exp(log_gamma[:, None] * (c - 1.0 - idx)[None, :])
    q_decay = jnp.exp(log_gamma[:, None] * (idx + 1.0)[None, :])
    chunk_decay = jnp.exp(log_gamma * c)
    n_pairs = RET_HEADS // 2

    def lanes(t):
        return jnp.repeat(t.reshape(n_pairs, 2, c).transpose(0, 2, 1), HEAD_DIM, axis=2)

    blockdiag = (jnp.arange(LANES)[:, None] // HEAD_DIM) == (jnp.arange(LANES)[None, :] // HEAD_DIM)
    mask = blockdiag.astype(F32)
    cdm = jnp.repeat(chunk_decay.reshape(n_pairs, 2), HEAD_DIM, axis=1)[:, :, None] * mask[None]
    return (cos_t, sin_t, intra.reshape(n_pairs, 2, c, c), lanes(k_decay), lanes(q_decay), cdm, mask)


def _retention(proj, out_gain, batch, seq):
    c = RET_CHUNK
    assert seq % c == 0
    n_pairs = RET_HEADS // 2
    base = 3 * MOBA_WIDTH // LANES
    cos_t, sin_t, intra, kdec, qdec, cdm, mask = _ret_tables(seq)
    blk = lambda off: pl.BlockSpec((seq, LANES), lambda b, p: (b, base + off + p))
    tab = pl.BlockSpec((seq, LANES), lambda b, p: (0, 0))
    return pl.pallas_call(
        functools.partial(_ret_kernel, n_ch=seq // c),
        grid=(batch, n_pairs),
        in_specs=[blk(0), blk(n_pairs), blk(2 * n_pairs), blk(3 * n_pairs), tab, tab,
                  pl.BlockSpec((1, 2, c, c), lambda b, p: (p, 0, 0, 0)),
                  pl.BlockSpec((1, c, LANES), lambda b, p: (p, 0, 0)),
                  pl.BlockSpec((1, c, LANES), lambda b, p: (p, 0, 0)),
                  pl.BlockSpec((1, LANES, LANES), lambda b, p: (p, 0, 0)),
                  pl.BlockSpec((LANES, LANES), lambda b, p: (0, 0)),
                  pl.BlockSpec((1, LANES), lambda b, p: (0, p))],
        out_specs=pl.BlockSpec((seq, LANES), lambda b, p: (b, p)),
        out_shape=jax.ShapeDtypeStruct((batch * seq, RET_WIDTH), MIX_DTYPE),
        compiler_params=_params("parallel", "parallel"),
    )(proj, proj, proj, proj, cos_t, sin_t, intra, kdec, qdec, cdm, mask,
      out_gain.reshape(1, RET_WIDTH))


def _mem_kv_kernel(m_ref, g_ref, w_ref, kg_ref, k_ref, v_ref):
    x = m_ref[0]
    ms = jnp.mean(x * x, axis=-1, keepdims=True)
    h = (x * lax.rsqrt(ms + EPS) * g_ref[...]).astype(BF16)
    kv = jnp.dot(h, w_ref[...], preferred_element_type=F32)
    lane = _lane_iota((x.shape[0], LANES))
    lo = lane < HEAD_DIM
    for p in range(MEM_WIDTH // LANES):
        kt = kv[:, p * LANES:(p + 1) * LANES]
        k_ref[0, :, p * LANES:(p + 1) * LANES] = _pair_rms(kt, kg_ref[...], lo).astype(BF16)
    v_ref[0] = kv[:, MEM_WIDTH:].astype(BF16)


def _mem_kv(mem, g_mem, w_mem_kv, k_gain):
    b, m, d = mem.shape
    return pl.pallas_call(
        _mem_kv_kernel,
        grid=(b,),
        in_specs=[pl.BlockSpec((1, m, d), lambda i: (i, 0, 0)),
                  pl.BlockSpec((1, d), lambda i: (0, 0)),
                  pl.BlockSpec((d, 2 * MEM_WIDTH), lambda i: (0, 0)),
                  pl.BlockSpec((1, LANES), lambda i: (0, 0))],
        out_specs=[pl.BlockSpec((1, m, MEM_WIDTH), lambda i: (i, 0, 0)),
                   pl.BlockSpec((1, m, MEM_WIDTH), lambda i: (i, 0, 0))],
        out_shape=[jax.ShapeDtypeStruct((b, m, MEM_WIDTH), BF16),
                   jax.ShapeDtypeStruct((b, m, MEM_WIDTH), BF16)],
        compiler_params=_params("parallel"),
    )(mem, g_mem.reshape(1, d), w_mem_kv.astype(BF16), jnp.tile(k_gain, 2).reshape(1, LANES))


def _mem_attn_kernel(q_ref, k_ref, v_ref, qg_ref, o_ref):
    tq = q_ref.shape[0]
    lane = _lane_iota((tq, LANES))
    lo = lane < HEAD_DIM
    for p in range(MEM_WIDTH // LANES):
        cols = slice(p * LANES, (p + 1) * LANES)
        qn = _pair_rms(q_ref[:, cols], qg_ref[...], lo) * QK_SCALE
        kt = k_ref[0, :, cols]
        vt = v_ref[0, :, cols]
        outs = []
        for h in range(2):
            qh = jnp.where(lo, qn, 0.0) if h == 0 else jnp.where(lo, 0.0, qn)
            s = lax.dot_general(qh.astype(BF16), kt, _NT, preferred_element_type=F32)
            m = jnp.max(s, axis=-1, keepdims=True)
            e = jnp.exp(s - m)
            pr = e / jnp.sum(e, axis=-1, keepdims=True)
            outs.append(jnp.dot(pr.astype(BF16), vt, preferred_element_type=F32))
        o_ref[:, cols] = jnp.where(lo, outs[0], outs[1]).astype(o_ref.dtype)


def _mem_attn(proj, km, vm, q_gain, batch, seq, tq):
    m = km.shape[1]
    qcol = (IN_WIDTH - MEM_WIDTH) // MEM_WIDTH
    assert qcol * MEM_WIDTH == IN_WIDTH - MEM_WIDTH
    nq = seq // tq
    return pl.pallas_call(
        _mem_attn_kernel,
        grid=(batch, nq),
        in_specs=[pl.BlockSpec((tq, MEM_WIDTH), lambda b, i: (b * nq + i, qcol)),
                  pl.BlockSpec((1, m, MEM_WIDTH), lambda b, i: (b, 0, 0)),
                  pl.BlockSpec((1, m, MEM_WIDTH), lambda b, i: (b, 0, 0)),
                  pl.BlockSpec((1, LANES), lambda b, i: (0, 0))],
        out_specs=pl.BlockSpec((tq, MEM_WIDTH), lambda b, i: (b * nq + i, 0)),
        out_shape=jax.ShapeDtypeStruct((batch * seq, MEM_WIDTH), MIX_DTYPE),
        compiler_params=_params("parallel", "parallel"),
    )(proj, km, vm, jnp.tile(q_gain, 2).reshape(1, LANES))


def _out_router_kernel(oa_ref, or_ref, om_ref, x_ref, wo_ref, g_ref, wr_ref, br_ref,
                       x2_ref, h2_ref, aux_ref, cnt_ref, run_ref):
    tm = x_ref.shape[0]

    @pl.when(pl.program_id(0) == 0)
    def _():
        run_ref[...] = jnp.zeros_like(run_ref)

    y = jnp.dot(oa_ref[...].astype(BF16), wo_ref[0:MOBA_WIDTH, :], preferred_element_type=F32)
    y += jnp.dot(or_ref[...].astype(BF16), wo_ref[MOBA_WIDTH:MOBA_WIDTH + RET_WIDTH, :],
                 preferred_element_type=F32)
    y += jnp.dot(om_ref[...].astype(BF16), wo_ref[MOBA_WIDTH + RET_WIDTH:, :],
                 preferred_element_type=F32)
    x2 = x_ref[...] + y
    x2_ref[...] = x2
    ms = jnp.mean(x2 * x2, axis=-1, keepdims=True)
    h2 = x2 * lax.rsqrt(ms + EPS) * g_ref[...]
    h2b = h2.astype(BF16)
    h2_ref[...] = _pack_bf16_pairs(h2)
    logits = jnp.dot(h2b, wr_ref[...], preferred_element_type=F32) + br_ref[...]
    lane = _lane_iota((tm, LANES))
    lane_f = lane.astype(F32)
    lg = jnp.where(lane < N_EXPERTS, logits, -jnp.inf)
    vals, picks = [], []
    for _ in range(TOP_K):
        m = jnp.max(lg, axis=-1, keepdims=True)
        first = jnp.min(jnp.where(lg == m, lane_f, jnp.inf), axis=-1, keepdims=True)
        pick = lane_f == first
        vals.append(m)
        picks.append(pick)
        lg = jnp.where(pick, -jnp.inf, lg)
    exps = [jnp.exp(v - vals[0]) for v in vals]
    denom = exps[0] + exps[1] + exps[2] + exps[3]
    sel = jnp.zeros((tm, LANES), F32)
    for pick in picks:
        sel = sel + jnp.where(pick, 1.0, 0.0)
    strict = (lax.broadcasted_iota(jnp.int32, (tm, tm), 0)
              > lax.broadcasted_iota(jnp.int32, (tm, tm), 1))
    pos = run_ref[0:1, :] + jnp.dot(jnp.where(strict, 1.0, 0.0).astype(BF16), sel.astype(BF16),
                                    preferred_element_type=F32)
    aux = jnp.zeros((tm, LANES), F32)
    for k in range(TOP_K):
        eid = jnp.sum(jnp.where(picks[k], lane_f, 0.0), axis=-1, keepdims=True)
        rank = jnp.sum(jnp.where(picks[k], pos, 0.0), axis=-1, keepdims=True)
        aux = aux + jnp.where(lane == k, eid, 0.0)
        aux = aux + jnp.where(lane == TOP_K + k, exps[k] / denom, 0.0)
        aux = aux + jnp.where(lane == 2 * TOP_K + k, rank, 0.0)
    aux_ref[...] = aux
    run = run_ref[0:1, :] + jnp.sum(sel, axis=0, keepdims=True)
    run_ref[...] = jnp.broadcast_to(run, run_ref.shape)
    cnt_ref[...] = jnp.broadcast_to(run, cnt_ref.shape)


def _out_router(oa, orr, om, x2d, w_out, g_ffn, w_router, b_router, tm):
    t, d = x2d.shape
    wr = jnp.zeros((d, LANES), BF16).at[:, :N_EXPERTS].set(w_router.astype(BF16))
    br = jnp.zeros((1, LANES), F32).at[0, :N_EXPERTS].set(b_router)
    row = lambda w: pl.BlockSpec((tm, w), lambda i: (i, 0))
    const = lambda r, c: pl.BlockSpec((r, c), lambda i: (0, 0))
    return pl.pallas_call(
        _out_router_kernel,
        grid=(t // tm,),
        in_specs=[row(MOBA_WIDTH), row(RET_WIDTH), row(MEM_WIDTH), row(d),
                  const(d, d), const(1, d), const(d, LANES), const(1, LANES)],
        out_specs=[row(d), row(d // 2), row(LANES), const(8, LANES)],
        out_shape=[jax.ShapeDtypeStruct((t, d), F32), jax.ShapeDtypeStruct((t, d // 2), jnp.uint32),
                   jax.ShapeDtypeStruct((t, LANES), F32), jax.ShapeDtypeStruct((8, LANES), F32)],
        scratch_shapes=[pltpu.VMEM((8, LANES), F32)],
        compiler_params=_params("arbitrary"),
    )(oa, orr, om, x2d, w_out.astype(BF16), g_ffn.reshape(1, d), wr, br)


MXU_COLS = 256


def _moe_kernel(be_ref, nu_ref, nv_ref, xs_ref, w1_ref, b1g_ref, b1l_ref, w2_ref, b2_ref, o_ref,
                w1p_ref, w2b_ref, act_ref):
    i = pl.program_id(0)
    half = MXU_COLS // 2
    n_chunks = w1_ref.shape[2] // MXU_COLS

    @pl.when(i < nu_ref[0])
    def _():
        @pl.when(jnp.logical_or(i == 0, be_ref[i] != be_ref[jnp.maximum(i - 1, 0)]))
        def _():
            r = lax.broadcasted_iota(jnp.int32, (MXU_COLS, MXU_COLS), 0)
            c = lax.broadcasted_iota(jnp.int32, (MXU_COLS, MXU_COLS), 1)
            src = jnp.where(c < half, 2 * c, 2 * (c - half) + 1)
            perm = jnp.where(r == src, 1.0, 0.0).astype(BF16)
            for ch in range(n_chunks):
                cols = slice(ch * MXU_COLS, (ch + 1) * MXU_COLS)
                w = w1_ref[0, :, cols].astype(BF16)
                w1p_ref[:, cols] = jnp.dot(w, perm, preferred_element_type=F32).astype(BF16)
            w2b_ref[...] = w2_ref[0].astype(BF16)

        row = lax.broadcasted_iota(jnp.int32, xs_ref.shape, 0)
        x_lo, x_hi = _unpack_bf16_pairs(jnp.where(row < nv_ref[i], xs_ref[...], jnp.uint32(0)))
        x = jnp.concatenate([x_lo.astype(BF16), x_hi.astype(BF16)], axis=1)
        for ch in range(n_chunks):
            z = jnp.dot(x, w1p_ref[:, ch * MXU_COLS:(ch + 1) * MXU_COLS], preferred_element_type=F32)
            hs = slice(ch * half, (ch + 1) * half)
            xg = jnp.minimum(z[:, :half] + b1g_ref[0][:, hs], SWIGLU_LIMIT)
            xl = jnp.clip(z[:, half:] + b1l_ref[0][:, hs], -SWIGLU_LIMIT, SWIGLU_LIMIT)
            act_ref[:, hs] = (xg * jax.nn.sigmoid(SWIGLU_ALPHA * xg) * (xl + 1.0)).astype(BF16)
        y = jnp.dot(act_ref[...], w2b_ref[...], preferred_element_type=F32) + b2_ref[0]
        o_ref[...] = _pack_bf16_pairs(y)

    @pl.when(i >= nu_ref[0])
    def _():
        o_ref[...] = jnp.zeros_like(o_ref)


def _moe_experts(xs, blk_exp, n_used, n_valid, w1, b1g, b1l, w2, b2):
    p_rows = xs.shape[0]
    d = w1.shape[1]
    d_ff = w2.shape[1]
    assert w1.shape[2] == 2 * d_ff and (2 * d_ff) % MXU_COLS == 0 and xs.shape[1] == d // 2
    n_blocks = p_rows // MOE_BLOCK
    rows = lambda i, be, nu, nv: (jnp.minimum(i, nu[0] - 1), 0)
    wsel = lambda i, be, nu, nv: (be[i], 0, 0)
    grid_spec = pltpu.PrefetchScalarGridSpec(
        num_scalar_prefetch=3,
        grid=(n_blocks,),
        in_specs=[pl.BlockSpec((MOE_BLOCK, d // 2), rows),
                  pl.BlockSpec((1, d, 2 * d_ff), wsel),
                  pl.BlockSpec((1, 1, d_ff), wsel), pl.BlockSpec((1, 1, d_ff), wsel),
                  pl.BlockSpec((1, d_ff, d), wsel), pl.BlockSpec((1, 1, d), wsel)],
        out_specs=pl.BlockSpec((MOE_BLOCK, d // 2), lambda i, be, nu, nv: (i, 0)),
        scratch_shapes=[pltpu.VMEM((d, 2 * d_ff), BF16), pltpu.VMEM((d_ff, d), BF16),
                        pltpu.VMEM((MOE_BLOCK, d_ff), BF16)],
    )
    return pl.pallas_call(
        _moe_kernel,
        grid_spec=grid_spec,
        out_shape=jax.ShapeDtypeStruct((p_rows, d // 2), jnp.uint32),
        compiler_params=pltpu.CompilerParams(dimension_semantics=("arbitrary",),
                                             vmem_limit_bytes=MOE_VMEM_LIMIT),
    )(blk_exp, n_used, n_valid, xs, w1, b1g, b1l, w2, b2)


def _pack_bf16_pairs(y):
    n = y.shape[1] // 2
    lo = lax.bitcast_convert_type(y[:, :n].astype(BF16).astype(F32), jnp.uint32)
    hi = lax.bitcast_convert_type(y[:, n:].astype(BF16).astype(F32), jnp.uint32)
    return (lo >> 16) | hi


def _unpack_bf16_pairs(w):
    lo = lax.bitcast_convert_type(w << 16, F32)
    hi = lax.bitcast_convert_type(w & jnp.uint32(0xFFFF0000), F32)
    return lo, hi


def _sc_gather_rows(src, idx):
    n, w = src.shape
    m = idx.shape[0]
    workers = SC_CORES * SC_SUBCORES
    chunk = SC_GATHER_ROWS
    assert m % (workers * chunk) == 0
    per_worker = m // workers
    mesh = plsc.VectorSubcoreMesh(core_axis_name="c", subcore_axis_name="s",
                                  num_cores=SC_CORES, num_subcores=SC_SUBCORES)

    @functools.partial(
        pl.kernel, mesh=mesh, out_type=jax.ShapeDtypeStruct((m, w), src.dtype),
        scratch_types=[pltpu.VMEM((chunk,), jnp.int32), pltpu.VMEM((chunk, w), src.dtype),
                       pltpu.SemaphoreType.DMA])
    def gather(src_hbm, idx_hbm, out_hbm, idx_v, rows_v, sem):
        base = (lax.axis_index("s") * SC_CORES + lax.axis_index("c")) * per_worker

        @pl.loop(0, per_worker // chunk)
        def _(c):
            off = base + c * chunk
            pltpu.sync_copy(idx_hbm.at[pl.ds(off, chunk)], idx_v)
            pltpu.async_copy(src_hbm.at[idx_v], rows_v, sem).wait()
            pltpu.sync_copy(rows_v, out_hbm.at[pl.ds(off, chunk)])

    return gather(src, idx)


def _sc_scatter_rows(src, idx_t, n_out):
    t, w = src.shape
    fan = idx_t.shape[0]
    workers = SC_CORES * SC_SUBCORES
    chunk = SC_GATHER_ROWS
    assert t % (workers * chunk) == 0 and idx_t.shape[1] == t
    per_worker = t // workers
    mesh = plsc.VectorSubcoreMesh(core_axis_name="c", subcore_axis_name="s",
                                  num_cores=SC_CORES, num_subcores=SC_SUBCORES)

    @functools.partial(
        pl.kernel, mesh=mesh, out_type=jax.ShapeDtypeStruct((n_out, w), src.dtype),
        scratch_types=[pltpu.VMEM((fan, chunk), jnp.int32), pltpu.VMEM((chunk, w), src.dtype)])
    def scatter(src_hbm, idx_hbm, out_hbm, idx_v, rows_v):
        base = (lax.axis_index("s") * SC_CORES + lax.axis_index("c")) * per_worker

        @pl.loop(0, per_worker // chunk)
        def _(c):
            off = base + c * chunk
            pltpu.sync_copy(idx_hbm.at[:, pl.ds(off, chunk)], idx_v)
            pltpu.sync_copy(src_hbm.at[pl.ds(off, chunk)], rows_v)
            for j in range(fan):
                pltpu.sync_copy(rows_v, out_hbm.at[idx_v.at[j]])

    return scatter(src, idx_t)


def _combine_kernel(x2_ref, aux_ref, yg_ref, o_ref):
    x2 = x2_ref[...]
    aux = aux_ref[...]
    half = x2.shape[1] // 2
    out_lo, out_hi = x2[:, :half], x2[:, half:]
    for k in range(TOP_K):
        lo, hi = _unpack_bf16_pairs(yg_ref[k])
        wk = aux[:, TOP_K + k:TOP_K + k + 1]
        out_lo = out_lo + wk * lo
        out_hi = out_hi + wk * hi
    o_ref[:, :half] = out_lo
    o_ref[:, half:] = out_hi


def _combine(x2, aux, yg, tm):
    t, d = x2.shape
    return pl.pallas_call(
        _combine_kernel,
        grid=(t // tm,),
        in_specs=[pl.BlockSpec((tm, d), lambda i: (i, 0)),
                  pl.BlockSpec((tm, LANES), lambda i: (i, 0)),
                  pl.BlockSpec((TOP_K, tm, d // 2), lambda i: (0, i, 0))],
        out_specs=pl.BlockSpec((tm, d), lambda i: (i, 0)),
        out_shape=jax.ShapeDtypeStruct((t, d), F32),
        compiler_params=_params("parallel"),
    )(x2, aux, yg)


def _layer(x, mem, g_attn, w_in, moba_q_gain, moba_k_gain, ret_out_gain, g_mem, w_mem_kv,
           mem_q_gain, mem_k_gain, w_out, g_ffn, w_router, b_router, w1, b1, w2, b2):
    batch, seq, d = x.shape
    t = batch * seq
    x2d = x.reshape(t, d)
    tm = min(512, t)

    proj = _rms_proj(x2d, g_attn, w_in.astype(BF16), tm)
    oa = _moba(proj, moba_q_gain, moba_k_gain, batch, seq)
    orr = _retention(proj, ret_out_gain, batch, seq)
    km, vm = _mem_kv(mem, g_mem, w_mem_kv, mem_k_gain)
    om = _mem_attn(proj, km, vm, mem_q_gain, batch, seq, min(512, seq))
    x2, h2, aux, counts = _out_router(oa, orr, om, x2d, w_out, g_ffn, w_router, b_router, tm)

    eidx = aux[:, 0:TOP_K].astype(jnp.int32)
    rank = aux[:, 2 * TOP_K:3 * TOP_K].astype(jnp.int32)
    cnt = counts[0, :N_EXPERTS].astype(jnp.int32)
    padded = ((cnt + MOE_BLOCK - 1) // MOE_BLOCK) * MOE_BLOCK
    pends = jnp.cumsum(padded)
    pstarts = pends - padded
    dest = pstarts[eidx] + rank
    a = t * TOP_K
    n_blocks = -(-(a + N_EXPERTS * (MOE_BLOCK - 1)) // MOE_BLOCK)
    p_rows = n_blocks * MOE_BLOCK
    blk_start = jnp.arange(n_blocks, dtype=jnp.int32) * MOE_BLOCK
    blk_exp = jnp.minimum(jnp.sum(blk_start[:, None] >= pends[None, :], axis=1),
                          N_EXPERTS - 1).astype(jnp.int32)
    n_used = (pends[-1] // MOE_BLOCK).astype(jnp.int32).reshape(1)

    n_valid = jnp.clip((pstarts + cnt)[blk_exp] - blk_start, 0, MOE_BLOCK).astype(jnp.int32)
    dest_t = dest.T
    xs = _sc_scatter_rows(h2, dest_t, p_rows)

    d_ff = w2.shape[1]
    b1g = b1[:, 0::2].reshape(N_EXPERTS, 1, d_ff)
    b1l = b1[:, 1::2].reshape(N_EXPERTS, 1, d_ff)
    ys = _moe_experts(xs, blk_exp, n_used, n_valid, w1, b1g, b1l, w2,
                      b2.reshape(N_EXPERTS, 1, d))
    yg = _sc_gather_rows(ys, dest_t.reshape(a)).reshape(TOP_K, t, d // 2)
    out = _combine(x2, aux, yg, min(256, t))
    return out.reshape(batch, seq, d)


def kernel(x, mem, g_attn, w_in, moba_q_gain, moba_k_gain, ret_out_gain, g_mem, w_mem_kv,
           mem_q_gain, mem_k_gain, w_out, g_ffn, w_router, b_router, w1, b1, w2, b2):
    for l in range(g_attn.shape[0]):
        x = _layer(x, mem, g_attn[l], w_in[l], moba_q_gain[l], moba_k_gain[l], ret_out_gain[l],
                   g_mem[l], w_mem_kv[l], mem_q_gain[l], mem_k_gain[l], w_out[l], g_ffn[l],
                   w_router[l], b_router[l], w1[l], b1[l], w2[l], b2[l])
    return x
```

```python
import functools

import jax
import jax.numpy as jnp
from jax import lax
from jax.experimental import pallas as pl
from jax.experimental.pallas import tpu as pltpu
from jax.experimental.pallas import tpu_sc as plsc

F32 = jnp.float32
BF16 = jnp.bfloat16

LANES = 128
HEAD_DIM = 64
MOBA_HEADS = 8
RET_HEADS = 4
MEM_HEADS = 4
MOBA_WIDTH = MOBA_HEADS * HEAD_DIM
RET_WIDTH = RET_HEADS * HEAD_DIM
MEM_WIDTH = MEM_HEADS * HEAD_DIM
IN_WIDTH = 3 * MOBA_WIDTH + 4 * RET_WIDTH + MEM_WIDTH
MOBA_BLOCK = 256
MOBA_TOPK = 3
RET_CHUNK = 128
RET_ROPE_BASE = 10000.0
N_EXPERTS = 32
TOP_K = 4
SWIGLU_ALPHA = 1.702
SWIGLU_LIMIT = 7.0
MOE_BLOCK = 512
PREP_UNROLL = 4
EPS = 1e-6
NEG = -1e30
QK_SCALE = HEAD_DIM ** -0.5

VMEM_LIMIT = 48 * 1024 * 1024
MOE_VMEM_LIMIT = 56 * 1024 * 1024
MIX_DTYPE = BF16
SC_CORES = 2
SC_SUBCORES = 16
SC_GATHER_ROWS = 128

_NT = (((1,), (1,)), ((), ()))
_TN = (((0,), (0,)), ((), ()))


def _params(*sem):
    return pltpu.CompilerParams(dimension_semantics=sem, vmem_limit_bytes=VMEM_LIMIT)


def _lane_iota(shape):
    return lax.broadcasted_iota(jnp.int32, shape, len(shape) - 1)


def _pair_rms(t, gain, lo):
    t2 = t * t
    s0 = jnp.sum(jnp.where(lo, t2, 0.0), axis=-1, keepdims=True)
    s1 = jnp.sum(jnp.where(lo, 0.0, t2), axis=-1, keepdims=True)
    r = jnp.where(lo, lax.rsqrt(s0 / HEAD_DIM + EPS), lax.rsqrt(s1 / HEAD_DIM + EPS))
    return t * r * gain


def _rms_proj_kernel(x_ref, g_ref, w_ref, o_ref):
    x = x_ref[...]
    ms = jnp.mean(x * x, axis=-1, keepdims=True)
    h = (x * lax.rsqrt(ms + EPS) * g_ref[...]).astype(BF16)
    o_ref[...] = jnp.dot(h, w_ref[...], preferred_element_type=F32)


def _rms_proj(x2d, gain, w, tm):
    t, d = x2d.shape
    n = w.shape[1]
    return pl.pallas_call(
        _rms_proj_kernel,
        grid=(t // tm,),
        in_specs=[pl.BlockSpec((tm, d), lambda i: (i, 0)),
                  pl.BlockSpec((1, d), lambda i: (0, 0)),
                  pl.BlockSpec((d, n), lambda i: (0, 0))],
        out_specs=pl.BlockSpec((tm, n), lambda i: (i, 0)),
        out_shape=jax.ShapeDtypeStruct((t, n), F32),
        compiler_params=_params("parallel"),
    )(x2d, gain.reshape(1, d), w)


SHIFT_SAFE = 80.0
MOBA_GROUP = 4


def _moba_kernel(tab_ref, q_ref, k_ref, v_ref, qg_ref, kg_ref, m_ref, o_ref,
                 ka_ref, va_ref, qa_ref, gt_ref, acc_ref, pa_ref, pb_ref, *, n_blk, n_trips):
    blk = MOBA_BLOCK
    grp = MOBA_GROUP
    nb8 = -(-n_blk // 8) * 8
    lane = _lane_iota((blk, LANES))
    lo = lane < HEAD_DIM
    lane1 = _lane_iota((1, LANES))
    qg = qg_ref[...]
    kg = kg_ref[...]
    tri = (lax.broadcasted_iota(jnp.int32, (blk, blk), 0)
           >= lax.broadcasted_iota(jnp.int32, (blk, blk), 1))
    row_t = lax.broadcasted_iota(jnp.int32, (nb8, blk), 0)
    row_tf = row_t.astype(F32)
    fill_row = lax.broadcasted_iota(jnp.int32, (HEAD_DIM - nb8, blk), 0)
    filler = jnp.where(fill_row == HEAD_DIM - nb8 - 1, -m_ref[...], 0.0)

    def rows_of(j):
        return pl.ds(pl.multiple_of(j * blk, blk), blk)

    gt_ref[...] = jnp.zeros_like(gt_ref)

    def prep_k(j, carry):
        rows = rows_of(j)
        kn = _pair_rms(k_ref[rows, :], kg, lo)
        km = jnp.mean(kn, axis=0, keepdims=True)
        gt_ref[pl.ds(HEAD_DIM + j, 1), :] = jnp.where(lane1 < HEAD_DIM, km, 0.0)
        gt_ref[pl.ds(j, 1), :] = jnp.where(lane1 < HEAD_DIM, 0.0, km)
        tag0 = jnp.where(jnp.logical_or(lane == HEAD_DIM + j, lane == LANES - 1), 1.0, 0.0)
        tag1 = jnp.where(jnp.logical_or(lane == j, lane == HEAD_DIM - 1), 1.0, 0.0)
        ka_ref[0, rows, :] = jnp.where(lo, kn, tag0).astype(BF16)
        ka_ref[1, rows, :] = jnp.where(lo, tag1, kn).astype(BF16)
        v = v_ref[rows, :]
        va_ref[0, rows, :] = jnp.where(lo, v, jnp.where(lane == HEAD_DIM, 1.0, 0.0)).astype(BF16)
        va_ref[1, rows, :] = jnp.where(lo, jnp.where(lane == 0, 1.0, 0.0), v).astype(BF16)
        return carry

    lax.fori_loop(0, n_blk, prep_k, 0, unroll=PREP_UNROLL)

    def block_bias(g, n):
        valid = row_t < n
        g = jnp.where(valid, g, NEG)
        sel = jnp.zeros((nb8, blk), jnp.bool_)
        for _ in range(MOBA_TOPK):
            m = jnp.max(g, axis=0, keepdims=True)
            first = jnp.min(jnp.where(g == m, row_tf, jnp.inf), axis=0, keepdims=True)
            pick = row_tf == first
            sel = jnp.logical_or(sel, pick)
            g = jnp.where(pick, -jnp.inf, g)
        keep = jnp.logical_or(jnp.logical_and(sel, valid), row_t == n)
        return jnp.where(keep, 0.0, NEG)

    def prep_q(n, carry):
        rows_n = rows_of(n)
        qn = _pair_rms(q_ref[rows_n, :], qg, lo)
        gate_t = lax.dot_general(gt_ref[...].astype(BF16), qn.astype(BF16), _NT,
                                 preferred_element_type=F32)
        bias_t = jnp.concatenate([block_bias(gate_t[0:nb8], n), filler,
                                  block_bias(gate_t[HEAD_DIM:HEAD_DIM + nb8], n), filler], axis=0)
        bias = bias_t.T
        qs = qn * QK_SCALE
        qa_ref[0, rows_n, :] = jnp.where(lo, qs, bias).astype(BF16)
        qa_ref[1, rows_n, :] = jnp.where(lo, bias, qs).astype(BF16)
        return carry

    lax.fori_loop(0, n_blk, prep_q, 0, unroll=PREP_UNROLL)

    def scores(h, n, j):
        s = lax.dot_general(qa_ref[h, rows_of(n), :], ka_ref[h, rows_of(j), :], _NT,
                            preferred_element_type=F32)
        return jnp.where(jnp.logical_or(tri, j != n), s, NEG)

    def normalise(n, a0, a1):
        o_ref[rows_of(n), :] = jnp.where(lo, a0 / a0[:, HEAD_DIM:HEAD_DIM + 1],
                                         a1 / a1[:, 0:1]).astype(o_ref.dtype)

    @pl.when(tab_ref[0, 0] == 1)
    def _():
        acc_ref[...] = jnp.zeros_like(acc_ref)

        def make_probs(trip, dst_ref):
            for g in range(grp):
                n = tab_ref[1, trip * grp + g]
                j = tab_ref[2, trip * grp + g]
                for h in range(2):
                    dst_ref[g, h] = jnp.exp(scores(h, n, j)).astype(BF16)

        def apply_probs(trip, src_ref):
            for g in range(grp):
                n = tab_ref[1, trip * grp + g]
                j = tab_ref[2, trip * grp + g]
                for h in range(2):
                    acc_ref[h, rows_of(n), :] += jnp.dot(src_ref[g, h], va_ref[h, rows_of(j), :],
                                                         preferred_element_type=F32)

        make_probs(0, pa_ref)

        def two_trips(i, carry):
            apply_probs(2 * i, pa_ref)
            make_probs(2 * i + 1, pb_ref)
            apply_probs(2 * i + 1, pb_ref)
            make_probs(2 * i + 2, pa_ref)
            return carry

        lax.fori_loop(0, n_trips // 2, two_trips, 0)

        def fin(n, carry):
            normalise(n, acc_ref[0, rows_of(n), :], acc_ref[1, rows_of(n), :])
            return carry

        lax.fori_loop(0, n_blk, fin, 0, unroll=PREP_UNROLL)

    @pl.when(tab_ref[0, 0] != 1)
    def _():
        def qblock(n, carry):
            def one(j, st):
                out = []
                for h in range(2):
                    m, acc = st[2 * h], st[2 * h + 1]
                    s = scores(h, n, j)
                    m_new = jnp.maximum(m, jnp.max(s, axis=-1, keepdims=True))
                    p = jnp.exp(s - m_new).astype(BF16)
                    acc = jnp.exp(m - m_new) * acc + jnp.dot(p, va_ref[h, rows_of(j), :],
                                                             preferred_element_type=F32)
                    out += [m_new, acc]
                return tuple(out)

            zero = jnp.zeros((blk, LANES), F32)
            ninf = jnp.full((blk, 1), -jnp.inf, F32)
            st = lax.fori_loop(0, n + 1, one, (ninf, zero, ninf, zero))
            normalise(n, st[1], st[3])
            return carry

        lax.fori_loop(0, n_blk, qblock, 0)


def _moba_tiles(n_blk):
    tiles = [(n, j) for n in range(n_blk) for j in range(n + 1)]
    per_two = 2 * MOBA_GROUP
    n_trips = 2 * (-(-len(tiles) // per_two))
    tiles += [(0, 1)] * ((n_trips + 1) * MOBA_GROUP - len(tiles))
    return n_trips, tiles


def _moba(proj, q_gain, k_gain, batch, seq):
    n_blk = seq // MOBA_BLOCK
    assert seq % MOBA_BLOCK == 0 and MOBA_TOPK <= n_blk <= HEAD_DIM - 8
    n_pairs = MOBA_WIDTH // LANES
    qg = jnp.tile(q_gain, 2).reshape(1, LANES)
    kg = jnp.tile(k_gain, 2).reshape(1, LANES)
    shift = HEAD_DIM * QK_SCALE * jnp.max(jnp.abs(q_gain)) * jnp.max(jnp.abs(k_gain))
    flag = (2.0 * shift <= SHIFT_SAFE).astype(jnp.int32)
    n_trips, tiles = _moba_tiles(n_blk)
    tab = jnp.stack([jnp.full((len(tiles),), flag, jnp.int32),
                     jnp.asarray([t[0] for t in tiles], jnp.int32),
                     jnp.asarray([t[1] for t in tiles], jnp.int32)])
    mrow = jnp.full((1, MOBA_BLOCK), shift, F32)
    blk = lambda off: pl.BlockSpec((seq, LANES), lambda b, p, f: (b, off + p))
    vec = lambda w: pl.BlockSpec((1, w), lambda b, p, f: (0, 0))
    prob = pltpu.VMEM((MOBA_GROUP, 2, MOBA_BLOCK, MOBA_BLOCK), BF16)
    grid_spec = pltpu.PrefetchScalarGridSpec(
        num_scalar_prefetch=1,
        grid=(batch, n_pairs),
        in_specs=[blk(0), blk(n_pairs), blk(2 * n_pairs), vec(LANES), vec(LANES), vec(MOBA_BLOCK)],
        out_specs=pl.BlockSpec((seq, LANES), lambda b, p, f: (b, p)),
        scratch_shapes=[pltpu.VMEM((2, seq, LANES), BF16),
                        pltpu.VMEM((2, seq, LANES), BF16),
                        pltpu.VMEM((2, seq, LANES), BF16),
                        pltpu.VMEM((LANES, LANES), F32),
                        pltpu.VMEM((2, seq, LANES), F32),
                        prob, prob],
    )
    return pl.pallas_call(
        functools.partial(_moba_kernel, n_blk=n_blk, n_trips=n_trips),
        grid_spec=grid_spec,
        out_shape=jax.ShapeDtypeStruct((batch * seq, MOBA_WIDTH), MIX_DTYPE),
        compiler_params=_params("parallel", "parallel"),
    )(tab, proj, proj, proj, qg, kg, mrow)


def _ret_kernel(q_ref, k_ref, v_ref, g_ref, cos_ref, sin_ref, intra_ref, kdec_ref, qdec_ref,
                cdm_ref, mask_ref, gain_ref, o_ref, *, n_ch):
    c = RET_CHUNK
    lane = _lane_iota((c, LANES))
    lo = lane < HEAD_DIM
    first_half = (lane % HEAD_DIM) < (HEAD_DIM // 2)
    gain = gain_ref[...]
    intra0 = intra_ref[0, 0]
    intra1 = intra_ref[0, 1]
    kdec = kdec_ref[0]
    qdec = qdec_ref[0]
    cdm = cdm_ref[0]
    mask = mask_ref[...]

    def rope(t, cos_t, sin_t):
        partner = jnp.where(first_half, pltpu.roll(t, LANES - HEAD_DIM // 2, 1),
                            pltpu.roll(t, HEAD_DIM // 2, 1))
        return t * cos_t + partner * sin_t

    def chunk(i, state):
        rows = pl.ds(pl.multiple_of(i * c, c), c)
        cos_t = cos_ref[rows, :]
        sin_t = sin_ref[rows, :]
        q = rope(q_ref[rows, :], cos_t, sin_t)
        k = rope(k_ref[rows, :], cos_t, sin_t) * QK_SCALE
        v = v_ref[rows, :]
        kb = k.astype(BF16)
        s0 = lax.dot_general(jnp.where(lo, q, 0.0).astype(BF16), kb, _NT,
                             preferred_element_type=F32) * intra0
        s1 = lax.dot_general(jnp.where(lo, 0.0, q).astype(BF16), kb, _NT,
                             preferred_element_type=F32) * intra1
        sc = jnp.concatenate([s0, s1], axis=1).astype(BF16)
        vv = jnp.concatenate([jnp.where(lo, v, 0.0), jnp.where(lo, 0.0, v)], axis=0).astype(BF16)
        o = jnp.dot(sc, vv, preferred_element_type=F32)
        o = o + jnp.dot(q.astype(BF16), state.astype(BF16), preferred_element_type=F32) * qdec
        kd = (k * kdec).astype(BF16)
        kv = lax.dot_general(kd, v.astype(BF16), _TN, preferred_element_type=F32)
        state = cdm * state + mask * kv
        o2 = o * o
        m0 = jnp.sum(jnp.where(lo, o2, 0.0), axis=-1, keepdims=True)
        m1 = jnp.sum(jnp.where(lo, 0.0, o2), axis=-1, keepdims=True)
        r = jnp.where(lo, lax.rsqrt(m0 / HEAD_DIM + EPS), lax.rsqrt(m1 / HEAD_DIM + EPS))
        gt = g_ref[rows, :]
        o_ref[rows, :] = ((gt * jax.nn.sigmoid(gt)) * (o * r * gain)).astype(o_ref.dtype)
        return state

    lax.fori_loop(0, n_ch, chunk, jnp.zeros((LANES, LANES), F32), unroll=PREP_UNROLL)


def _ret_tables(seq):
    c = RET_CHUNK
    half = HEAD_DIM // 2
    inv_freq = 1.0 / (RET_ROPE_BASE ** jnp.linspace(0.0, 1.0, half, dtype=F32))
    ang = jnp.arange(seq, dtype=F32)[:, None] * inv_freq[None, :]
    cos, sin = jnp.cos(ang), jnp.sin(ang)
    cos_t = jnp.tile(cos, (1, LANES // half))
    sin_t = jnp.tile(jnp.concatenate([-sin, sin], axis=1), (1, LANES // HEAD_DIM))
    log_gamma = jnp.log1p(-jnp.exp2(-5.0 - jnp.arange(RET_HEADS, dtype=F32)))
    idx = jnp.arange(c, dtype=F32)
    diff = idx[:, None] - idx[None, :]
    intra = jnp.where(diff >= 0, jnp.exp(log_gamma[:, None, None] * jnp.maximum(diff, 0.0)), 0.0)
    k_decay = jnp.exp(log_gamma[:, None] * (c - 1.0 - idx)[None, :])
    q_decay = jnp.exp(log_gamma[:, None] * (idx + 1.0)[None, :])
    chunk_decay = jnp.exp(log_gamma * c)
    n_pairs = RET_HEADS // 2

    def lanes(t):
        return jnp.repeat(t.reshape(n_pairs, 2, c).transpose(0, 2, 1), HEAD_DIM, axis=2)

    blockdiag = (jnp.arange(LANES)[:, None] // HEAD_DIM) == (jnp.arange(LANES)[None, :] // HEAD_DIM)
    mask = blockdiag.astype(F32)
    cdm = jnp.repeat(chunk_decay.reshape(n_pairs, 2), HEAD_DIM, axis=1)[:, :, None] * mask[None]
    return (cos_t, sin_t, intra.reshape(n_pairs, 2, c, c), lanes(k_decay), lanes(q_decay), cdm, mask)


def _retention(proj, out_gain, batch, seq):
    c = RET_CHUNK
    assert seq % c == 0
    n_pairs = RET_HEADS // 2
    base = 3 * MOBA_WIDTH // LANES
    cos_t, sin_t, intra, kdec, qdec, cdm, mask = _ret_tables(seq)
    blk = lambda off: pl.BlockSpec((seq, LANES), lambda b, p: (b, base + off + p))
    tab = pl.BlockSpec((seq, LANES), lambda b, p: (0, 0))
    return pl.pallas_call(
        functools.partial(_ret_kernel, n_ch=seq // c),
        grid=(batch, n_pairs),
        in_specs=[blk(0), blk(n_pairs), blk(2 * n_pairs), blk(3 * n_pairs), tab, tab,
                  pl.BlockSpec((1, 2, c, c), lambda b, p: (p, 0, 0, 0)),
                  pl.BlockSpec((1, c, LANES), lambda b, p: (p, 0, 0)),
                  pl.BlockSpec((1, c, LANES), lambda b, p: (p, 0, 0)),
                  pl.BlockSpec((1, LANES, LANES), lambda b, p: (p, 0, 0)),
                  pl.BlockSpec((LANES, LANES), lambda b, p: (0, 0)),
                  pl.BlockSpec((1, LANES), lambda b, p: (0, p))],
        out_specs=pl.BlockSpec((seq, LANES), lambda b, p: (b, p)),
        out_shape=jax.ShapeDtypeStruct((batch * seq, RET_WIDTH), MIX_DTYPE),
        compiler_params=_params("parallel", "parallel"),
    )(proj, proj, proj, proj, cos_t, sin_t, intra, kdec, qdec, cdm, mask,
      out_gain.reshape(1, RET_WIDTH))


def _mem_kv_kernel(m_ref, g_ref, w_ref, kg_ref, k_ref, v_ref):
    x = m_ref[0]
    ms = jnp.mean(x * x, axis=-1, keepdims=True)
    h = (x * lax.rsqrt(ms + EPS) * g_ref[...]).astype(BF16)
    kv = jnp.dot(h, w_ref[...], preferred_element_type=F32)
    lane = _lane_iota((x.shape[0], LANES))
    lo = lane < HEAD_DIM
    for p in range(MEM_WIDTH // LANES):
        kt = kv[:, p * LANES:(p + 1) * LANES]
        k_ref[0, :, p * LANES:(p + 1) * LANES] = _pair_rms(kt, kg_ref[...], lo).astype(BF16)
    v_ref[0] = kv[:, MEM_WIDTH:].astype(BF16)


def _mem_kv(mem, g_mem, w_mem_kv, k_gain):
    b, m, d = mem.shape
    return pl.pallas_call(
        _mem_kv_kernel,
        grid=(b,),
        in_specs=[pl.BlockSpec((1, m, d), lambda i: (i, 0, 0)),
                  pl.BlockSpec((1, d), lambda i: (0, 0)),
                  pl.BlockSpec((d, 2 * MEM_WIDTH), lambda i: (0, 0)),
                  pl.BlockSpec((1, LANES), lambda i: (0, 0))],
        out_specs=[pl.BlockSpec((1, m, MEM_WIDTH), lambda i: (i, 0, 0)),
                   pl.BlockSpec((1, m, MEM_WIDTH), lambda i: (i, 0, 0))],
        out_shape=[jax.ShapeDtypeStruct((b, m, MEM_WIDTH), BF16),
                   jax.ShapeDtypeStruct((b, m, MEM_WIDTH), BF16)],
        compiler_params=_params("parallel"),
    )(mem, g_mem.reshape(1, d), w_mem_kv.astype(BF16), jnp.tile(k_gain, 2).reshape(1, LANES))


def _mem_attn_kernel(q_ref, k_ref, v_ref, qg_ref, o_ref):
    tq = q_ref.shape[0]
    lane = _lane_iota((tq, LANES))
    lo = lane < HEAD_DIM
    for p in range(MEM_WIDTH // LANES):
        cols = slice(p * LANES, (p + 1) * LANES)
        qn = _pair_rms(q_ref[:, cols], qg_ref[...], lo) * QK_SCALE
        kt = k_ref[0, :, cols]
        vt = v_ref[0, :, cols]
        outs = []
        for h in range(2):
            qh = jnp.where(lo, qn, 0.0) if h == 0 else jnp.where(lo, 0.0, qn)
            s = lax.dot_general(qh.astype(BF16), kt, _NT, preferred_element_type=F32)
            m = jnp.max(s, axis=-1, keepdims=True)
            e = jnp.exp(s - m)
            pr = e / jnp.sum(e, axis=-1, keepdims=True)
            outs.append(jnp.dot(pr.astype(BF16), vt, preferred_element_type=F32))
        o_ref[:, cols] = jnp.where(lo, outs[0], outs[1]).astype(o_ref.dtype)


def _mem_attn(proj, km, vm, q_gain, batch, seq, tq):
    m = km.shape[1]
    qcol = (IN_WIDTH - MEM_WIDTH) // MEM_WIDTH
    assert qcol * MEM_WIDTH == IN_WIDTH - MEM_WIDTH
    nq = seq // tq
    return pl.pallas_call(
        _mem_attn_kernel,
        grid=(batch, nq),
        in_specs=[pl.BlockSpec((tq, MEM_WIDTH), lambda b, i: (b * nq + i, qcol)),
                  pl.BlockSpec((1, m, MEM_WIDTH), lambda b, i: (b, 0, 0)),
                  pl.BlockSpec((1, m, MEM_WIDTH), lambda b, i: (b, 0, 0)),
                  pl.BlockSpec((1, LANES), lambda b, i: (0, 0))],
        out_specs=pl.BlockSpec((tq, MEM_WIDTH), lambda b, i: (b * nq + i, 0)),
        out_shape=jax.ShapeDtypeStruct((batch * seq, MEM_WIDTH), MIX_DTYPE),
        compiler_params=_params("parallel", "parallel"),
    )(proj, km, vm, jnp.tile(q_gain, 2).reshape(1, LANES))


def _out_router_kernel(oa_ref, or_ref, om_ref, x_ref, wo_ref, g_ref, wr_ref, br_ref,
                       x2_ref, h2_ref, aux_ref, cnt_ref, run_ref):
    tm = x_ref.shape[0]

    @pl.when(pl.program_id(0) == 0)
    def _():
        run_ref[...] = jnp.zeros_like(run_ref)

    y = jnp.dot(oa_ref[...].astype(BF16), wo_ref[0:MOBA_WIDTH, :], preferred_element_type=F32)
    y += jnp.dot(or_ref[...].astype(BF16), wo_ref[MOBA_WIDTH:MOBA_WIDTH + RET_WIDTH, :],
                 preferred_element_type=F32)
    y += jnp.dot(om_ref[...].astype(BF16), wo_ref[MOBA_WIDTH + RET_WIDTH:, :],
                 preferred_element_type=F32)
    x2 = x_ref[...] + y
    x2_ref[...] = x2
    ms = jnp.mean(x2 * x2, axis=-1, keepdims=True)
    h2 = x2 * lax.rsqrt(ms + EPS) * g_ref[...]
    h2b = h2.astype(BF16)
    h2_ref[...] = _pack_bf16_pairs(h2)
    logits = jnp.dot(h2b, wr_ref[...], preferred_element_type=F32) + br_ref[...]
    lane = _lane_iota((tm, LANES))
    lane_f = lane.astype(F32)
    lg = jnp.where(lane < N_EXPERTS, logits, -jnp.inf)
    vals, picks = [], []
    for _ in range(TOP_K):
        m = jnp.max(lg, axis=-1, keepdims=True)
        first = jnp.min(jnp.where(lg == m, lane_f, jnp.inf), axis=-1, keepdims=True)
        pick = lane_f == first
        vals.append(m)
        picks.append(pick)
        lg = jnp.where(pick, -jnp.inf, lg)
    exps = [jnp.exp(v - vals[0]) for v in vals]
    denom = exps[0] + exps[1] + exps[2] + exps[3]
    sel = jnp.zeros((tm, LANES), F32)
    for pick in picks:
        sel = sel + jnp.where(pick, 1.0, 0.0)
    strict = (lax.broadcasted_iota(jnp.int32, (tm, tm), 0)
              > lax.broadcasted_iota(jnp.int32, (tm, tm), 1))
    pos = run_ref[0:1, :] + jnp.dot(jnp.where(strict, 1.0, 0.0).astype(BF16), sel.astype(BF16),
                                    preferred_element_type=F32)
    aux = jnp.zeros((tm, LANES), F32)
    for k in range(TOP_K):
        eid = jnp.sum(jnp.where(picks[k], lane_f, 0.0), axis=-1, keepdims=True)
        rank = jnp.sum(jnp.where(picks[k], pos, 0.0), axis=-1, keepdims=True)
        aux = aux + jnp.where(lane == k, eid, 0.0)
        aux = aux + jnp.where(lane == TOP_K + k, exps[k] / denom, 0.0)
        aux = aux + jnp.where(lane == 2 * TOP_K + k, rank, 0.0)
    aux_ref[...] = aux
    run = run_ref[0:1, :] + jnp.sum(sel, axis=0, keepdims=True)
    run_ref[...] = jnp.broadcast_to(run, run_ref.shape)
    cnt_ref[...] = jnp.broadcast_to(run, cnt_ref.shape)


def _out_router(oa, orr, om, x2d, w_out, g_ffn, w_router, b_router, tm):
    t, d = x2d.shape
    wr = jnp.zeros((d, LANES), BF16).at[:, :N_EXPERTS].set(w_router.astype(BF16))
    br = jnp.zeros((1, LANES), F32).at[0, :N_EXPERTS].set(b_router)
    row = lambda w: pl.BlockSpec((tm, w), lambda i: (i, 0))
    const = lambda r, c: pl.BlockSpec((r, c), lambda i: (0, 0))
    return pl.pallas_call(
        _out_router_kernel,
        grid=(t // tm,),
        in_specs=[row(MOBA_WIDTH), row(RET_WIDTH), row(MEM_WIDTH), row(d),
                  const(d, d), const(1, d), const(d, LANES), const(1, LANES)],
        out_specs=[row(d), row(d // 2), row(LANES), const(8, LANES)],
        out_shape=[jax.ShapeDtypeStruct((t, d), F32), jax.ShapeDtypeStruct((t, d // 2), jnp.uint32),
                   jax.ShapeDtypeStruct((t, LANES), F32), jax.ShapeDtypeStruct((8, LANES), F32)],
        scratch_shapes=[pltpu.VMEM((8, LANES), F32)],
        compiler_params=_params("arbitrary"),
    )(oa, orr, om, x2d, w_out.astype(BF16), g_ffn.reshape(1, d), wr, br)


MXU_COLS = 256


def _moe_kernel(first_ref, nblk_ref, cnt_ref, xs_hbm, w1_ref, b1g_ref, b1l_ref, w2_ref, b2_ref,
                ys_hbm, w1p_ref, w2b_ref, act_ref, xbuf, ybuf, xsem, ysem):
    e = pl.program_id(0)
    rb = MOE_BLOCK
    half = MXU_COLS // 2
    n_chunks = w1_ref.shape[2] // MXU_COLS
    g0 = first_ref[e]
    total = first_ref[N_EXPERTS]

    def x_copy(g, slot):
        return pltpu.make_async_copy(xs_hbm.at[pl.ds(pl.multiple_of(g * rb, rb), rb)],
                                     xbuf.at[slot], xsem.at[slot])

    def y_copy(g, slot):
        return pltpu.make_async_copy(ybuf.at[slot],
                                     ys_hbm.at[pl.ds(pl.multiple_of(g * rb, rb), rb)],
                                     ysem.at[slot])

    @pl.when(e == 0)
    def _():
        x_copy(0, 0).start()

    @pl.when(nblk_ref[e] > 0)
    def _():
        r = lax.broadcasted_iota(jnp.int32, (MXU_COLS, MXU_COLS), 0)
        c = lax.broadcasted_iota(jnp.int32, (MXU_COLS, MXU_COLS), 1)
        src = jnp.where(c < half, 2 * c, 2 * (c - half) + 1)
        perm = jnp.where(r == src, 1.0, 0.0).astype(BF16)
        for ch in range(n_chunks):
            cols = slice(ch * MXU_COLS, (ch + 1) * MXU_COLS)
            w = w1_ref[0, :, cols].astype(BF16)
            w1p_ref[:, cols] = jnp.dot(w, perm, preferred_element_type=F32).astype(BF16)
        w2b_ref[...] = w2_ref[0].astype(BF16)

    def block(b, carry):
        g = g0 + b
        slot = g % 2
        x_copy(g, slot).wait()

        @pl.when(g + 1 < total)
        def _():
            x_copy(g + 1, 1 - slot).start()

        row = lax.broadcasted_iota(jnp.int32, (rb, xbuf.shape[2]), 0)
        packed = jnp.where(row < cnt_ref[e] - b * rb, xbuf[slot], jnp.uint32(0))
        x_lo, x_hi = _unpack_bf16_pairs(packed)
        x = jnp.concatenate([x_lo.astype(BF16), x_hi.astype(BF16)], axis=1)
        for ch in range(n_chunks):
            z = jnp.dot(x, w1p_ref[:, ch * MXU_COLS:(ch + 1) * MXU_COLS], preferred_element_type=F32)
            hs = slice(ch * half, (ch + 1) * half)
            xg = jnp.minimum(z[:, :half] + b1g_ref[0][:, hs], SWIGLU_LIMIT)
            xl = jnp.clip(z[:, half:] + b1l_ref[0][:, hs], -SWIGLU_LIMIT, SWIGLU_LIMIT)
            act_ref[:, hs] = (xg * jax.nn.sigmoid(SWIGLU_ALPHA * xg) * (xl + 1.0)).astype(BF16)
        y = jnp.dot(act_ref[...], w2b_ref[...], preferred_element_type=F32) + b2_ref[0]

        @pl.when(g >= 2)
        def _():
            y_copy(g - 2, slot).wait()

        ybuf[slot] = _pack_bf16_pairs(y)
        y_copy(g, slot).start()
        return carry

    lax.fori_loop(0, nblk_ref[e], block, 0)

    @pl.when(e == pl.num_programs(0) - 1)
    def _():
        @pl.when(total >= 2)
        def _():
            y_copy(total - 2, total % 2).wait()

        y_copy(total - 1, (total - 1) % 2).wait()


def _moe_experts(xs, first_blk, n_blk, cnt, w1, b1g, b1l, w2, b2):
    p_rows = xs.shape[0]
    n_exp, d = w1.shape[0], w1.shape[1]
    d_ff = w2.shape[1]
    assert w1.shape[2] == 2 * d_ff and (2 * d_ff) % MXU_COLS == 0 and xs.shape[1] == d // 2
    wsel = lambda e, *_: (e, 0, 0)
    grid_spec = pltpu.PrefetchScalarGridSpec(
        num_scalar_prefetch=3,
        grid=(n_exp,),
        in_specs=[pl.BlockSpec(memory_space=pl.ANY),
                  pl.BlockSpec((1, d, 2 * d_ff), wsel),
                  pl.BlockSpec((1, 1, d_ff), wsel), pl.BlockSpec((1, 1, d_ff), wsel),
                  pl.BlockSpec((1, d_ff, d), wsel), pl.BlockSpec((1, 1, d), wsel)],
        out_specs=pl.BlockSpec(memory_space=pl.ANY),
        scratch_shapes=[pltpu.VMEM((d, 2 * d_ff), BF16), pltpu.VMEM((d_ff, d), BF16),
                        pltpu.VMEM((MOE_BLOCK, d_ff), BF16),
                        pltpu.VMEM((2, MOE_BLOCK, d // 2), jnp.uint32),
                        pltpu.VMEM((2, MOE_BLOCK, d // 2), jnp.uint32),
                        pltpu.SemaphoreType.DMA((2,)), pltpu.SemaphoreType.DMA((2,))],
    )
    return pl.pallas_call(
        _moe_kernel,
        grid_spec=grid_spec,
        out_shape=jax.ShapeDtypeStruct((p_rows, d // 2), jnp.uint32),
        compiler_params=pltpu.CompilerParams(dimension_semantics=("arbitrary",),
                                             vmem_limit_bytes=MOE_VMEM_LIMIT),
    )(first_blk, n_blk, cnt, xs, w1, b1g, b1l, w2, b2)


def _pack_bf16_pairs(y):
    n = y.shape[1] // 2
    lo = lax.bitcast_convert_type(y[:, :n].astype(BF16).astype(F32), jnp.uint32)
    hi = lax.bitcast_convert_type(y[:, n:].astype(BF16).astype(F32), jnp.uint32)
    return (lo >> 16) | hi


def _unpack_bf16_pairs(w):
    lo = lax.bitcast_convert_type(w << 16, F32)
    hi = lax.bitcast_convert_type(w & jnp.uint32(0xFFFF0000), F32)
    return lo, hi


def _sc_gather_rows(src, idx):
    n, w = src.shape
    m = idx.shape[0]
    workers = SC_CORES * SC_SUBCORES
    chunk = SC_GATHER_ROWS
    assert m % (workers * chunk) == 0
    per_worker = m // workers
    mesh = plsc.VectorSubcoreMesh(core_axis_name="c", subcore_axis_name="s",
                                  num_cores=SC_CORES, num_subcores=SC_SUBCORES)

    @functools.partial(
        pl.kernel, mesh=mesh, out_type=jax.ShapeDtypeStruct((m, w), src.dtype),
        scratch_types=[pltpu.VMEM((chunk,), jnp.int32), pltpu.VMEM((chunk, w), src.dtype),
                       pltpu.SemaphoreType.DMA])
    def gather(src_hbm, idx_hbm, out_hbm, idx_v, rows_v, sem):
        base = (lax.axis_index("s") * SC_CORES + lax.axis_index("c")) * per_worker

        @pl.loop(0, per_worker // chunk)
        def _(c):
            off = base + c * chunk
            pltpu.sync_copy(idx_hbm.at[pl.ds(off, chunk)], idx_v)
            pltpu.async_copy(src_hbm.at[idx_v], rows_v, sem).wait()
            pltpu.sync_copy(rows_v, out_hbm.at[pl.ds(off, chunk)])

    return gather(src, idx)


def _sc_scatter_rows(src, idx_t, n_out):
    t, w = src.shape
    fan = idx_t.shape[0]
    workers = SC_CORES * SC_SUBCORES
    chunk = SC_GATHER_ROWS
    assert t % (workers * chunk) == 0 and idx_t.shape[1] == t
    per_worker = t // workers
    mesh = plsc.VectorSubcoreMesh(core_axis_name="c", subcore_axis_name="s",
                                  num_cores=SC_CORES, num_subcores=SC_SUBCORES)

    @functools.partial(
        pl.kernel, mesh=mesh, out_type=jax.ShapeDtypeStruct((n_out, w), src.dtype),
        scratch_types=[pltpu.VMEM((fan, chunk), jnp.int32), pltpu.VMEM((chunk, w), src.dtype)])
    def scatter(src_hbm, idx_hbm, out_hbm, idx_v, rows_v):
        base = (lax.axis_index("s") * SC_CORES + lax.axis_index("c")) * per_worker

        @pl.loop(0, per_worker // chunk)
        def _(c):
            off = base + c * chunk
            pltpu.sync_copy(idx_hbm.at[:, pl.ds(off, chunk)], idx_v)
            pltpu.sync_copy(src_hbm.at[pl.ds(off, chunk)], rows_v)
            for j in range(fan):
                pltpu.sync_copy(rows_v, out_hbm.at[idx_v.at[j]])

    return scatter(src, idx_t)


def _combine_kernel(x2_ref, aux_ref, yg_ref, o_ref):
    x2 = x2_ref[...]
    aux = aux_ref[...]
    half = x2.shape[1] // 2
    out_lo, out_hi = x2[:, :half], x2[:, half:]
    for k in range(TOP_K):
        lo, hi = _unpack_bf16_pairs(yg_ref[k])
        wk = aux[:, TOP_K + k:TOP_K + k + 1]
        out_lo = out_lo + wk * lo
        out_hi = out_hi + wk * hi
    o_ref[:, :half] = out_lo
    o_ref[:, half:] = out_hi


def _combine(x2, aux, yg, tm):
    t, d = x2.shape
    return pl.pallas_call(
        _combine_kernel,
        grid=(t // tm,),
        in_specs=[pl.BlockSpec((tm, d), lambda i: (i, 0)),
                  pl.BlockSpec((tm, LANES), lambda i: (i, 0)),
                  pl.BlockSpec((TOP_K, tm, d // 2), lambda i: (0, i, 0))],
        out_specs=pl.BlockSpec((tm, d), lambda i: (i, 0)),
        out_shape=jax.ShapeDtypeStruct((t, d), F32),
        compiler_params=_params("parallel"),
    )(x2, aux, yg)


def _layer(x, mem, g_attn, w_in, moba_q_gain, moba_k_gain, ret_out_gain, g_mem, w_mem_kv,
           mem_q_gain, mem_k_gain, w_out, g_ffn, w_router, b_router, w1, b1, w2, b2):
    batch, seq, d = x.shape
    t = batch * seq
    x2d = x.reshape(t, d)
    tm = min(512, t)

    proj = _rms_proj(x2d, g_attn, w_in.astype(BF16), tm)
    oa = _moba(proj, moba_q_gain, moba_k_gain, batch, seq)
    orr = _retention(proj, ret_out_gain, batch, seq)
    km, vm = _mem_kv(mem, g_mem, w_mem_kv, mem_k_gain)
    om = _mem_attn(proj, km, vm, mem_q_gain, batch, seq, min(512, seq))
    x2, h2, aux, counts = _out_router(oa, orr, om, x2d, w_out, g_ffn, w_router, b_router, tm)

    eidx = aux[:, 0:TOP_K].astype(jnp.int32)
    rank = aux[:, 2 * TOP_K:3 * TOP_K].astype(jnp.int32)
    cnt = counts[0, :N_EXPERTS].astype(jnp.int32)
    n_blk = (cnt + MOE_BLOCK - 1) // MOE_BLOCK
    blk_ends = jnp.cumsum(n_blk)
    first_blk = jnp.concatenate([jnp.zeros((1,), jnp.int32), blk_ends]).astype(jnp.int32)
    pstarts = first_blk[:N_EXPERTS] * MOE_BLOCK
    dest = pstarts[eidx] + rank
    a = t * TOP_K
    p_rows = -(-(a + N_EXPERTS * (MOE_BLOCK - 1)) // MOE_BLOCK) * MOE_BLOCK
    dest_t = dest.T
    xs = _sc_scatter_rows(h2, dest_t, p_rows)

    d_ff = w2.shape[1]
    b1g = b1[:, 0::2].reshape(N_EXPERTS, 1, d_ff)
    b1l = b1[:, 1::2].reshape(N_EXPERTS, 1, d_ff)
    ys = _moe_experts(xs, first_blk, n_blk.astype(jnp.int32), cnt, w1, b1g, b1l, w2,
                      b2.reshape(N_EXPERTS, 1, d))
    yg = _sc_gather_rows(ys, dest_t.reshape(a)).reshape(TOP_K, t, d // 2)
    out = _combine(x2, aux, yg, min(256, t))
    return out.reshape(batch, seq, d)


def kernel(x, mem, g_attn, w_in, moba_q_gain, moba_k_gain, ret_out_gain, g_mem, w_mem_kv,
           mem_q_gain, mem_k_gain, w_out, g_ffn, w_router, b_router, w1, b1, w2, b2):
    for l in range(g_attn.shape[0]):
        x = _layer(x, mem, g_attn[l], w_in[l], moba_q_gain[l], moba_k_gain[l], ret_out_gain[l],
                   g_mem[l], w_mem_kv[l], mem_q_gain[l], mem_k_gain[l], w_out[l], g_ffn[l],
                   w_router[l], b_router[l], w1[l], b1[l], w2[l], b2[l])
    return x
```

```python
import functools

import jax
import jax.numpy as jnp
from jax import lax
from jax.experimental import pallas as pl
from jax.experimental.pallas import tpu as pltpu
from jax.experimental.pallas import tpu_sc as plsc

F32 = jnp.float32
BF16 = jnp.bfloat16

LANES = 128
HEAD_DIM = 64
MOBA_HEADS = 8
RET_HEADS = 4
MEM_HEADS = 4
MOBA_WIDTH = MOBA_HEADS * HEAD_DIM
RET_WIDTH = RET_HEADS * HEAD_DIM
MEM_WIDTH = MEM_HEADS * HEAD_DIM
IN_WIDTH = 3 * MOBA_WIDTH + 4 * RET_WIDTH + MEM_WIDTH
MOBA_BLOCK = 256
MOBA_TOPK = 3
RET_CHUNK = 128
RET_ROPE_BASE = 10000.0
N_EXPERTS = 32
TOP_K = 4
SWIGLU_ALPHA = 1.702
SWIGLU_LIMIT = 7.0
MOE_BLOCK = 512
RET_UNROLL = 8
PREP_UNROLL = 4
EPS = 1e-6
NEG = -1e30
QK_SCALE = HEAD_DIM ** -0.5

VMEM_LIMIT = 48 * 1024 * 1024
MOE_VMEM_LIMIT = 56 * 1024 * 1024
MIX_DTYPE = BF16
SC_CORES = 2
SC_SUBCORES = 16
SC_GATHER_ROWS = 128

_NT = (((1,), (1,)), ((), ()))
_TN = (((0,), (0,)), ((), ()))


def _params(*sem):
    return pltpu.CompilerParams(dimension_semantics=sem, vmem_limit_bytes=VMEM_LIMIT)


def _lane_iota(shape):
    return lax.broadcasted_iota(jnp.int32, shape, len(shape) - 1)


def _pair_rms(t, gain, lo):
    t2 = t * t
    s0 = jnp.sum(jnp.where(lo, t2, 0.0), axis=-1, keepdims=True)
    s1 = jnp.sum(jnp.where(lo, 0.0, t2), axis=-1, keepdims=True)
    r = jnp.where(lo, lax.rsqrt(s0 / HEAD_DIM + EPS), lax.rsqrt(s1 / HEAD_DIM + EPS))
    return t * r * gain


def _rms_proj_kernel(x_ref, g_ref, w_ref, o_ref):
    x = x_ref[...]
    ms = jnp.mean(x * x, axis=-1, keepdims=True)
    h = (x * lax.rsqrt(ms + EPS) * g_ref[...]).astype(BF16)
    o_ref[...] = jnp.dot(h, w_ref[...], preferred_element_type=F32)


def _rms_proj(x2d, gain, w, tm):
    t, d = x2d.shape
    n = w.shape[1]
    return pl.pallas_call(
        _rms_proj_kernel,
        grid=(t // tm,),
        in_specs=[pl.BlockSpec((tm, d), lambda i: (i, 0)),
                  pl.BlockSpec((1, d), lambda i: (0, 0)),
                  pl.BlockSpec((d, n), lambda i: (0, 0))],
        out_specs=pl.BlockSpec((tm, n), lambda i: (i, 0)),
        out_shape=jax.ShapeDtypeStruct((t, n), F32),
        compiler_params=_params("parallel"),
    )(x2d, gain.reshape(1, d), w)


SHIFT_SAFE = 80.0
MOBA_GROUP = 4


def _moba_kernel(tab_ref, q_ref, k_ref, v_ref, qg_ref, kg_ref, m_ref, o_ref,
                 ka_ref, va_ref, qa_ref, gt_ref, acc_ref, pa_ref, pb_ref, *, n_blk, n_trips):
    blk = MOBA_BLOCK
    grp = MOBA_GROUP
    nb8 = -(-n_blk // 8) * 8
    lane = _lane_iota((blk, LANES))
    lo = lane < HEAD_DIM
    lane1 = _lane_iota((1, LANES))
    qg = qg_ref[...]
    kg = kg_ref[...]
    tri = (lax.broadcasted_iota(jnp.int32, (blk, blk), 0)
           >= lax.broadcasted_iota(jnp.int32, (blk, blk), 1))
    row_t = lax.broadcasted_iota(jnp.int32, (nb8, blk), 0)
    row_tf = row_t.astype(F32)
    fill_row = lax.broadcasted_iota(jnp.int32, (HEAD_DIM - nb8, blk), 0)
    filler = jnp.where(fill_row == HEAD_DIM - nb8 - 1, -m_ref[...], 0.0)

    def rows_of(j):
        return pl.ds(pl.multiple_of(j * blk, blk), blk)

    gt_ref[...] = jnp.zeros_like(gt_ref)

    def prep_k(j, carry):
        rows = rows_of(j)
        kn = _pair_rms(k_ref[rows, :], kg, lo)
        km = jnp.mean(kn, axis=0, keepdims=True)
        gt_ref[pl.ds(HEAD_DIM + j, 1), :] = jnp.where(lane1 < HEAD_DIM, km, 0.0)
        gt_ref[pl.ds(j, 1), :] = jnp.where(lane1 < HEAD_DIM, 0.0, km)
        tag0 = jnp.where(jnp.logical_or(lane == HEAD_DIM + j, lane == LANES - 1), 1.0, 0.0)
        tag1 = jnp.where(jnp.logical_or(lane == j, lane == HEAD_DIM - 1), 1.0, 0.0)
        ka_ref[0, rows, :] = jnp.where(lo, kn, tag0).astype(BF16)
        ka_ref[1, rows, :] = jnp.where(lo, tag1, kn).astype(BF16)
        v = v_ref[rows, :]
        va_ref[0, rows, :] = jnp.where(lo, v, jnp.where(lane == HEAD_DIM, 1.0, 0.0)).astype(BF16)
        va_ref[1, rows, :] = jnp.where(lo, jnp.where(lane == 0, 1.0, 0.0), v).astype(BF16)
        return carry

    lax.fori_loop(0, n_blk, prep_k, 0, unroll=PREP_UNROLL)

    def block_bias(g, n):
        valid = row_t < n
        g = jnp.where(valid, g, NEG)
        sel = jnp.zeros((nb8, blk), jnp.bool_)
        for _ in range(MOBA_TOPK):
            m = jnp.max(g, axis=0, keepdims=True)
            first = jnp.min(jnp.where(g == m, row_tf, jnp.inf), axis=0, keepdims=True)
            pick = row_tf == first
            sel = jnp.logical_or(sel, pick)
            g = jnp.where(pick, -jnp.inf, g)
        keep = jnp.logical_or(jnp.logical_and(sel, valid), row_t == n)
        return jnp.where(keep, 0.0, NEG)

    def prep_q(n, carry):
        rows_n = rows_of(n)
        qn = _pair_rms(q_ref[rows_n, :], qg, lo)
        gate_t = lax.dot_general(gt_ref[...].astype(BF16), qn.astype(BF16), _NT,
                                 preferred_element_type=F32)
        bias_t = jnp.concatenate([block_bias(gate_t[0:nb8], n), filler,
                                  block_bias(gate_t[HEAD_DIM:HEAD_DIM + nb8], n), filler], axis=0)
        bias = bias_t.T
        qs = qn * QK_SCALE
        qa_ref[0, rows_n, :] = jnp.where(lo, qs, bias).astype(BF16)
        qa_ref[1, rows_n, :] = jnp.where(lo, bias, qs).astype(BF16)
        return carry

    lax.fori_loop(0, n_blk, prep_q, 0, unroll=PREP_UNROLL)

    def scores(h, n, j):
        s = lax.dot_general(qa_ref[h, rows_of(n), :], ka_ref[h, rows_of(j), :], _NT,
                            preferred_element_type=F32)
        return jnp.where(jnp.logical_or(tri, j != n), s, NEG)

    def normalise(n, a0, a1):
        o_ref[rows_of(n), :] = jnp.where(lo, a0 / a0[:, HEAD_DIM:HEAD_DIM + 1],
                                         a1 / a1[:, 0:1]).astype(o_ref.dtype)

    @pl.when(tab_ref[0, 0] == 1)
    def _():
        acc_ref[...] = jnp.zeros_like(acc_ref)

        def make_probs(trip, dst_ref):
            for g in range(grp):
                n = tab_ref[1, trip * grp + g]
                j = tab_ref[2, trip * grp + g]
                for h in range(2):
                    dst_ref[g, h] = jnp.exp(scores(h, n, j)).astype(BF16)

        def apply_probs(trip, src_ref):
            for g in range(grp):
                n = tab_ref[1, trip * grp + g]
                j = tab_ref[2, trip * grp + g]
                for h in range(2):
                    acc_ref[h, rows_of(n), :] += jnp.dot(src_ref[g, h], va_ref[h, rows_of(j), :],
                                                         preferred_element_type=F32)

        make_probs(0, pa_ref)

        def two_trips(i, carry):
            apply_probs(2 * i, pa_ref)
            make_probs(2 * i + 1, pb_ref)
            apply_probs(2 * i + 1, pb_ref)
            make_probs(2 * i + 2, pa_ref)
            return carry

        lax.fori_loop(0, n_trips // 2, two_trips, 0)

        def fin(n, carry):
            normalise(n, acc_ref[0, rows_of(n), :], acc_ref[1, rows_of(n), :])
            return carry

        lax.fori_loop(0, n_blk, fin, 0, unroll=PREP_UNROLL)

    @pl.when(tab_ref[0, 0] != 1)
    def _():
        def qblock(n, carry):
            def one(j, st):
                out = []
                for h in range(2):
                    m, acc = st[2 * h], st[2 * h + 1]
                    s = scores(h, n, j)
                    m_new = jnp.maximum(m, jnp.max(s, axis=-1, keepdims=True))
                    p = jnp.exp(s - m_new).astype(BF16)
                    acc = jnp.exp(m - m_new) * acc + jnp.dot(p, va_ref[h, rows_of(j), :],
                                                             preferred_element_type=F32)
                    out += [m_new, acc]
                return tuple(out)

            zero = jnp.zeros((blk, LANES), F32)
            ninf = jnp.full((blk, 1), -jnp.inf, F32)
            st = lax.fori_loop(0, n + 1, one, (ninf, zero, ninf, zero))
            normalise(n, st[1], st[3])
            return carry

        lax.fori_loop(0, n_blk, qblock, 0)


def _moba_tiles(n_blk):
    tiles = [(n, j) for n in range(n_blk) for j in range(n + 1)]
    per_two = 2 * MOBA_GROUP
    n_trips = 2 * (-(-len(tiles) // per_two))
    tiles += [(0, 1)] * ((n_trips + 1) * MOBA_GROUP - len(tiles))
    return n_trips, tiles


def _moba(proj, q_gain, k_gain, batch, seq):
    n_blk = seq // MOBA_BLOCK
    assert seq % MOBA_BLOCK == 0 and MOBA_TOPK <= n_blk <= HEAD_DIM - 8
    n_pairs = MOBA_WIDTH // LANES
    qg = jnp.tile(q_gain, 2).reshape(1, LANES)
    kg = jnp.tile(k_gain, 2).reshape(1, LANES)
    shift = HEAD_DIM * QK_SCALE * jnp.max(jnp.abs(q_gain)) * jnp.max(jnp.abs(k_gain))
    flag = (2.0 * shift <= SHIFT_SAFE).astype(jnp.int32)
    n_trips, tiles = _moba_tiles(n_blk)
    tab = jnp.stack([jnp.full((len(tiles),), flag, jnp.int32),
                     jnp.asarray([t[0] for t in tiles], jnp.int32),
                     jnp.asarray([t[1] for t in tiles], jnp.int32)])
    mrow = jnp.full((1, MOBA_BLOCK), shift, F32)
    blk = lambda off: pl.BlockSpec((seq, LANES), lambda b, p, f: (b, off + p))
    vec = lambda w: pl.BlockSpec((1, w), lambda b, p, f: (0, 0))
    prob = pltpu.VMEM((MOBA_GROUP, 2, MOBA_BLOCK, MOBA_BLOCK), BF16)
    grid_spec = pltpu.PrefetchScalarGridSpec(
        num_scalar_prefetch=1,
        grid=(batch, n_pairs),
        in_specs=[blk(0), blk(n_pairs), blk(2 * n_pairs), vec(LANES), vec(LANES), vec(MOBA_BLOCK)],
        out_specs=pl.BlockSpec((seq, LANES), lambda b, p, f: (b, p)),
        scratch_shapes=[pltpu.VMEM((2, seq, LANES), BF16),
                        pltpu.VMEM((2, seq, LANES), BF16),
                        pltpu.VMEM((2, seq, LANES), BF16),
                        pltpu.VMEM((LANES, LANES), F32),
                        pltpu.VMEM((2, seq, LANES), F32),
                        prob, prob],
    )
    return pl.pallas_call(
        functools.partial(_moba_kernel, n_blk=n_blk, n_trips=n_trips),
        grid_spec=grid_spec,
        out_shape=jax.ShapeDtypeStruct((batch * seq, MOBA_WIDTH), MIX_DTYPE),
        compiler_params=_params("parallel", "parallel"),
    )(tab, proj, proj, proj, qg, kg, mrow)


def _ret_kernel(q_ref, k_ref, v_ref, g_ref, cos_ref, sin_ref, intra_ref, kdec_ref, qdec_ref,
                cdm_ref, mask_ref, gain_ref, o_ref, *, n_ch):
    c = RET_CHUNK
    lane = _lane_iota((c, LANES))
    lo = lane < HEAD_DIM
    first_half = (lane % HEAD_DIM) < (HEAD_DIM // 2)
    gain = gain_ref[...]
    intra0 = intra_ref[0, 0]
    intra1 = intra_ref[0, 1]
    kdec = kdec_ref[0]
    qdec = qdec_ref[0]
    cdm = cdm_ref[0]
    mask = mask_ref[...]

    def rope(t, cos_t, sin_t):
        partner = jnp.where(first_half, pltpu.roll(t, LANES - HEAD_DIM // 2, 1),
                            pltpu.roll(t, HEAD_DIM // 2, 1))
        return t * cos_t + partner * sin_t

    def chunk(i, state):
        rows = pl.ds(pl.multiple_of(i * c, c), c)
        cos_t = cos_ref[rows, :]
        sin_t = sin_ref[rows, :]
        q = rope(q_ref[rows, :], cos_t, sin_t)
        k = rope(k_ref[rows, :], cos_t, sin_t) * QK_SCALE
        v = v_ref[rows, :]
        kb = k.astype(BF16)
        s0 = lax.dot_general(jnp.where(lo, q, 0.0).astype(BF16), kb, _NT,
                             preferred_element_type=F32) * intra0
        s1 = lax.dot_general(jnp.where(lo, 0.0, q).astype(BF16), kb, _NT,
                             preferred_element_type=F32) * intra1
        sc = jnp.concatenate([s0, s1], axis=1).astype(BF16)
        vv = jnp.concatenate([jnp.where(lo, v, 0.0), jnp.where(lo, 0.0, v)], axis=0).astype(BF16)
        o = jnp.dot(sc, vv, preferred_element_type=F32)
        o = o + jnp.dot(q.astype(BF16), state.astype(BF16), preferred_element_type=F32) * qdec
        kd = (k * kdec).astype(BF16)
        kv = lax.dot_general(kd, v.astype(BF16), _TN, preferred_element_type=F32)
        state = cdm * state + mask * kv
        o2 = o * o
        m0 = jnp.sum(jnp.where(lo, o2, 0.0), axis=-1, keepdims=True)
        m1 = jnp.sum(jnp.where(lo, 0.0, o2), axis=-1, keepdims=True)
        r = jnp.where(lo, lax.rsqrt(m0 / HEAD_DIM + EPS), lax.rsqrt(m1 / HEAD_DIM + EPS))
        gt = g_ref[rows, :]
        o_ref[rows, :] = ((gt * jax.nn.sigmoid(gt)) * (o * r * gain)).astype(o_ref.dtype)
        return state

    lax.fori_loop(0, n_ch, chunk, jnp.zeros((LANES, LANES), F32), unroll=RET_UNROLL)


def _ret_tables(seq):
    c = RET_CHUNK
    half = HEAD_DIM // 2
    inv_freq = 1.0 / (RET_ROPE_BASE ** jnp.linspace(0.0, 1.0, half, dtype=F32))
    ang = jnp.arange(seq, dtype=F32)[:, None] * inv_freq[None, :]
    cos, sin = jnp.cos(ang), jnp.sin(ang)
    cos_t = jnp.tile(cos, (1, LANES // half))
    sin_t = jnp.tile(jnp.concatenate([-sin, sin], axis=1), (1, LANES // HEAD_DIM))
    log_gamma = jnp.log1p(-jnp.exp2(-5.0 - jnp.arange(RET_HEADS, dtype=F32)))
    idx = jnp.arange(c, dtype=F32)
    diff = idx[:, None] - idx[None, :]
    intra = jnp.where(diff >= 0, jnp.exp(log_gamma[:, None, None] * jnp.maximum(diff, 0.0)), 0.0)
    k_decay = jnp.exp(log_gamma[:, None] * (c - 1.0 - idx)[None, :])
    q_decay = jnp.exp(log_gamma[:, None] * (idx + 1.0)[None, :])
    chunk_decay = jnp.exp(log_gamma * c)
    n_pairs = RET_HEADS // 2

    def lanes(t):
        return jnp.repeat(t.reshape(n_pairs, 2, c).transpose(0, 2, 1), HEAD_DIM, axis=2)

    blockdiag = (jnp.arange(LANES)[:, None] // HEAD_DIM) == (jnp.arange(LANES)[None, :] // HEAD_DIM)
    mask = blockdiag.astype(F32)
    cdm = jnp.repeat(chunk_decay.reshape(n_pairs, 2), HEAD_DIM, axis=1)[:, :, None] * mask[None]
    return (cos_t, sin_t, intra.reshape(n_pairs, 2, c, c), lanes(k_decay), lanes(q_decay), cdm, mask)


def _retention(proj, out_gain, batch, seq):
    c = RET_CHUNK
    assert seq % c == 0
    n_pairs = RET_HEADS // 2
    base = 3 * MOBA_WIDTH // LANES
    cos_t, sin_t, intra, kdec, qdec, cdm, mask = _ret_tables(seq)
    blk = lambda off: pl.BlockSpec((seq, LANES), lambda b, p: (b, base + off + p))
    tab = pl.BlockSpec((seq, LANES), lambda b, p: (0, 0))
    return pl.pallas_call(
        functools.partial(_ret_kernel, n_ch=seq // c),
        grid=(batch, n_pairs),
        in_specs=[blk(0), blk(n_pairs), blk(2 * n_pairs), blk(3 * n_pairs), tab, tab,
                  pl.BlockSpec((1, 2, c, c), lambda b, p: (p, 0, 0, 0)),
                  pl.BlockSpec((1, c, LANES), lambda b, p: (p, 0, 0)),
                  pl.BlockSpec((1, c, LANES), lambda b, p: (p, 0, 0)),
                  pl.BlockSpec((1, LANES, LANES), lambda b, p: (p, 0, 0)),
                  pl.BlockSpec((LANES, LANES), lambda b, p: (0, 0)),
                  pl.BlockSpec((1, LANES), lambda b, p: (0, p))],
        out_specs=pl.BlockSpec((seq, LANES), lambda b, p: (b, p)),
        out_shape=jax.ShapeDtypeStruct((batch * seq, RET_WIDTH), MIX_DTYPE),
        compiler_params=_params("parallel", "parallel"),
    )(proj, proj, proj, proj, cos_t, sin_t, intra, kdec, qdec, cdm, mask,
      out_gain.reshape(1, RET_WIDTH))


def _mem_kv_kernel(m_ref, g_ref, w_ref, kg_ref, k_ref, v_ref):
    x = m_ref[0]
    ms = jnp.mean(x * x, axis=-1, keepdims=True)
    h = (x * lax.rsqrt(ms + EPS) * g_ref[...]).astype(BF16)
    kv = jnp.dot(h, w_ref[...], preferred_element_type=F32)
    lane = _lane_iota((x.shape[0], LANES))
    lo = lane < HEAD_DIM
    for p in range(MEM_WIDTH // LANES):
        kt = kv[:, p * LANES:(p + 1) * LANES]
        k_ref[0, :, p * LANES:(p + 1) * LANES] = _pair_rms(kt, kg_ref[...], lo).astype(BF16)
    v_ref[0] = kv[:, MEM_WIDTH:].astype(BF16)


def _mem_kv(mem, g_mem, w_mem_kv, k_gain):
    b, m, d = mem.shape
    return pl.pallas_call(
        _mem_kv_kernel,
        grid=(b,),
        in_specs=[pl.BlockSpec((1, m, d), lambda i: (i, 0, 0)),
                  pl.BlockSpec((1, d), lambda i: (0, 0)),
                  pl.BlockSpec((d, 2 * MEM_WIDTH), lambda i: (0, 0)),
                  pl.BlockSpec((1, LANES), lambda i: (0, 0))],
        out_specs=[pl.BlockSpec((1, m, MEM_WIDTH), lambda i: (i, 0, 0)),
                   pl.BlockSpec((1, m, MEM_WIDTH), lambda i: (i, 0, 0))],
        out_shape=[jax.ShapeDtypeStruct((b, m, MEM_WIDTH), BF16),
                   jax.ShapeDtypeStruct((b, m, MEM_WIDTH), BF16)],
        compiler_params=_params("parallel"),
    )(mem, g_mem.reshape(1, d), w_mem_kv.astype(BF16), jnp.tile(k_gain, 2).reshape(1, LANES))


def _mem_attn_kernel(q_ref, k_ref, v_ref, qg_ref, o_ref):
    tq = q_ref.shape[0]
    lane = _lane_iota((tq, LANES))
    lo = lane < HEAD_DIM
    for p in range(MEM_WIDTH // LANES):
        cols = slice(p * LANES, (p + 1) * LANES)
        qn = _pair_rms(q_ref[:, cols], qg_ref[...], lo) * QK_SCALE
        kt = k_ref[0, :, cols]
        vt = v_ref[0, :, cols]
        outs = []
        for h in range(2):
            qh = jnp.where(lo, qn, 0.0) if h == 0 else jnp.where(lo, 0.0, qn)
            s = lax.dot_general(qh.astype(BF16), kt, _NT, preferred_element_type=F32)
            m = jnp.max(s, axis=-1, keepdims=True)
            e = jnp.exp(s - m)
            pr = e / jnp.sum(e, axis=-1, keepdims=True)
            outs.append(jnp.dot(pr.astype(BF16), vt, preferred_element_type=F32))
        o_ref[:, cols] = jnp.where(lo, outs[0], outs[1]).astype(o_ref.dtype)


def _mem_attn(proj, km, vm, q_gain, batch, seq, tq):
    m = km.shape[1]
    qcol = (IN_WIDTH - MEM_WIDTH) // MEM_WIDTH
    assert qcol * MEM_WIDTH == IN_WIDTH - MEM_WIDTH
    nq = seq // tq
    return pl.pallas_call(
        _mem_attn_kernel,
        grid=(batch, nq),
        in_specs=[pl.BlockSpec((tq, MEM_WIDTH), lambda b, i: (b * nq + i, qcol)),
                  pl.BlockSpec((1, m, MEM_WIDTH), lambda b, i: (b, 0, 0)),
                  pl.BlockSpec((1, m, MEM_WIDTH), lambda b, i: (b, 0, 0)),
                  pl.BlockSpec((1, LANES), lambda b, i: (0, 0))],
        out_specs=pl.BlockSpec((tq, MEM_WIDTH), lambda b, i: (b * nq + i, 0)),
        out_shape=jax.ShapeDtypeStruct((batch * seq, MEM_WIDTH), MIX_DTYPE),
        compiler_params=_params("parallel", "parallel"),
    )(proj, km, vm, jnp.tile(q_gain, 2).reshape(1, LANES))


def _out_router_kernel(oa_ref, or_ref, om_ref, x_ref, wo_ref, g_ref, wr_ref, br_ref,
                       x2_ref, h2_ref, aux_ref, cnt_ref, run_ref):
    tm = x_ref.shape[0]

    @pl.when(pl.program_id(0) == 0)
    def _():
        run_ref[...] = jnp.zeros_like(run_ref)

    y = jnp.dot(oa_ref[...].astype(BF16), wo_ref[0:MOBA_WIDTH, :], preferred_element_type=F32)
    y += jnp.dot(or_ref[...].astype(BF16), wo_ref[MOBA_WIDTH:MOBA_WIDTH + RET_WIDTH, :],
                 preferred_element_type=F32)
    y += jnp.dot(om_ref[...].astype(BF16), wo_ref[MOBA_WIDTH + RET_WIDTH:, :],
                 preferred_element_type=F32)
    x2 = x_ref[...] + y
    x2_ref[...] = x2
    ms = jnp.mean(x2 * x2, axis=-1, keepdims=True)
    h2 = x2 * lax.rsqrt(ms + EPS) * g_ref[...]
    h2_ref[...] = _pack_bf16_pairs(h2)
    logits_t = lax.dot_general(wr_ref[...], h2.astype(BF16), _NT,
                               preferred_element_type=F32)[0:N_EXPERTS] + br_ref[...]
    row_f = lax.broadcasted_iota(jnp.int32, (N_EXPERTS, tm), 0).astype(F32)
    lg = logits_t
    vals, firsts, picks = [], [], []
    for _ in range(TOP_K):
        m = jnp.max(lg, axis=0, keepdims=True)
        first = jnp.min(jnp.where(lg == m, row_f, jnp.inf), axis=0, keepdims=True)
        pick = row_f == first
        vals.append(m)
        firsts.append(first)
        picks.append(pick)
        lg = jnp.where(pick, -jnp.inf, lg)
    exps = [jnp.exp(v - vals[0]) for v in vals]
    denom = exps[0] + exps[1] + exps[2] + exps[3]
    sel = jnp.zeros((N_EXPERTS, tm), F32)
    for pick in picks:
        sel = sel + jnp.where(pick, 1.0, 0.0)
    before = jnp.where(lax.broadcasted_iota(jnp.int32, (tm, tm), 0)
                       < lax.broadcasted_iota(jnp.int32, (tm, tm), 1), 1.0, 0.0).astype(BF16)
    run = run_ref[:, 0:1]
    pos = run + jnp.dot(sel.astype(BF16), before, preferred_element_type=F32)
    ranks = [jnp.sum(jnp.where(p, pos, 0.0), axis=0, keepdims=True) for p in picks]
    slot = lax.broadcasted_iota(jnp.int32, (4 * TOP_K, tm), 0)
    aux_t = jnp.zeros((4 * TOP_K, tm), F32)
    for k in range(TOP_K):
        aux_t = aux_t + jnp.where(slot == k, firsts[k], 0.0)
        aux_t = aux_t + jnp.where(slot == TOP_K + k, exps[k] / denom, 0.0)
        aux_t = aux_t + jnp.where(slot == 2 * TOP_K + k, ranks[k], 0.0)
    aux_t = jnp.concatenate([aux_t, jnp.zeros((LANES - 4 * TOP_K, tm), F32)], axis=0)
    aux_ref[...] = aux_t.T
    run = run + jnp.sum(sel, axis=1, keepdims=True)
    run_ref[...] = jnp.broadcast_to(run, run_ref.shape)
    cnt_ref[...] = jnp.broadcast_to(run, cnt_ref.shape)


def _out_router(oa, orr, om, x2d, w_out, g_ffn, w_router, b_router, tm):
    t, d = x2d.shape
    wr_t = jnp.zeros((LANES, d), BF16).at[:N_EXPERTS, :].set(w_router.T.astype(BF16))
    br = b_router.reshape(N_EXPERTS, 1).astype(F32)
    row = lambda w: pl.BlockSpec((tm, w), lambda i: (i, 0))
    const = lambda r, c: pl.BlockSpec((r, c), lambda i: (0, 0))
    return pl.pallas_call(
        _out_router_kernel,
        grid=(t // tm,),
        in_specs=[row(MOBA_WIDTH), row(RET_WIDTH), row(MEM_WIDTH), row(d),
                  const(d, d), const(1, d), const(LANES, d), const(N_EXPERTS, 1)],
        out_specs=[row(d), row(d // 2), row(LANES), const(N_EXPERTS, LANES)],
        out_shape=[jax.ShapeDtypeStruct((t, d), F32), jax.ShapeDtypeStruct((t, d // 2), jnp.uint32),
                   jax.ShapeDtypeStruct((t, LANES), F32),
                   jax.ShapeDtypeStruct((N_EXPERTS, LANES), F32)],
        scratch_shapes=[pltpu.VMEM((N_EXPERTS, LANES), F32)],
        compiler_params=_params("arbitrary"),
    )(oa, orr, om, x2d, w_out.astype(BF16), g_ffn.reshape(1, d), wr_t, br)


MXU_COLS = 256


def _moe_kernel(first_ref, nblk_ref, cnt_ref, xs_hbm, w1_ref, b1g_ref, b1l_ref, w2_ref, b2_ref,
                ys_hbm, w1p_ref, w2b_ref, act_ref, xbuf, ybuf, xsem, ysem):
    e = pl.program_id(0)
    rb = MOE_BLOCK
    half = MXU_COLS // 2
    n_chunks = w1_ref.shape[2] // MXU_COLS
    g0 = first_ref[e]
    total = first_ref[N_EXPERTS]

    def x_copy(g, slot):
        return pltpu.make_async_copy(xs_hbm.at[pl.ds(pl.multiple_of(g * rb, rb), rb)],
                                     xbuf.at[slot], xsem.at[slot])

    def y_copy(g, slot):
        return pltpu.make_async_copy(ybuf.at[slot],
                                     ys_hbm.at[pl.ds(pl.multiple_of(g * rb, rb), rb)],
                                     ysem.at[slot])

    @pl.when(e == 0)
    def _():
        x_copy(0, 0).start()

    @pl.when(nblk_ref[e] > 0)
    def _():
        r = lax.broadcasted_iota(jnp.int32, (MXU_COLS, MXU_COLS), 0)
        c = lax.broadcasted_iota(jnp.int32, (MXU_COLS, MXU_COLS), 1)
        src = jnp.where(c < half, 2 * c, 2 * (c - half) + 1)
        perm = jnp.where(r == src, 1.0, 0.0).astype(BF16)
        for ch in range(n_chunks):
            cols = slice(ch * MXU_COLS, (ch + 1) * MXU_COLS)
            w = w1_ref[0, :, cols].astype(BF16)
            w1p_ref[:, cols] = jnp.dot(w, perm, preferred_element_type=F32).astype(BF16)
        w2b_ref[...] = w2_ref[0].astype(BF16)

    def block(b, carry):
        g = g0 + b
        slot = g % 2
        x_copy(g, slot).wait()

        @pl.when(g + 1 < total)
        def _():
            x_copy(g + 1, 1 - slot).start()

        row = lax.broadcasted_iota(jnp.int32, (rb, xbuf.shape[2]), 0)
        packed = jnp.where(row < cnt_ref[e] - b * rb, xbuf[slot], jnp.uint32(0))
        x_lo, x_hi = _unpack_bf16_pairs(packed)
        x = jnp.concatenate([x_lo.astype(BF16), x_hi.astype(BF16)], axis=1)
        for ch in range(n_chunks):
            z = jnp.dot(x, w1p_ref[:, ch * MXU_COLS:(ch + 1) * MXU_COLS], preferred_element_type=F32)
            hs = slice(ch * half, (ch + 1) * half)
            xg = jnp.minimum(z[:, :half] + b1g_ref[0][:, hs], SWIGLU_LIMIT)
            xl = jnp.clip(z[:, half:] + b1l_ref[0][:, hs], -SWIGLU_LIMIT, SWIGLU_LIMIT)
            act_ref[:, hs] = (xg * jax.nn.sigmoid(SWIGLU_ALPHA * xg) * (xl + 1.0)).astype(BF16)
        y = jnp.dot(act_ref[...], w2b_ref[...], preferred_element_type=F32) + b2_ref[0]

        @pl.when(g >= 2)
        def _():
            y_copy(g - 2, slot).wait()

        ybuf[slot] = _pack_bf16_pairs(y)
        y_copy(g, slot).start()
        return carry

    lax.fori_loop(0, nblk_ref[e], block, 0)

    @pl.when(e == pl.num_programs(0) - 1)
    def _():
        @pl.when(total >= 2)
        def _():
            y_copy(total - 2, total % 2).wait()

        y_copy(total - 1, (total - 1) % 2).wait()


def _moe_experts(xs, first_blk, n_blk, cnt, w1, b1g, b1l, w2, b2):
    p_rows = xs.shape[0]
    n_exp, d = w1.shape[0], w1.shape[1]
    d_ff = w2.shape[1]
    assert w1.shape[2] == 2 * d_ff and (2 * d_ff) % MXU_COLS == 0 and xs.shape[1] == d // 2
    wsel = lambda e, *_: (e, 0, 0)
    grid_spec = pltpu.PrefetchScalarGridSpec(
        num_scalar_prefetch=3,
        grid=(n_exp,),
        in_specs=[pl.BlockSpec(memory_space=pl.ANY),
                  pl.BlockSpec((1, d, 2 * d_ff), wsel),
                  pl.BlockSpec((1, 1, d_ff), wsel), pl.BlockSpec((1, 1, d_ff), wsel),
                  pl.BlockSpec((1, d_ff, d), wsel), pl.BlockSpec((1, 1, d), wsel)],
        out_specs=pl.BlockSpec(memory_space=pl.ANY),
        scratch_shapes=[pltpu.VMEM((d, 2 * d_ff), BF16), pltpu.VMEM((d_ff, d), BF16),
                        pltpu.VMEM((MOE_BLOCK, d_ff), BF16),
                        pltpu.VMEM((2, MOE_BLOCK, d // 2), jnp.uint32),
                        pltpu.VMEM((2, MOE_BLOCK, d // 2), jnp.uint32),
                        pltpu.SemaphoreType.DMA((2,)), pltpu.SemaphoreType.DMA((2,))],
    )
    return pl.pallas_call(
        _moe_kernel,
        grid_spec=grid_spec,
        out_shape=jax.ShapeDtypeStruct((p_rows, d // 2), jnp.uint32),
        compiler_params=pltpu.CompilerParams(dimension_semantics=("arbitrary",),
                                             vmem_limit_bytes=MOE_VMEM_LIMIT),
    )(first_blk, n_blk, cnt, xs, w1, b1g, b1l, w2, b2)


def _pack_bf16_pairs(y):
    n = y.shape[1] // 2
    lo = lax.bitcast_convert_type(y[:, :n].astype(BF16).astype(F32), jnp.uint32)
    hi = lax.bitcast_convert_type(y[:, n:].astype(BF16).astype(F32), jnp.uint32)
    return (lo >> 16) | hi


def _unpack_bf16_pairs(w):
    lo = lax.bitcast_convert_type(w << 16, F32)
    hi = lax.bitcast_convert_type(w & jnp.uint32(0xFFFF0000), F32)
    return lo, hi


def _sc_gather_rows(src, idx):
    n, w = src.shape
    m = idx.shape[0]
    workers = SC_CORES * SC_SUBCORES
    chunk = SC_GATHER_ROWS
    assert m % (workers * chunk) == 0
    per_worker = m // workers
    mesh = plsc.VectorSubcoreMesh(core_axis_name="c", subcore_axis_name="s",
                                  num_cores=SC_CORES, num_subcores=SC_SUBCORES)

    @functools.partial(
        pl.kernel, mesh=mesh, out_type=jax.ShapeDtypeStruct((m, w), src.dtype),
        scratch_types=[pltpu.VMEM((chunk,), jnp.int32), pltpu.VMEM((chunk, w), src.dtype),
                       pltpu.SemaphoreType.DMA])
    def gather(src_hbm, idx_hbm, out_hbm, idx_v, rows_v, sem):
        base = (lax.axis_index("s") * SC_CORES + lax.axis_index("c")) * per_worker

        @pl.loop(0, per_worker // chunk)
        def _(c):
            off = base + c * chunk
            pltpu.sync_copy(idx_hbm.at[pl.ds(off, chunk)], idx_v)
            pltpu.async_copy(src_hbm.at[idx_v], rows_v, sem).wait()
            pltpu.sync_copy(rows_v, out_hbm.at[pl.ds(off, chunk)])

    return gather(src, idx)


def _sc_scatter_rows(src, idx_t, n_out):
    t, w = src.shape
    fan = idx_t.shape[0]
    workers = SC_CORES * SC_SUBCORES
    chunk = SC_GATHER_ROWS
    assert t % (workers * chunk) == 0 and idx_t.shape[1] == t
    per_worker = t // workers
    mesh = plsc.VectorSubcoreMesh(core_axis_name="c", subcore_axis_name="s",
                                  num_cores=SC_CORES, num_subcores=SC_SUBCORES)

    @functools.partial(
        pl.kernel, mesh=mesh, out_type=jax.ShapeDtypeStruct((n_out, w), src.dtype),
        scratch_types=[pltpu.VMEM((fan, chunk), jnp.int32), pltpu.VMEM((chunk, w), src.dtype)])
    def scatter(src_hbm, idx_hbm, out_hbm, idx_v, rows_v):
        base = (lax.axis_index("s") * SC_CORES + lax.axis_index("c")) * per_worker

        @pl.loop(0, per_worker // chunk)
        def _(c):
            off = base + c * chunk
            pltpu.sync_copy(idx_hbm.at[:, pl.ds(off, chunk)], idx_v)
            pltpu.sync_copy(src_hbm.at[pl.ds(off, chunk)], rows_v)
            for j in range(fan):
                pltpu.sync_copy(rows_v, out_hbm.at[idx_v.at[j]])

    return scatter(src, idx_t)


def _combine_kernel(x2_ref, aux_ref, yg_ref, o_ref):
    x2 = x2_ref[...]
    aux = aux_ref[...]
    half = x2.shape[1] // 2
    out_lo, out_hi = x2[:, :half], x2[:, half:]
    for k in range(TOP_K):
        lo, hi = _unpack_bf16_pairs(yg_ref[k])
        wk = aux[:, TOP_K + k:TOP_K + k + 1]
        out_lo = out_lo + wk * lo
        out_hi = out_hi + wk * hi
    o_ref[:, :half] = out_lo
    o_ref[:, half:] = out_hi


def _combine(x2, aux, yg, tm):
    t, d = x2.shape
    return pl.pallas_call(
        _combine_kernel,
        grid=(t // tm,),
        in_specs=[pl.BlockSpec((tm, d), lambda i: (i, 0)),
                  pl.BlockSpec((tm, LANES), lambda i: (i, 0)),
                  pl.BlockSpec((TOP_K, tm, d // 2), lambda i: (0, i, 0))],
        out_specs=pl.BlockSpec((tm, d), lambda i: (i, 0)),
        out_shape=jax.ShapeDtypeStruct((t, d), F32),
        compiler_params=_params("parallel"),
    )(x2, aux, yg)


def _layer(x, mem, g_attn, w_in, moba_q_gain, moba_k_gain, ret_out_gain, g_mem, w_mem_kv,
           mem_q_gain, mem_k_gain, w_out, g_ffn, w_router, b_router, w1, b1, w2, b2):
    batch, seq, d = x.shape
    t = batch * seq
    x2d = x.reshape(t, d)
    tm = min(512, t)

    proj = _rms_proj(x2d, g_attn, w_in.astype(BF16), tm)
    oa = _moba(proj, moba_q_gain, moba_k_gain, batch, seq)
    orr = _retention(proj, ret_out_gain, batch, seq)
    km, vm = _mem_kv(mem, g_mem, w_mem_kv, mem_k_gain)
    om = _mem_attn(proj, km, vm, mem_q_gain, batch, seq, min(512, seq))
    x2, h2, aux, counts = _out_router(oa, orr, om, x2d, w_out, g_ffn, w_router, b_router, tm)

    eidx = aux[:, 0:TOP_K].astype(jnp.int32)
    rank = aux[:, 2 * TOP_K:3 * TOP_K].astype(jnp.int32)
    cnt = counts[:, 0].astype(jnp.int32)
    n_blk = (cnt + MOE_BLOCK - 1) // MOE_BLOCK
    blk_ends = jnp.cumsum(n_blk)
    first_blk = jnp.concatenate([jnp.zeros((1,), jnp.int32), blk_ends]).astype(jnp.int32)
    pstarts = first_blk[:N_EXPERTS] * MOE_BLOCK
    dest = pstarts[eidx] + rank
    a = t * TOP_K
    p_rows = -(-(a + N_EXPERTS * (MOE_BLOCK - 1)) // MOE_BLOCK) * MOE_BLOCK
    dest_t = dest.T
    xs = _sc_scatter_rows(h2, dest_t, p_rows)

    d_ff = w2.shape[1]
    b1g = b1[:, 0::2].reshape(N_EXPERTS, 1, d_ff)
    b1l = b1[:, 1::2].reshape(N_EXPERTS, 1, d_ff)
    ys = _moe_experts(xs, first_blk, n_blk.astype(jnp.int32), cnt, w1, b1g, b1l, w2,
                      b2.reshape(N_EXPERTS, 1, d))
    yg = _sc_gather_rows(ys, dest_t.reshape(a)).reshape(TOP_K, t, d // 2)
    out = _combine(x2, aux, yg, min(256, t))
    return out.reshape(batch, seq, d)


def kernel(x, mem, g_attn, w_in, moba_q_gain, moba_k_gain, ret_out_gain, g_mem, w_mem_kv,
           mem_q_gain, mem_k_gain, w_out, g_ffn, w_router, b_router, w1, b1, w2, b2):
    for l in range(g_attn.shape[0]):
        x = _layer(x, mem, g_attn[l], w_in[l], moba_q_gain[l], moba_k_gain[l], ret_out_gain[l],
                   g_mem[l], w_mem_kv[l], mem_q_gain[l], mem_k_gain[l], w_out[l], g_ffn[l],
                   w_router[l], b_router[l], w1[l], b1[l], w2[l], b2[l])
    return x
```

```python
import functools

import jax
import jax.numpy as jnp
from jax import lax
from jax.experimental import pallas as pl
from jax.experimental.pallas import tpu as pltpu
from jax.experimental.pallas import tpu_sc as plsc

F32 = jnp.float32
BF16 = jnp.bfloat16

LANES = 128
HEAD_DIM = 64
MOBA_HEADS = 8
RET_HEADS = 4
MEM_HEADS = 4
MOBA_WIDTH = MOBA_HEADS * HEAD_DIM
RET_WIDTH = RET_HEADS * HEAD_DIM
MEM_WIDTH = MEM_HEADS * HEAD_DIM
IN_WIDTH = 3 * MOBA_WIDTH + 4 * RET_WIDTH + MEM_WIDTH
MOBA_BLOCK = 256
MOBA_TOPK = 3
RET_CHUNK = 128
RET_ROPE_BASE = 10000.0
N_EXPERTS = 32
TOP_K = 4
SWIGLU_ALPHA = 1.702
SWIGLU_LIMIT = 7.0
MOE_BLOCK = 512
COMBINE_CHUNKS = 4
RET_UNROLL = 8
PREP_UNROLL = 4
EPS = 1e-6
NEG = -1e30
QK_SCALE = HEAD_DIM ** -0.5

VMEM_LIMIT = 48 * 1024 * 1024
MOE_VMEM_LIMIT = 56 * 1024 * 1024
MIX_DTYPE = BF16
SC_CORES = 2
SC_SUBCORES = 16
SC_GATHER_ROWS = 128

_NT = (((1,), (1,)), ((), ()))
_TN = (((0,), (0,)), ((), ()))


def _params(*sem):
    return pltpu.CompilerParams(dimension_semantics=sem, vmem_limit_bytes=VMEM_LIMIT)


def _lane_iota(shape):
    return lax.broadcasted_iota(jnp.int32, shape, len(shape) - 1)


def _pair_rms(t, gain, lo):
    t2 = t * t
    s0 = jnp.sum(jnp.where(lo, t2, 0.0), axis=-1, keepdims=True)
    s1 = jnp.sum(jnp.where(lo, 0.0, t2), axis=-1, keepdims=True)
    r = jnp.where(lo, lax.rsqrt(s0 / HEAD_DIM + EPS), lax.rsqrt(s1 / HEAD_DIM + EPS))
    return t * r * gain


def _rms_proj_kernel(x_ref, g_ref, w_ref, o_ref):
    x = x_ref[...]
    ms = jnp.mean(x * x, axis=-1, keepdims=True)
    h = (x * lax.rsqrt(ms + EPS) * g_ref[...]).astype(BF16)
    o_ref[...] = jnp.dot(h, w_ref[...], preferred_element_type=F32)


def _rms_proj(x2d, gain, w, tm):
    t, d = x2d.shape
    n = w.shape[1]
    return pl.pallas_call(
        _rms_proj_kernel,
        grid=(t // tm,),
        in_specs=[pl.BlockSpec((tm, d), lambda i: (i, 0)),
                  pl.BlockSpec((1, d), lambda i: (0, 0)),
                  pl.BlockSpec((d, n), lambda i: (0, 0))],
        out_specs=pl.BlockSpec((tm, n), lambda i: (i, 0)),
        out_shape=jax.ShapeDtypeStruct((t, n), F32),
        compiler_params=_params("parallel"),
    )(x2d, gain.reshape(1, d), w)


SHIFT_SAFE = 80.0
MOBA_GROUP = 4


def _moba_kernel(tab_ref, q_ref, k_ref, v_ref, qg_ref, kg_ref, m_ref, o_ref,
                 ka_ref, va_ref, qa_ref, gt_ref, acc_ref, pa_ref, pb_ref, *, n_blk, n_trips):
    blk = MOBA_BLOCK
    grp = MOBA_GROUP
    nb8 = -(-n_blk // 8) * 8
    lane = _lane_iota((blk, LANES))
    lo = lane < HEAD_DIM
    lane1 = _lane_iota((1, LANES))
    qg = qg_ref[...]
    kg = kg_ref[...]
    tri = (lax.broadcasted_iota(jnp.int32, (blk, blk), 0)
           >= lax.broadcasted_iota(jnp.int32, (blk, blk), 1))
    row_t = lax.broadcasted_iota(jnp.int32, (nb8, blk), 0)
    row_tf = row_t.astype(F32)
    fill_row = lax.broadcasted_iota(jnp.int32, (HEAD_DIM - nb8, blk), 0)
    filler = jnp.where(fill_row == HEAD_DIM - nb8 - 1, -m_ref[...], 0.0)

    def rows_of(j):
        return pl.ds(pl.multiple_of(j * blk, blk), blk)

    gt_ref[...] = jnp.zeros_like(gt_ref)

    def prep_k(j, carry):
        rows = rows_of(j)
        kn = _pair_rms(k_ref[rows, :], kg, lo)
        km = jnp.mean(kn, axis=0, keepdims=True)
        gt_ref[pl.ds(HEAD_DIM + j, 1), :] = jnp.where(lane1 < HEAD_DIM, km, 0.0)
        gt_ref[pl.ds(j, 1), :] = jnp.where(lane1 < HEAD_DIM, 0.0, km)
        tag0 = jnp.where(jnp.logical_or(lane == HEAD_DIM + j, lane == LANES - 1), 1.0, 0.0)
        tag1 = jnp.where(jnp.logical_or(lane == j, lane == HEAD_DIM - 1), 1.0, 0.0)
        ka_ref[0, rows, :] = jnp.where(lo, kn, tag0).astype(BF16)
        ka_ref[1, rows, :] = jnp.where(lo, tag1, kn).astype(BF16)
        v = v_ref[rows, :]
        va_ref[0, rows, :] = jnp.where(lo, v, jnp.where(lane == HEAD_DIM, 1.0, 0.0)).astype(BF16)
        va_ref[1, rows, :] = jnp.where(lo, jnp.where(lane == 0, 1.0, 0.0), v).astype(BF16)
        return carry

    lax.fori_loop(0, n_blk, prep_k, 0, unroll=PREP_UNROLL)

    def block_bias(g, n):
        valid = row_t < n
        g = jnp.where(valid, g, NEG)
        sel = jnp.zeros((nb8, blk), jnp.bool_)
        for _ in range(MOBA_TOPK):
            m = jnp.max(g, axis=0, keepdims=True)
            first = jnp.min(jnp.where(g == m, row_tf, jnp.inf), axis=0, keepdims=True)
            pick = row_tf == first
            sel = jnp.logical_or(sel, pick)
            g = jnp.where(pick, -jnp.inf, g)
        keep = jnp.logical_or(jnp.logical_and(sel, valid), row_t == n)
        return jnp.where(keep, 0.0, NEG)

    def prep_q(n, carry):
        rows_n = rows_of(n)
        qn = _pair_rms(q_ref[rows_n, :], qg, lo)
        gate_t = lax.dot_general(gt_ref[...].astype(BF16), qn.astype(BF16), _NT,
                                 preferred_element_type=F32)
        bias_t = jnp.concatenate([block_bias(gate_t[0:nb8], n), filler,
                                  block_bias(gate_t[HEAD_DIM:HEAD_DIM + nb8], n), filler], axis=0)
        bias = bias_t.T
        qs = qn * QK_SCALE
        qa_ref[0, rows_n, :] = jnp.where(lo, qs, bias).astype(BF16)
        qa_ref[1, rows_n, :] = jnp.where(lo, bias, qs).astype(BF16)
        return carry

    lax.fori_loop(0, n_blk, prep_q, 0, unroll=PREP_UNROLL)

    def scores(h, n, j):
        s = lax.dot_general(qa_ref[h, rows_of(n), :], ka_ref[h, rows_of(j), :], _NT,
                            preferred_element_type=F32)
        return jnp.where(jnp.logical_or(tri, j != n), s, NEG)

    def normalise(n, a0, a1):
        o_ref[rows_of(n), :] = jnp.where(lo, a0 / a0[:, HEAD_DIM:HEAD_DIM + 1],
                                         a1 / a1[:, 0:1]).astype(o_ref.dtype)

    @pl.when(tab_ref[0, 0] == 1)
    def _():
        acc_ref[...] = jnp.zeros_like(acc_ref)

        def make_probs(trip, dst_ref):
            for g in range(grp):
                n = tab_ref[1, trip * grp + g]
                j = tab_ref[2, trip * grp + g]
                for h in range(2):
                    dst_ref[g, h] = jnp.exp(scores(h, n, j)).astype(BF16)

        def apply_probs(trip, src_ref):
            for g in range(grp):
                n = tab_ref[1, trip * grp + g]
                j = tab_ref[2, trip * grp + g]
                for h in range(2):
                    acc_ref[h, rows_of(n), :] += jnp.dot(src_ref[g, h], va_ref[h, rows_of(j), :],
                                                         preferred_element_type=F32)

        make_probs(0, pa_ref)

        def two_trips(i, carry):
            apply_probs(2 * i, pa_ref)
            make_probs(2 * i + 1, pb_ref)
            apply_probs(2 * i + 1, pb_ref)
            make_probs(2 * i + 2, pa_ref)
            return carry

        lax.fori_loop(0, n_trips // 2, two_trips, 0)

        def fin(n, carry):
            normalise(n, acc_ref[0, rows_of(n), :], acc_ref[1, rows_of(n), :])
            return carry

        lax.fori_loop(0, n_blk, fin, 0, unroll=PREP_UNROLL)

    @pl.when(tab_ref[0, 0] != 1)
    def _():
        def qblock(n, carry):
            def one(j, st):
                out = []
                for h in range(2):
                    m, acc = st[2 * h], st[2 * h + 1]
                    s = scores(h, n, j)
                    m_new = jnp.maximum(m, jnp.max(s, axis=-1, keepdims=True))
                    p = jnp.exp(s - m_new).astype(BF16)
                    acc = jnp.exp(m - m_new) * acc + jnp.dot(p, va_ref[h, rows_of(j), :],
                                                             preferred_element_type=F32)
                    out += [m_new, acc]
                return tuple(out)

            zero = jnp.zeros((blk, LANES), F32)
            ninf = jnp.full((blk, 1), -jnp.inf, F32)
            st = lax.fori_loop(0, n + 1, one, (ninf, zero, ninf, zero))
            normalise(n, st[1], st[3])
            return carry

        lax.fori_loop(0, n_blk, qblock, 0)


def _moba_tiles(n_blk):
    tiles = [(n, j) for n in range(n_blk) for j in range(n + 1)]
    per_two = 2 * MOBA_GROUP
    n_trips = 2 * (-(-len(tiles) // per_two))
    tiles += [(0, 1)] * ((n_trips + 1) * MOBA_GROUP - len(tiles))
    return n_trips, tiles


def _moba(proj, q_gain, k_gain, batch, seq):
    n_blk = seq // MOBA_BLOCK
    assert seq % MOBA_BLOCK == 0 and MOBA_TOPK <= n_blk <= HEAD_DIM - 8
    n_pairs = MOBA_WIDTH // LANES
    qg = jnp.tile(q_gain, 2).reshape(1, LANES)
    kg = jnp.tile(k_gain, 2).reshape(1, LANES)
    shift = HEAD_DIM * QK_SCALE * jnp.max(jnp.abs(q_gain)) * jnp.max(jnp.abs(k_gain))
    flag = (2.0 * shift <= SHIFT_SAFE).astype(jnp.int32)
    n_trips, tiles = _moba_tiles(n_blk)
    tab = jnp.stack([jnp.full((len(tiles),), flag, jnp.int32),
                     jnp.asarray([t[0] for t in tiles], jnp.int32),
                     jnp.asarray([t[1] for t in tiles], jnp.int32)])
    mrow = jnp.full((1, MOBA_BLOCK), shift, F32)
    blk = lambda off: pl.BlockSpec((seq, LANES), lambda b, p, f: (b, off + p))
    vec = lambda w: pl.BlockSpec((1, w), lambda b, p, f: (0, 0))
    prob = pltpu.VMEM((MOBA_GROUP, 2, MOBA_BLOCK, MOBA_BLOCK), BF16)
    grid_spec = pltpu.PrefetchScalarGridSpec(
        num_scalar_prefetch=1,
        grid=(batch, n_pairs),
        in_specs=[blk(0), blk(n_pairs), blk(2 * n_pairs), vec(LANES), vec(LANES), vec(MOBA_BLOCK)],
        out_specs=pl.BlockSpec((seq, LANES), lambda b, p, f: (b, p)),
        scratch_shapes=[pltpu.VMEM((2, seq, LANES), BF16),
                        pltpu.VMEM((2, seq, LANES), BF16),
                        pltpu.VMEM((2, seq, LANES), BF16),
                        pltpu.VMEM((LANES, LANES), F32),
                        pltpu.VMEM((2, seq, LANES), F32),
                        prob, prob],
    )
    return pl.pallas_call(
        functools.partial(_moba_kernel, n_blk=n_blk, n_trips=n_trips),
        grid_spec=grid_spec,
        out_shape=jax.ShapeDtypeStruct((batch * seq, MOBA_WIDTH), MIX_DTYPE),
        compiler_params=_params("parallel", "parallel"),
    )(tab, proj, proj, proj, qg, kg, mrow)


def _ret_kernel(q_ref, k_ref, v_ref, g_ref, cos_ref, sin_ref, intra_ref, kdec_ref, qdec_ref,
                cdm_ref, mask_ref, gain_ref, o_ref, *, n_ch):
    c = RET_CHUNK
    lane = _lane_iota((c, LANES))
    lo = lane < HEAD_DIM
    first_half = (lane % HEAD_DIM) < (HEAD_DIM // 2)
    gain = gain_ref[...]
    intra0 = intra_ref[0, 0]
    intra1 = intra_ref[0, 1]
    kdec = kdec_ref[0]
    qdec = qdec_ref[0]
    cdm = cdm_ref[0]
    mask = mask_ref[...]

    def rope(t, cos_t, sin_t):
        partner = jnp.where(first_half, pltpu.roll(t, LANES - HEAD_DIM // 2, 1),
                            pltpu.roll(t, HEAD_DIM // 2, 1))
        return t * cos_t + partner * sin_t

    def chunk(i, state):
        rows = pl.ds(pl.multiple_of(i * c, c), c)
        cos_t = cos_ref[rows, :]
        sin_t = sin_ref[rows, :]
        q = rope(q_ref[rows, :], cos_t, sin_t)
        k = rope(k_ref[rows, :], cos_t, sin_t) * QK_SCALE
        v = v_ref[rows, :]
        kb = k.astype(BF16)
        s0 = lax.dot_general(jnp.where(lo, q, 0.0).astype(BF16), kb, _NT,
                             preferred_element_type=F32) * intra0
        s1 = lax.dot_general(jnp.where(lo, 0.0, q).astype(BF16), kb, _NT,
                             preferred_element_type=F32) * intra1
        sc = jnp.concatenate([s0, s1], axis=1).astype(BF16)
        vv = jnp.concatenate([jnp.where(lo, v, 0.0), jnp.where(lo, 0.0, v)], axis=0).astype(BF16)
        o = jnp.dot(sc, vv, preferred_element_type=F32)
        o = o + jnp.dot(q.astype(BF16), state.astype(BF16), preferred_element_type=F32) * qdec
        kd = (k * kdec).astype(BF16)
        kv = lax.dot_general(kd, v.astype(BF16), _TN, preferred_element_type=F32)
        state = cdm * state + mask * kv
        o2 = o * o
        m0 = jnp.sum(jnp.where(lo, o2, 0.0), axis=-1, keepdims=True)
        m1 = jnp.sum(jnp.where(lo, 0.0, o2), axis=-1, keepdims=True)
        r = jnp.where(lo, lax.rsqrt(m0 / HEAD_DIM + EPS), lax.rsqrt(m1 / HEAD_DIM + EPS))
        gt = g_ref[rows, :]
        o_ref[rows, :] = ((gt * jax.nn.sigmoid(gt)) * (o * r * gain)).astype(o_ref.dtype)
        return state

    lax.fori_loop(0, n_ch, chunk, jnp.zeros((LANES, LANES), F32), unroll=RET_UNROLL)


def _ret_tables(seq):
    c = RET_CHUNK
    half = HEAD_DIM // 2
    inv_freq = 1.0 / (RET_ROPE_BASE ** jnp.linspace(0.0, 1.0, half, dtype=F32))
    ang = jnp.arange(seq, dtype=F32)[:, None] * inv_freq[None, :]
    cos, sin = jnp.cos(ang), jnp.sin(ang)
    cos_t = jnp.tile(cos, (1, LANES // half))
    sin_t = jnp.tile(jnp.concatenate([-sin, sin], axis=1), (1, LANES // HEAD_DIM))
    log_gamma = jnp.log1p(-jnp.exp2(-5.0 - jnp.arange(RET_HEADS, dtype=F32)))
    idx = jnp.arange(c, dtype=F32)
    diff = idx[:, None] - idx[None, :]
    intra = jnp.where(diff >= 0, jnp.exp(log_gamma[:, None, None] * jnp.maximum(diff, 0.0)), 0.0)
    k_decay = jnp.exp(log_gamma[:, None] * (c - 1.0 - idx)[None, :])
    q_decay = jnp.exp(log_gamma[:, None] * (idx + 1.0)[None, :])
    chunk_decay = jnp.exp(log_gamma * c)
    n_pairs = RET_HEADS // 2

    def lanes(t):
        return jnp.repeat(t.reshape(n_pairs, 2, c).transpose(0, 2, 1), HEAD_DIM, axis=2)

    blockdiag = (jnp.arange(LANES)[:, None] // HEAD_DIM) == (jnp.arange(LANES)[None, :] // HEAD_DIM)
    mask = blockdiag.astype(F32)
    cdm = jnp.repeat(chunk_decay.reshape(n_pairs, 2), HEAD_DIM, axis=1)[:, :, None] * mask[None]
    return (cos_t, sin_t, intra.reshape(n_pairs, 2, c, c), lanes(k_decay), lanes(q_decay), cdm, mask)


def _retention(proj, out_gain, batch, seq):
    c = RET_CHUNK
    assert seq % c == 0
    n_pairs = RET_HEADS // 2
    base = 3 * MOBA_WIDTH // LANES
    cos_t, sin_t, intra, kdec, qdec, cdm, mask = _ret_tables(seq)
    blk = lambda off: pl.BlockSpec((seq, LANES), lambda b, p: (b, base + off + p))
    tab = pl.BlockSpec((seq, LANES), lambda b, p: (0, 0))
    return pl.pallas_call(
        functools.partial(_ret_kernel, n_ch=seq // c),
        grid=(batch, n_pairs),
        in_specs=[blk(0), blk(n_pairs), blk(2 * n_pairs), blk(3 * n_pairs), tab, tab,
                  pl.BlockSpec((1, 2, c, c), lambda b, p: (p, 0, 0, 0)),
                  pl.BlockSpec((1, c, LANES), lambda b, p: (p, 0, 0)),
                  pl.BlockSpec((1, c, LANES), lambda b, p: (p, 0, 0)),
                  pl.BlockSpec((1, LANES, LANES), lambda b, p: (p, 0, 0)),
                  pl.BlockSpec((LANES, LANES), lambda b, p: (0, 0)),
                  pl.BlockSpec((1, LANES), lambda b, p: (0, p))],
        out_specs=pl.BlockSpec((seq, LANES), lambda b, p: (b, p)),
        out_shape=jax.ShapeDtypeStruct((batch * seq, RET_WIDTH), MIX_DTYPE),
        compiler_params=_params("parallel", "parallel"),
    )(proj, proj, proj, proj, cos_t, sin_t, intra, kdec, qdec, cdm, mask,
      out_gain.reshape(1, RET_WIDTH))


def _mem_kv_kernel(m_ref, g_ref, w_ref, kg_ref, k_ref, v_ref):
    x = m_ref[0]
    ms = jnp.mean(x * x, axis=-1, keepdims=True)
    h = (x * lax.rsqrt(ms + EPS) * g_ref[...]).astype(BF16)
    kv = jnp.dot(h, w_ref[...], preferred_element_type=F32)
    lane = _lane_iota((x.shape[0], LANES))
    lo = lane < HEAD_DIM
    for p in range(MEM_WIDTH // LANES):
        kt = kv[:, p * LANES:(p + 1) * LANES]
        k_ref[0, :, p * LANES:(p + 1) * LANES] = _pair_rms(kt, kg_ref[...], lo).astype(BF16)
    v_ref[0] = kv[:, MEM_WIDTH:].astype(BF16)


def _mem_kv(mem, g_mem, w_mem_kv, k_gain):
    b, m, d = mem.shape
    return pl.pallas_call(
        _mem_kv_kernel,
        grid=(b,),
        in_specs=[pl.BlockSpec((1, m, d), lambda i: (i, 0, 0)),
                  pl.BlockSpec((1, d), lambda i: (0, 0)),
                  pl.BlockSpec((d, 2 * MEM_WIDTH), lambda i: (0, 0)),
                  pl.BlockSpec((1, LANES), lambda i: (0, 0))],
        out_specs=[pl.BlockSpec((1, m, MEM_WIDTH), lambda i: (i, 0, 0)),
                   pl.BlockSpec((1, m, MEM_WIDTH), lambda i: (i, 0, 0))],
        out_shape=[jax.ShapeDtypeStruct((b, m, MEM_WIDTH), BF16),
                   jax.ShapeDtypeStruct((b, m, MEM_WIDTH), BF16)],
        compiler_params=_params("parallel"),
    )(mem, g_mem.reshape(1, d), w_mem_kv.astype(BF16), jnp.tile(k_gain, 2).reshape(1, LANES))


def _mem_attn_kernel(q_ref, k_ref, v_ref, qg_ref, o_ref):
    tq = q_ref.shape[0]
    lane = _lane_iota((tq, LANES))
    lo = lane < HEAD_DIM
    for p in range(MEM_WIDTH // LANES):
        cols = slice(p * LANES, (p + 1) * LANES)
        qn = _pair_rms(q_ref[:, cols], qg_ref[...], lo) * QK_SCALE
        kt = k_ref[0, :, cols]
        vt = v_ref[0, :, cols]
        outs = []
        for h in range(2):
            qh = jnp.where(lo, qn, 0.0) if h == 0 else jnp.where(lo, 0.0, qn)
            s = lax.dot_general(qh.astype(BF16), kt, _NT, preferred_element_type=F32)
            m = jnp.max(s, axis=-1, keepdims=True)
            e = jnp.exp(s - m)
            pr = e / jnp.sum(e, axis=-1, keepdims=True)
            outs.append(jnp.dot(pr.astype(BF16), vt, preferred_element_type=F32))
        o_ref[:, cols] = jnp.where(lo, outs[0], outs[1]).astype(o_ref.dtype)


def _mem_attn(proj, km, vm, q_gain, batch, seq, tq):
    m = km.shape[1]
    qcol = (IN_WIDTH - MEM_WIDTH) // MEM_WIDTH
    assert qcol * MEM_WIDTH == IN_WIDTH - MEM_WIDTH
    nq = seq // tq
    return pl.pallas_call(
        _mem_attn_kernel,
        grid=(batch, nq),
        in_specs=[pl.BlockSpec((tq, MEM_WIDTH), lambda b, i: (b * nq + i, qcol)),
                  pl.BlockSpec((1, m, MEM_WIDTH), lambda b, i: (b, 0, 0)),
                  pl.BlockSpec((1, m, MEM_WIDTH), lambda b, i: (b, 0, 0)),
                  pl.BlockSpec((1, LANES), lambda b, i: (0, 0))],
        out_specs=pl.BlockSpec((tq, MEM_WIDTH), lambda b, i: (b * nq + i, 0)),
        out_shape=jax.ShapeDtypeStruct((batch * seq, MEM_WIDTH), MIX_DTYPE),
        compiler_params=_params("parallel", "parallel"),
    )(proj, km, vm, jnp.tile(q_gain, 2).reshape(1, LANES))


def _out_router_kernel(oa_ref, or_ref, om_ref, x_ref, wo_ref, g_ref, wr_ref, br_ref,
                       x2_ref, h2_ref, aux_ref, cnt_ref, run_ref):
    tm = x_ref.shape[0]

    @pl.when(pl.program_id(0) == 0)
    def _():
        run_ref[...] = jnp.zeros_like(run_ref)

    y = jnp.dot(oa_ref[...].astype(BF16), wo_ref[0:MOBA_WIDTH, :], preferred_element_type=F32)
    y += jnp.dot(or_ref[...].astype(BF16), wo_ref[MOBA_WIDTH:MOBA_WIDTH + RET_WIDTH, :],
                 preferred_element_type=F32)
    y += jnp.dot(om_ref[...].astype(BF16), wo_ref[MOBA_WIDTH + RET_WIDTH:, :],
                 preferred_element_type=F32)
    x2 = x_ref[...] + y
    x2_ref[...] = x2
    ms = jnp.mean(x2 * x2, axis=-1, keepdims=True)
    h2 = x2 * lax.rsqrt(ms + EPS) * g_ref[...]
    h2_ref[...] = _pack_bf16_pairs(h2)
    logits_t = lax.dot_general(wr_ref[...], h2.astype(BF16), _NT,
                               preferred_element_type=F32)[0:N_EXPERTS] + br_ref[...]
    row_f = lax.broadcasted_iota(jnp.int32, (N_EXPERTS, tm), 0).astype(F32)
    lg = logits_t
    vals, firsts, picks = [], [], []
    for _ in range(TOP_K):
        m = jnp.max(lg, axis=0, keepdims=True)
        first = jnp.min(jnp.where(lg == m, row_f, jnp.inf), axis=0, keepdims=True)
        pick = row_f == first
        vals.append(m)
        firsts.append(first)
        picks.append(pick)
        lg = jnp.where(pick, -jnp.inf, lg)
    exps = [jnp.exp(v - vals[0]) for v in vals]
    denom = exps[0] + exps[1] + exps[2] + exps[3]
    sel = jnp.zeros((N_EXPERTS, tm), F32)
    for pick in picks:
        sel = sel + jnp.where(pick, 1.0, 0.0)
    before = jnp.where(lax.broadcasted_iota(jnp.int32, (tm, tm), 0)
                       < lax.broadcasted_iota(jnp.int32, (tm, tm), 1), 1.0, 0.0).astype(BF16)
    run = run_ref[:, 0:1]
    pos = run + jnp.dot(sel.astype(BF16), before, preferred_element_type=F32)
    ranks = [jnp.sum(jnp.where(p, pos, 0.0), axis=0, keepdims=True) for p in picks]
    slot = lax.broadcasted_iota(jnp.int32, (4 * TOP_K, tm), 0)
    aux_t = jnp.zeros((4 * TOP_K, tm), F32)
    for k in range(TOP_K):
        aux_t = aux_t + jnp.where(slot == k, firsts[k], 0.0)
        aux_t = aux_t + jnp.where(slot == TOP_K + k, exps[k] / denom, 0.0)
        aux_t = aux_t + jnp.where(slot == 2 * TOP_K + k, ranks[k], 0.0)
    aux_t = jnp.concatenate([aux_t, jnp.zeros((LANES - 4 * TOP_K, tm), F32)], axis=0)
    aux_ref[...] = aux_t.T
    run = run + jnp.sum(sel, axis=1, keepdims=True)
    run_ref[...] = jnp.broadcast_to(run, run_ref.shape)
    cnt_ref[...] = jnp.broadcast_to(run, cnt_ref.shape)


def _out_router(oa, orr, om, x2d, w_out, g_ffn, w_router, b_router, tm):
    t, d = x2d.shape
    wr_t = jnp.zeros((LANES, d), BF16).at[:N_EXPERTS, :].set(w_router.T.astype(BF16))
    br = b_router.reshape(N_EXPERTS, 1).astype(F32)
    row = lambda w: pl.BlockSpec((tm, w), lambda i: (i, 0))
    const = lambda r, c: pl.BlockSpec((r, c), lambda i: (0, 0))
    return pl.pallas_call(
        _out_router_kernel,
        grid=(t // tm,),
        in_specs=[row(MOBA_WIDTH), row(RET_WIDTH), row(MEM_WIDTH), row(d),
                  const(d, d), const(1, d), const(LANES, d), const(N_EXPERTS, 1)],
        out_specs=[row(d), row(d // 2), row(LANES), const(N_EXPERTS, LANES)],
        out_shape=[jax.ShapeDtypeStruct((t, d), F32), jax.ShapeDtypeStruct((t, d // 2), jnp.uint32),
                   jax.ShapeDtypeStruct((t, LANES), F32),
                   jax.ShapeDtypeStruct((N_EXPERTS, LANES), F32)],
        scratch_shapes=[pltpu.VMEM((N_EXPERTS, LANES), F32)],
        compiler_params=_params("arbitrary"),
    )(oa, orr, om, x2d, w_out.astype(BF16), g_ffn.reshape(1, d), wr_t, br)


MXU_COLS = 256


def _moe_kernel(first_ref, nblk_ref, cnt_ref, xs_hbm, w1_ref, b1g_ref, b1l_ref, w2_ref, b2_ref,
                ys_hbm, w1p_ref, w2b_ref, act_ref, xbuf, ybuf, xsem, ysem):
    e = pl.program_id(0)
    rb = MOE_BLOCK
    half = MXU_COLS // 2
    n_chunks = w1_ref.shape[2] // MXU_COLS
    g0 = first_ref[e]
    total = first_ref[N_EXPERTS]

    def x_copy(g, slot):
        return pltpu.make_async_copy(xs_hbm.at[pl.ds(pl.multiple_of(g * rb, rb), rb)],
                                     xbuf.at[slot], xsem.at[slot])

    def y_copy(g, slot):
        return pltpu.make_async_copy(ybuf.at[slot],
                                     ys_hbm.at[pl.ds(pl.multiple_of(g * rb, rb), rb)],
                                     ysem.at[slot])

    @pl.when(e == 0)
    def _():
        x_copy(0, 0).start()

    @pl.when(nblk_ref[e] > 0)
    def _():
        r = lax.broadcasted_iota(jnp.int32, (MXU_COLS, MXU_COLS), 0)
        c = lax.broadcasted_iota(jnp.int32, (MXU_COLS, MXU_COLS), 1)
        src = jnp.where(c < half, 2 * c, 2 * (c - half) + 1)
        perm = jnp.where(r == src, 1.0, 0.0).astype(BF16)
        for ch in range(n_chunks):
            cols = slice(ch * MXU_COLS, (ch + 1) * MXU_COLS)
            w = w1_ref[0, :, cols].astype(BF16)
            w1p_ref[:, cols] = jnp.dot(w, perm, preferred_element_type=F32).astype(BF16)
        w2b_ref[...] = w2_ref[0].astype(BF16)

    def block(b, carry):
        g = g0 + b
        slot = g % 2
        x_copy(g, slot).wait()

        @pl.when(g + 1 < total)
        def _():
            x_copy(g + 1, 1 - slot).start()

        row = lax.broadcasted_iota(jnp.int32, (rb, xbuf.shape[2]), 0)
        packed = jnp.where(row < cnt_ref[e] - b * rb, xbuf[slot], jnp.uint32(0))
        x_lo, x_hi = _unpack_bf16_pairs(packed)
        x = jnp.concatenate([x_lo.astype(BF16), x_hi.astype(BF16)], axis=1)
        for ch in range(n_chunks):
            z = jnp.dot(x, w1p_ref[:, ch * MXU_COLS:(ch + 1) * MXU_COLS], preferred_element_type=F32)
            hs = slice(ch * half, (ch + 1) * half)
            xg = jnp.minimum(z[:, :half] + b1g_ref[0][:, hs], SWIGLU_LIMIT)
            xl = jnp.clip(z[:, half:] + b1l_ref[0][:, hs], -SWIGLU_LIMIT, SWIGLU_LIMIT)
            act_ref[:, hs] = (xg * jax.nn.sigmoid(SWIGLU_ALPHA * xg) * (xl + 1.0)).astype(BF16)
        y = jnp.dot(act_ref[...], w2b_ref[...], preferred_element_type=F32) + b2_ref[0]

        @pl.when(g >= 2)
        def _():
            y_copy(g - 2, slot).wait()

        ybuf[slot] = _pack_bf16_pairs(y)
        y_copy(g, slot).start()
        return carry

    lax.fori_loop(0, nblk_ref[e], block, 0)

    @pl.when(e == pl.num_programs(0) - 1)
    def _():
        @pl.when(total >= 2)
        def _():
            y_copy(total - 2, total % 2).wait()

        y_copy(total - 1, (total - 1) % 2).wait()


def _moe_experts(xs, first_blk, n_blk, cnt, w1, b1g, b1l, w2, b2):
    p_rows = xs.shape[0]
    n_exp, d = w1.shape[0], w1.shape[1]
    d_ff = w2.shape[1]
    assert w1.shape[2] == 2 * d_ff and (2 * d_ff) % MXU_COLS == 0 and xs.shape[1] == d // 2
    wsel = lambda e, *_: (e, 0, 0)
    grid_spec = pltpu.PrefetchScalarGridSpec(
        num_scalar_prefetch=3,
        grid=(n_exp,),
        in_specs=[pl.BlockSpec(memory_space=pl.ANY),
                  pl.BlockSpec((1, d, 2 * d_ff), wsel),
                  pl.BlockSpec((1, 1, d_ff), wsel), pl.BlockSpec((1, 1, d_ff), wsel),
                  pl.BlockSpec((1, d_ff, d), wsel), pl.BlockSpec((1, 1, d), wsel)],
        out_specs=pl.BlockSpec(memory_space=pl.ANY),
        scratch_shapes=[pltpu.VMEM((d, 2 * d_ff), BF16), pltpu.VMEM((d_ff, d), BF16),
                        pltpu.VMEM((MOE_BLOCK, d_ff), BF16),
                        pltpu.VMEM((2, MOE_BLOCK, d // 2), jnp.uint32),
                        pltpu.VMEM((2, MOE_BLOCK, d // 2), jnp.uint32),
                        pltpu.SemaphoreType.DMA((2,)), pltpu.SemaphoreType.DMA((2,))],
    )
    return pl.pallas_call(
        _moe_kernel,
        grid_spec=grid_spec,
        out_shape=jax.ShapeDtypeStruct((p_rows, d // 2), jnp.uint32),
        compiler_params=pltpu.CompilerParams(dimension_semantics=("arbitrary",),
                                             vmem_limit_bytes=MOE_VMEM_LIMIT),
    )(first_blk, n_blk, cnt, xs, w1, b1g, b1l, w2, b2)


def _pack_bf16_pairs(y):
    n = y.shape[1] // 2
    lo = lax.bitcast_convert_type(y[:, :n].astype(BF16).astype(F32), jnp.uint32)
    hi = lax.bitcast_convert_type(y[:, n:].astype(BF16).astype(F32), jnp.uint32)
    return (lo >> 16) | hi


def _unpack_bf16_pairs(w):
    lo = lax.bitcast_convert_type(w << 16, F32)
    hi = lax.bitcast_convert_type(w & jnp.uint32(0xFFFF0000), F32)
    return lo, hi


def _sc_gather_rows(src, idx):
    n, w = src.shape
    m = idx.shape[0]
    workers = SC_CORES * SC_SUBCORES
    chunk = SC_GATHER_ROWS
    assert m % (workers * chunk) == 0
    per_worker = m // workers
    mesh = plsc.VectorSubcoreMesh(core_axis_name="c", subcore_axis_name="s",
                                  num_cores=SC_CORES, num_subcores=SC_SUBCORES)

    @functools.partial(
        pl.kernel, mesh=mesh, out_type=jax.ShapeDtypeStruct((m, w), src.dtype),
        scratch_types=[pltpu.VMEM((chunk,), jnp.int32), pltpu.VMEM((chunk, w), src.dtype),
                       pltpu.SemaphoreType.DMA])
    def gather(src_hbm, idx_hbm, out_hbm, idx_v, rows_v, sem):
        base = (lax.axis_index("s") * SC_CORES + lax.axis_index("c")) * per_worker

        @pl.loop(0, per_worker // chunk)
        def _(c):
            off = base + c * chunk
            pltpu.sync_copy(idx_hbm.at[pl.ds(off, chunk)], idx_v)
            pltpu.async_copy(src_hbm.at[idx_v], rows_v, sem).wait()
            pltpu.sync_copy(rows_v, out_hbm.at[pl.ds(off, chunk)])

    return gather(src, idx)


def _sc_scatter_rows(src, idx_t, n_out):
    t, w = src.shape
    fan = idx_t.shape[0]
    workers = SC_CORES * SC_SUBCORES
    chunk = SC_GATHER_ROWS
    assert t % (workers * chunk) == 0 and idx_t.shape[1] == t
    per_worker = t // workers
    mesh = plsc.VectorSubcoreMesh(core_axis_name="c", subcore_axis_name="s",
                                  num_cores=SC_CORES, num_subcores=SC_SUBCORES)

    @functools.partial(
        pl.kernel, mesh=mesh, out_type=jax.ShapeDtypeStruct((n_out, w), src.dtype),
        scratch_types=[pltpu.VMEM((fan, chunk), jnp.int32), pltpu.VMEM((chunk, w), src.dtype)])
    def scatter(src_hbm, idx_hbm, out_hbm, idx_v, rows_v):
        base = (lax.axis_index("s") * SC_CORES + lax.axis_index("c")) * per_worker

        @pl.loop(0, per_worker // chunk)
        def _(c):
            off = base + c * chunk
            pltpu.sync_copy(idx_hbm.at[:, pl.ds(off, chunk)], idx_v)
            pltpu.sync_copy(src_hbm.at[pl.ds(off, chunk)], rows_v)
            for j in range(fan):
                pltpu.sync_copy(rows_v, out_hbm.at[idx_v.at[j]])

    return scatter(src, idx_t)


def _combine_kernel(x2_ref, aux_ref, yg_ref, o_ref):
    x2 = x2_ref[...]
    aux = aux_ref[...]
    half = x2.shape[1] // 2
    out_lo, out_hi = x2[:, :half], x2[:, half:]
    for k in range(TOP_K):
        lo, hi = _unpack_bf16_pairs(yg_ref[k])
        wk = aux[:, TOP_K + k:TOP_K + k + 1]
        out_lo = out_lo + wk * lo
        out_hi = out_hi + wk * hi
    o_ref[:, :half] = out_lo
    o_ref[:, half:] = out_hi


def _combine(acc, aux, yg, first_row, tm):
    t, d = acc.shape
    tc = yg.shape[1]
    assert tc % tm == 0 and first_row % tm == 0
    off = first_row // tm
    return pl.pallas_call(
        _combine_kernel,
        grid=(tc // tm,),
        in_specs=[pl.BlockSpec((tm, d), lambda i: (off + i, 0)),
                  pl.BlockSpec((tm, LANES), lambda i: (off + i, 0)),
                  pl.BlockSpec((TOP_K, tm, d // 2), lambda i: (0, i, 0))],
        out_specs=pl.BlockSpec((tm, d), lambda i: (off + i, 0)),
        out_shape=jax.ShapeDtypeStruct((t, d), F32),
        input_output_aliases={0: 0},
        compiler_params=_params("parallel"),
    )(acc, aux, yg)


def _layer(x, mem, g_attn, w_in, moba_q_gain, moba_k_gain, ret_out_gain, g_mem, w_mem_kv,
           mem_q_gain, mem_k_gain, w_out, g_ffn, w_router, b_router, w1, b1, w2, b2):
    batch, seq, d = x.shape
    t = batch * seq
    x2d = x.reshape(t, d)
    tm = min(512, t)

    proj = _rms_proj(x2d, g_attn, w_in.astype(BF16), tm)
    oa = _moba(proj, moba_q_gain, moba_k_gain, batch, seq)
    orr = _retention(proj, ret_out_gain, batch, seq)
    km, vm = _mem_kv(mem, g_mem, w_mem_kv, mem_k_gain)
    om = _mem_attn(proj, km, vm, mem_q_gain, batch, seq, min(512, seq))
    x2, h2, aux, counts = _out_router(oa, orr, om, x2d, w_out, g_ffn, w_router, b_router, tm)

    eidx = aux[:, 0:TOP_K].astype(jnp.int32)
    rank = aux[:, 2 * TOP_K:3 * TOP_K].astype(jnp.int32)
    cnt = counts[:, 0].astype(jnp.int32)
    n_blk = (cnt + MOE_BLOCK - 1) // MOE_BLOCK
    blk_ends = jnp.cumsum(n_blk)
    first_blk = jnp.concatenate([jnp.zeros((1,), jnp.int32), blk_ends]).astype(jnp.int32)
    pstarts = first_blk[:N_EXPERTS] * MOE_BLOCK
    dest = pstarts[eidx] + rank
    a = t * TOP_K
    p_rows = -(-(a + N_EXPERTS * (MOE_BLOCK - 1)) // MOE_BLOCK) * MOE_BLOCK
    dest_t = dest.T
    xs = _sc_scatter_rows(h2, dest_t, p_rows)

    d_ff = w2.shape[1]
    b1g = b1[:, 0::2].reshape(N_EXPERTS, 1, d_ff)
    b1l = b1[:, 1::2].reshape(N_EXPERTS, 1, d_ff)
    ys = _moe_experts(xs, first_blk, n_blk.astype(jnp.int32), cnt, w1, b1g, b1l, w2,
                      b2.reshape(N_EXPERTS, 1, d))
    n_chunks = COMBINE_CHUNKS if t % (COMBINE_CHUNKS * SC_CORES * SC_SUBCORES * SC_GATHER_ROWS) == 0 else 1
    tc = t // n_chunks
    out = x2
    for c in range(n_chunks):
        idx = dest_t[:, c * tc:(c + 1) * tc].reshape(TOP_K * tc)
        yg = _sc_gather_rows(ys, idx).reshape(TOP_K, tc, d // 2)
        out = _combine(out, aux, yg, c * tc, min(256, tc))
    return out.reshape(batch, seq, d)


def kernel(x, mem, g_attn, w_in, moba_q_gain, moba_k_gain, ret_out_gain, g_mem, w_mem_kv,
           mem_q_gain, mem_k_gain, w_out, g_ffn, w_router, b_router, w1, b1, w2, b2):
    for l in range(g_attn.shape[0]):
        x = _layer(x, mem, g_attn[l], w_in[l], moba_q_gain[l], moba_k_gain[l], ret_out_gain[l],
                   g_mem[l], w_mem_kv[l], mem_q_gain[l], mem_k_gain[l], w_out[l], g_ffn[l],
                   w_router[l], b_router[l], w1[l], b1[l], w2[l], b2[l])
    return x
```

```python
import functools

import jax
import jax.numpy as jnp
from jax import lax
from jax.experimental import pallas as pl
from jax.experimental.pallas import tpu as pltpu
from jax.experimental.pallas import tpu_sc as plsc

F32 = jnp.float32
BF16 = jnp.bfloat16

LANES = 128
HEAD_DIM = 64
MOBA_HEADS = 8
RET_HEADS = 4
MEM_HEADS = 4
MOBA_WIDTH = MOBA_HEADS * HEAD_DIM
RET_WIDTH = RET_HEADS * HEAD_DIM
MEM_WIDTH = MEM_HEADS * HEAD_DIM
IN_WIDTH = 3 * MOBA_WIDTH + 4 * RET_WIDTH + MEM_WIDTH
MOBA_BLOCK = 256
MOBA_TOPK = 3
RET_CHUNK = 128
RET_ROPE_BASE = 10000.0
N_EXPERTS = 32
TOP_K = 4
SWIGLU_ALPHA = 1.702
SWIGLU_LIMIT = 7.0
MOE_BLOCK = 512
COMBINE_CHUNKS = 4
RET_UNROLL = 8
PREP_UNROLL = 4
EPS = 1e-6
NEG = -1e30
QK_SCALE = HEAD_DIM ** -0.5

VMEM_LIMIT = 48 * 1024 * 1024
MOE_VMEM_LIMIT = 56 * 1024 * 1024
MIX_DTYPE = BF16
SC_CORES = 2
SC_SUBCORES = 16
SC_GATHER_ROWS = 128

_NT = (((1,), (1,)), ((), ()))
_TN = (((0,), (0,)), ((), ()))


def _params(*sem):
    return pltpu.CompilerParams(dimension_semantics=sem, vmem_limit_bytes=VMEM_LIMIT)


def _lane_iota(shape):
    return lax.broadcasted_iota(jnp.int32, shape, len(shape) - 1)


def _pair_rms(t, gain, lo):
    t2 = t * t
    s0 = jnp.sum(jnp.where(lo, t2, 0.0), axis=-1, keepdims=True)
    s1 = jnp.sum(jnp.where(lo, 0.0, t2), axis=-1, keepdims=True)
    r = jnp.where(lo, lax.rsqrt(s0 / HEAD_DIM + EPS), lax.rsqrt(s1 / HEAD_DIM + EPS))
    return t * r * gain


def _rms_proj_kernel(x_ref, g_ref, w_ref, o_ref):
    x = x_ref[...]
    ms = jnp.mean(x * x, axis=-1, keepdims=True)
    h = (x * lax.rsqrt(ms + EPS) * g_ref[...]).astype(BF16)
    o_ref[...] = jnp.dot(h, w_ref[...], preferred_element_type=F32)


def _rms_proj(x2d, gain, w, tm):
    t, d = x2d.shape
    n = w.shape[1]
    return pl.pallas_call(
        _rms_proj_kernel,
        grid=(t // tm,),
        in_specs=[pl.BlockSpec((tm, d), lambda i: (i, 0)),
                  pl.BlockSpec((1, d), lambda i: (0, 0)),
                  pl.BlockSpec((d, n), lambda i: (0, 0))],
        out_specs=pl.BlockSpec((tm, n), lambda i: (i, 0)),
        out_shape=jax.ShapeDtypeStruct((t, n), F32),
        compiler_params=_params("parallel"),
    )(x2d, gain.reshape(1, d), w)


SHIFT_SAFE = 80.0
MOBA_GROUP = 4


def _moba_kernel(tab_ref, q_ref, k_ref, v_ref, qg_ref, kg_ref, m_ref, o_ref,
                 ka_ref, va_ref, qa_ref, gt_ref, acc_ref, pa_ref, pb_ref, *, n_blk, n_trips):
    blk = MOBA_BLOCK
    grp = MOBA_GROUP
    nb8 = -(-n_blk // 8) * 8
    lane = _lane_iota((blk, LANES))
    lo = lane < HEAD_DIM
    lane1 = _lane_iota((1, LANES))
    qg = qg_ref[...]
    kg = kg_ref[...]
    tri = (lax.broadcasted_iota(jnp.int32, (blk, blk), 0)
           >= lax.broadcasted_iota(jnp.int32, (blk, blk), 1))
    row_t = lax.broadcasted_iota(jnp.int32, (nb8, blk), 0)
    row_tf = row_t.astype(F32)
    fill_row = lax.broadcasted_iota(jnp.int32, (HEAD_DIM - nb8, blk), 0)
    filler = jnp.where(fill_row == HEAD_DIM - nb8 - 1, -m_ref[...], 0.0)

    def rows_of(j):
        return pl.ds(pl.multiple_of(j * blk, blk), blk)

    gt_ref[...] = jnp.zeros_like(gt_ref)

    def prep_k(j, carry):
        rows = rows_of(j)
        kn = _pair_rms(k_ref[rows, :], kg, lo)
        km = jnp.mean(kn, axis=0, keepdims=True)
        gt_ref[pl.ds(HEAD_DIM + j, 1), :] = jnp.where(lane1 < HEAD_DIM, km, 0.0)
        gt_ref[pl.ds(j, 1), :] = jnp.where(lane1 < HEAD_DIM, 0.0, km)
        tag0 = jnp.where(jnp.logical_or(lane == HEAD_DIM + j, lane == LANES - 1), 1.0, 0.0)
        tag1 = jnp.where(jnp.logical_or(lane == j, lane == HEAD_DIM - 1), 1.0, 0.0)
        ka_ref[0, rows, :] = jnp.where(lo, kn, tag0).astype(BF16)
        ka_ref[1, rows, :] = jnp.where(lo, tag1, kn).astype(BF16)
        v = v_ref[rows, :]
        va_ref[0, rows, :] = jnp.where(lo, v, jnp.where(lane == HEAD_DIM, 1.0, 0.0)).astype(BF16)
        va_ref[1, rows, :] = jnp.where(lo, jnp.where(lane == 0, 1.0, 0.0), v).astype(BF16)
        return carry

    lax.fori_loop(0, n_blk, prep_k, 0, unroll=PREP_UNROLL)

    def block_bias(g, n):
        valid = row_t < n
        g = jnp.where(valid, g, NEG)
        sel = jnp.zeros((nb8, blk), jnp.bool_)
        for _ in range(MOBA_TOPK):
            m = jnp.max(g, axis=0, keepdims=True)
            first = jnp.min(jnp.where(g == m, row_tf, jnp.inf), axis=0, keepdims=True)
            pick = row_tf == first
            sel = jnp.logical_or(sel, pick)
            g = jnp.where(pick, -jnp.inf, g)
        keep = jnp.logical_or(jnp.logical_and(sel, valid), row_t == n)
        return jnp.where(keep, 0.0, NEG)

    def prep_q(n, carry):
        rows_n = rows_of(n)
        qn = _pair_rms(q_ref[rows_n, :], qg, lo)
        gate_t = lax.dot_general(gt_ref[...].astype(BF16), qn.astype(BF16), _NT,
                                 preferred_element_type=F32)
        bias_t = jnp.concatenate([block_bias(gate_t[0:nb8], n), filler,
                                  block_bias(gate_t[HEAD_DIM:HEAD_DIM + nb8], n), filler], axis=0)
        bias = bias_t.T
        qs = qn * QK_SCALE
        qa_ref[0, rows_n, :] = jnp.where(lo, qs, bias).astype(BF16)
        qa_ref[1, rows_n, :] = jnp.where(lo, bias, qs).astype(BF16)
        return carry

    lax.fori_loop(0, n_blk, prep_q, 0, unroll=PREP_UNROLL)

    def scores(h, n, j):
        s = lax.dot_general(qa_ref[h, rows_of(n), :], ka_ref[h, rows_of(j), :], _NT,
                            preferred_element_type=F32)
        return jnp.where(jnp.logical_or(tri, j != n), s, NEG)

    def normalise(n, a0, a1):
        o_ref[rows_of(n), :] = jnp.where(lo, a0 / a0[:, HEAD_DIM:HEAD_DIM + 1],
                                         a1 / a1[:, 0:1]).astype(o_ref.dtype)

    @pl.when(tab_ref[0, 0] == 1)
    def _():
        acc_ref[...] = jnp.zeros_like(acc_ref)

        def make_probs(trip, dst_ref):
            for g in range(grp):
                n = tab_ref[1, trip * grp + g]
                j = tab_ref[2, trip * grp + g]
                for h in range(2):
                    dst_ref[g, h] = jnp.exp(scores(h, n, j)).astype(BF16)

        def apply_probs(trip, src_ref):
            for g in range(grp):
                n = tab_ref[1, trip * grp + g]
                j = tab_ref[2, trip * grp + g]
                for h in range(2):
                    acc_ref[h, rows_of(n), :] += jnp.dot(src_ref[g, h], va_ref[h, rows_of(j), :],
                                                         preferred_element_type=F32)

        make_probs(0, pa_ref)

        def two_trips(i, carry):
            apply_probs(2 * i, pa_ref)
            make_probs(2 * i + 1, pb_ref)
            apply_probs(2 * i + 1, pb_ref)
            make_probs(2 * i + 2, pa_ref)
            return carry

        lax.fori_loop(0, n_trips // 2, two_trips, 0)

        def fin(n, carry):
            normalise(n, acc_ref[0, rows_of(n), :], acc_ref[1, rows_of(n), :])
            return carry

        lax.fori_loop(0, n_blk, fin, 0, unroll=PREP_UNROLL)

    @pl.when(tab_ref[0, 0] != 1)
    def _():
        def qblock(n, carry):
            def one(j, st):
                out = []
                for h in range(2):
                    m, acc = st[2 * h], st[2 * h + 1]
                    s = scores(h, n, j)
                    m_new = jnp.maximum(m, jnp.max(s, axis=-1, keepdims=True))
                    p = jnp.exp(s - m_new).astype(BF16)
                    acc = jnp.exp(m - m_new) * acc + jnp.dot(p, va_ref[h, rows_of(j), :],
                                                             preferred_element_type=F32)
                    out += [m_new, acc]
                return tuple(out)

            zero = jnp.zeros((blk, LANES), F32)
            ninf = jnp.full((blk, 1), -jnp.inf, F32)
            st = lax.fori_loop(0, n + 1, one, (ninf, zero, ninf, zero))
            normalise(n, st[1], st[3])
            return carry

        lax.fori_loop(0, n_blk, qblock, 0)


def _moba_tiles(n_blk):
    tiles = [(n, j) for n in range(n_blk) for j in range(n + 1)]
    per_two = 2 * MOBA_GROUP
    n_trips = 2 * (-(-len(tiles) // per_two))
    tiles += [(0, 1)] * ((n_trips + 1) * MOBA_GROUP - len(tiles))
    return n_trips, tiles


def _moba(proj, q_gain, k_gain, batch, seq):
    n_blk = seq // MOBA_BLOCK
    assert seq % MOBA_BLOCK == 0 and MOBA_TOPK <= n_blk <= HEAD_DIM - 8
    n_pairs = MOBA_WIDTH // LANES
    qg = jnp.tile(q_gain, 2).reshape(1, LANES)
    kg = jnp.tile(k_gain, 2).reshape(1, LANES)
    shift = HEAD_DIM * QK_SCALE * jnp.max(jnp.abs(q_gain)) * jnp.max(jnp.abs(k_gain))
    flag = (2.0 * shift <= SHIFT_SAFE).astype(jnp.int32)
    n_trips, tiles = _moba_tiles(n_blk)
    tab = jnp.stack([jnp.full((len(tiles),), flag, jnp.int32),
                     jnp.asarray([t[0] for t in tiles], jnp.int32),
                     jnp.asarray([t[1] for t in tiles], jnp.int32)])
    mrow = jnp.full((1, MOBA_BLOCK), shift, F32)
    blk = lambda off: pl.BlockSpec((seq, LANES), lambda b, p, f: (b, off + p))
    vec = lambda w: pl.BlockSpec((1, w), lambda b, p, f: (0, 0))
    prob = pltpu.VMEM((MOBA_GROUP, 2, MOBA_BLOCK, MOBA_BLOCK), BF16)
    grid_spec = pltpu.PrefetchScalarGridSpec(
        num_scalar_prefetch=1,
        grid=(batch, n_pairs),
        in_specs=[blk(0), blk(n_pairs), blk(2 * n_pairs), vec(LANES), vec(LANES), vec(MOBA_BLOCK)],
        out_specs=pl.BlockSpec((seq, LANES), lambda b, p, f: (b, p)),
        scratch_shapes=[pltpu.VMEM((2, seq, LANES), BF16),
                        pltpu.VMEM((2, seq, LANES), BF16),
                        pltpu.VMEM((2, seq, LANES), BF16),
                        pltpu.VMEM((LANES, LANES), F32),
                        pltpu.VMEM((2, seq, LANES), F32),
                        prob, prob],
    )
    return pl.pallas_call(
        functools.partial(_moba_kernel, n_blk=n_blk, n_trips=n_trips),
        grid_spec=grid_spec,
        out_shape=jax.ShapeDtypeStruct((batch * seq, MOBA_WIDTH), MIX_DTYPE),
        compiler_params=_params("parallel", "parallel"),
    )(tab, proj, proj, proj, qg, kg, mrow)


def _ret_kernel(q_ref, k_ref, v_ref, g_ref, cos_ref, sin_ref, intra_ref, kdec_ref, qdec_ref,
                cdm_ref, mask_ref, gain_ref, o_ref, *, n_ch):
    c = RET_CHUNK
    lane = _lane_iota((c, LANES))
    lo = lane < HEAD_DIM
    first_half = (lane % HEAD_DIM) < (HEAD_DIM // 2)
    gain = gain_ref[...]
    intra0 = intra_ref[0, 0]
    intra1 = intra_ref[0, 1]
    kdec = kdec_ref[0]
    qdec = qdec_ref[0]
    cdm = cdm_ref[0]
    mask = mask_ref[...]

    def rope(t, cos_t, sin_t):
        partner = jnp.where(first_half, pltpu.roll(t, LANES - HEAD_DIM // 2, 1),
                            pltpu.roll(t, HEAD_DIM // 2, 1))
        return t * cos_t + partner * sin_t

    def chunk(i, state):
        rows = pl.ds(pl.multiple_of(i * c, c), c)
        cos_t = cos_ref[rows, :]
        sin_t = sin_ref[rows, :]
        q = rope(q_ref[rows, :], cos_t, sin_t)
        k = rope(k_ref[rows, :], cos_t, sin_t) * QK_SCALE
        v = v_ref[rows, :]
        kb = k.astype(BF16)
        s0 = lax.dot_general(jnp.where(lo, q, 0.0).astype(BF16), kb, _NT,
                             preferred_element_type=F32) * intra0
        s1 = lax.dot_general(jnp.where(lo, 0.0, q).astype(BF16), kb, _NT,
                             preferred_element_type=F32) * intra1
        sc = jnp.concatenate([s0, s1], axis=1).astype(BF16)
        vv = jnp.concatenate([jnp.where(lo, v, 0.0), jnp.where(lo, 0.0, v)], axis=0).astype(BF16)
        o = jnp.dot(sc, vv, preferred_element_type=F32)
        o = o + jnp.dot(q.astype(BF16), state.astype(BF16), preferred_element_type=F32) * qdec
        kd = (k * kdec).astype(BF16)
        kv = lax.dot_general(kd, v.astype(BF16), _TN, preferred_element_type=F32)
        state = cdm * state + mask * kv
        o2 = o * o
        m0 = jnp.sum(jnp.where(lo, o2, 0.0), axis=-1, keepdims=True)
        m1 = jnp.sum(jnp.where(lo, 0.0, o2), axis=-1, keepdims=True)
        r = jnp.where(lo, lax.rsqrt(m0 / HEAD_DIM + EPS), lax.rsqrt(m1 / HEAD_DIM + EPS))
        gt = g_ref[rows, :]
        o_ref[rows, :] = ((gt * jax.nn.sigmoid(gt)) * (o * r * gain)).astype(o_ref.dtype)
        return state

    lax.fori_loop(0, n_ch, chunk, jnp.zeros((LANES, LANES), F32), unroll=RET_UNROLL)


def _ret_tables(seq):
    c = RET_CHUNK
    half = HEAD_DIM // 2
    inv_freq = 1.0 / (RET_ROPE_BASE ** jnp.linspace(0.0, 1.0, half, dtype=F32))
    ang = jnp.arange(seq, dtype=F32)[:, None] * inv_freq[None, :]
    cos, sin = jnp.cos(ang), jnp.sin(ang)
    cos_t = jnp.tile(cos, (1, LANES // half))
    sin_t = jnp.tile(jnp.concatenate([-sin, sin], axis=1), (1, LANES // HEAD_DIM))
    log_gamma = jnp.log1p(-jnp.exp2(-5.0 - jnp.arange(RET_HEADS, dtype=F32)))
    idx = jnp.arange(c, dtype=F32)
    diff = idx[:, None] - idx[None, :]
    intra = jnp.where(diff >= 0, jnp.exp(log_gamma[:, None, None] * jnp.maximum(diff, 0.0)), 0.0)
    k_decay = jnp.exp(log_gamma[:, None] * (c - 1.0 - idx)[None, :])
    q_decay = jnp.exp(log_gamma[:, None] * (idx + 1.0)[None, :])
    chunk_decay = jnp.exp(log_gamma * c)
    n_pairs = RET_HEADS // 2

    def lanes(t):
        return jnp.repeat(t.reshape(n_pairs, 2, c).transpose(0, 2, 1), HEAD_DIM, axis=2)

    blockdiag = (jnp.arange(LANES)[:, None] // HEAD_DIM) == (jnp.arange(LANES)[None, :] // HEAD_DIM)
    mask = blockdiag.astype(F32)
    cdm = jnp.repeat(chunk_decay.reshape(n_pairs, 2), HEAD_DIM, axis=1)[:, :, None] * mask[None]
    return (cos_t, sin_t, intra.reshape(n_pairs, 2, c, c), lanes(k_decay), lanes(q_decay), cdm, mask)


def _retention(proj, out_gain, batch, seq):
    c = RET_CHUNK
    assert seq % c == 0
    n_pairs = RET_HEADS // 2
    base = 3 * MOBA_WIDTH // LANES
    cos_t, sin_t, intra, kdec, qdec, cdm, mask = _ret_tables(seq)
    blk = lambda off: pl.BlockSpec((seq, LANES), lambda b, p: (b, base + off + p))
    tab = pl.BlockSpec((seq, LANES), lambda b, p: (0, 0))
    return pl.pallas_call(
        functools.partial(_ret_kernel, n_ch=seq // c),
        grid=(batch, n_pairs),
        in_specs=[blk(0), blk(n_pairs), blk(2 * n_pairs), blk(3 * n_pairs), tab, tab,
                  pl.BlockSpec((1, 2, c, c), lambda b, p: (p, 0, 0, 0)),
                  pl.BlockSpec((1, c, LANES), lambda b, p: (p, 0, 0)),
                  pl.BlockSpec((1, c, LANES), lambda b, p: (p, 0, 0)),
                  pl.BlockSpec((1, LANES, LANES), lambda b, p: (p, 0, 0)),
                  pl.BlockSpec((LANES, LANES), lambda b, p: (0, 0)),
                  pl.BlockSpec((1, LANES), lambda b, p: (0, p))],
        out_specs=pl.BlockSpec((seq, LANES), lambda b, p: (b, p)),
        out_shape=jax.ShapeDtypeStruct((batch * seq, RET_WIDTH), MIX_DTYPE),
        compiler_params=_params("parallel", "parallel"),
    )(proj, proj, proj, proj, cos_t, sin_t, intra, kdec, qdec, cdm, mask,
      out_gain.reshape(1, RET_WIDTH))


def _head_tags(lane, half, k_side):
    if k_side:
        one_at = LANES - 1 if half == 0 else HEAD_DIM - 1
    else:
        one_at = HEAD_DIM if half == 0 else 0
    return jnp.where(lane == one_at, 1.0, 0.0)


def _mem_kv_kernel(m_ref, g_ref, w_ref, kg_ref, k_ref, v_ref):
    x = m_ref[0]
    ms = jnp.mean(x * x, axis=-1, keepdims=True)
    h = (x * lax.rsqrt(ms + EPS) * g_ref[...]).astype(BF16)
    kv = jnp.dot(h, w_ref[...], preferred_element_type=F32)
    lane = _lane_iota((x.shape[0], LANES))
    lo = lane < HEAD_DIM
    for p in range(MEM_WIDTH // LANES):
        kt = _pair_rms(kv[:, p * LANES:(p + 1) * LANES], kg_ref[...], lo)
        vt = kv[:, MEM_WIDTH + p * LANES:MEM_WIDTH + (p + 1) * LANES]
        for half in range(2):
            own = lo if half == 0 else jnp.logical_not(lo)
            k_ref[0, 2 * p + half] = jnp.where(own, kt, _head_tags(lane, half, True)).astype(BF16)
            v_ref[0, 2 * p + half] = jnp.where(own, vt, _head_tags(lane, half, False)).astype(BF16)


def _mem_kv(mem, g_mem, w_mem_kv, k_gain):
    b, m, d = mem.shape
    spec = pl.BlockSpec((1, MEM_HEADS, m, LANES), lambda i: (i, 0, 0, 0))
    shape = jax.ShapeDtypeStruct((b, MEM_HEADS, m, LANES), BF16)
    return pl.pallas_call(
        _mem_kv_kernel,
        grid=(b,),
        in_specs=[pl.BlockSpec((1, m, d), lambda i: (i, 0, 0)),
                  pl.BlockSpec((1, d), lambda i: (0, 0)),
                  pl.BlockSpec((d, 2 * MEM_WIDTH), lambda i: (0, 0)),
                  pl.BlockSpec((1, LANES), lambda i: (0, 0))],
        out_specs=[spec, spec],
        out_shape=[shape, shape],
        compiler_params=_params("parallel"),
    )(mem, g_mem.reshape(1, d), w_mem_kv.astype(BF16), jnp.tile(k_gain, 2).reshape(1, LANES))


def _mem_attn_kernel(flag_ref, q_ref, k_ref, v_ref, qg_ref, m_ref, o_ref):
    tq = q_ref.shape[0]
    lane = _lane_iota((tq, LANES))
    lo = lane < HEAD_DIM

    def attend(subtract_max):
        for p in range(MEM_WIDTH // LANES):
            cols = slice(p * LANES, (p + 1) * LANES)
            qn = _pair_rms(q_ref[:, cols], qg_ref[...], lo) * QK_SCALE
            accs = []
            for half in range(2):
                own = lo if half == 0 else jnp.logical_not(lo)
                shift_at = LANES - 1 if half == 0 else HEAD_DIM - 1
                qa = jnp.where(own, qn, jnp.where(lane == shift_at, -m_ref[...], 0.0)).astype(BF16)
                s = lax.dot_general(qa, k_ref[0, 2 * p + half], _NT, preferred_element_type=F32)
                if subtract_max:
                    s = s - jnp.max(s, axis=-1, keepdims=True)
                accs.append(jnp.dot(jnp.exp(s).astype(BF16), v_ref[0, 2 * p + half],
                                    preferred_element_type=F32))
            a0, a1 = accs
            o_ref[:, cols] = jnp.where(lo, a0 / a0[:, HEAD_DIM:HEAD_DIM + 1],
                                       a1 / a1[:, 0:1]).astype(o_ref.dtype)

    @pl.when(flag_ref[0] == 1)
    def _():
        attend(False)

    @pl.when(flag_ref[0] != 1)
    def _():
        attend(True)


def _mem_attn(proj, km, vm, q_gain, k_gain, batch, seq, tq):
    m = km.shape[2]
    qcol = (IN_WIDTH - MEM_WIDTH) // MEM_WIDTH
    assert qcol * MEM_WIDTH == IN_WIDTH - MEM_WIDTH
    nq = seq // tq
    shift = HEAD_DIM * QK_SCALE * jnp.max(jnp.abs(q_gain)) * jnp.max(jnp.abs(k_gain))
    flag = (2.0 * shift <= SHIFT_SAFE).astype(jnp.int32).reshape(1)
    kv_spec = pl.BlockSpec((1, MEM_HEADS, m, LANES), lambda b, i, f: (b, 0, 0, 0))
    grid_spec = pltpu.PrefetchScalarGridSpec(
        num_scalar_prefetch=1,
        grid=(batch, nq),
        in_specs=[pl.BlockSpec((tq, MEM_WIDTH), lambda b, i, f: (b * nq + i, qcol)),
                  kv_spec, kv_spec,
                  pl.BlockSpec((1, LANES), lambda b, i, f: (0, 0)),
                  pl.BlockSpec((1, LANES), lambda b, i, f: (0, 0))],
        out_specs=pl.BlockSpec((tq, MEM_WIDTH), lambda b, i, f: (b * nq + i, 0)),
    )
    return pl.pallas_call(
        _mem_attn_kernel,
        grid_spec=grid_spec,
        out_shape=jax.ShapeDtypeStruct((batch * seq, MEM_WIDTH), MIX_DTYPE),
        compiler_params=_params("parallel", "parallel"),
    )(flag, proj, km, vm, jnp.tile(q_gain, 2).reshape(1, LANES), jnp.full((1, LANES), shift, F32))


def _out_router_kernel(oa_ref, or_ref, om_ref, x_ref, wo_ref, g_ref, wr_ref, br_ref,
                       x2_ref, h2_ref, aux_ref, meta_ref, cnt_ref, run_ref):
    tm = x_ref.shape[0]

    @pl.when(pl.program_id(0) == 0)
    def _():
        run_ref[...] = jnp.zeros_like(run_ref)

    y = jnp.dot(oa_ref[...].astype(BF16), wo_ref[0:MOBA_WIDTH, :], preferred_element_type=F32)
    y += jnp.dot(or_ref[...].astype(BF16), wo_ref[MOBA_WIDTH:MOBA_WIDTH + RET_WIDTH, :],
                 preferred_element_type=F32)
    y += jnp.dot(om_ref[...].astype(BF16), wo_ref[MOBA_WIDTH + RET_WIDTH:, :],
                 preferred_element_type=F32)
    x2 = x_ref[...] + y
    x2_ref[...] = x2
    ms = jnp.mean(x2 * x2, axis=-1, keepdims=True)
    h2 = x2 * lax.rsqrt(ms + EPS) * g_ref[...]
    h2_ref[...] = _pack_bf16_pairs(h2)
    logits_t = lax.dot_general(wr_ref[...], h2.astype(BF16), _NT,
                               preferred_element_type=F32)[0:N_EXPERTS] + br_ref[...]
    row_f = lax.broadcasted_iota(jnp.int32, (N_EXPERTS, tm), 0).astype(F32)
    lg = logits_t
    vals, firsts, picks = [], [], []
    for _ in range(TOP_K):
        m = jnp.max(lg, axis=0, keepdims=True)
        first = jnp.min(jnp.where(lg == m, row_f, jnp.inf), axis=0, keepdims=True)
        pick = row_f == first
        vals.append(m)
        firsts.append(first)
        picks.append(pick)
        lg = jnp.where(pick, -jnp.inf, lg)
    exps = [jnp.exp(v - vals[0]) for v in vals]
    denom = exps[0] + exps[1] + exps[2] + exps[3]
    sel = jnp.zeros((N_EXPERTS, tm), F32)
    for pick in picks:
        sel = sel + jnp.where(pick, 1.0, 0.0)
    before = jnp.where(lax.broadcasted_iota(jnp.int32, (tm, tm), 0)
                       < lax.broadcasted_iota(jnp.int32, (tm, tm), 1), 1.0, 0.0).astype(BF16)
    run = run_ref[:, 0:1]
    pos = run + jnp.dot(sel.astype(BF16), before, preferred_element_type=F32)
    ranks = [jnp.sum(jnp.where(p, pos, 0.0), axis=0, keepdims=True) for p in picks]
    slot = lax.broadcasted_iota(jnp.int32, (4 * TOP_K, tm), 0)
    aux_t = jnp.zeros((4 * TOP_K, tm), F32)
    for k in range(TOP_K):
        aux_t = aux_t + jnp.where(slot == k, firsts[k], 0.0)
        aux_t = aux_t + jnp.where(slot == TOP_K + k, exps[k] / denom, 0.0)
        aux_t = aux_t + jnp.where(slot == 2 * TOP_K + k, ranks[k], 0.0)
    meta_ref[...] = aux_t
    aux_t = jnp.concatenate([aux_t, jnp.zeros((LANES - 4 * TOP_K, tm), F32)], axis=0)
    aux_ref[...] = aux_t.T
    run = run + jnp.sum(sel, axis=1, keepdims=True)
    run_ref[...] = jnp.broadcast_to(run, run_ref.shape)
    cnt_ref[...] = jnp.broadcast_to(run, cnt_ref.shape)


def _out_router(oa, orr, om, x2d, w_out, g_ffn, w_router, b_router, tm):
    t, d = x2d.shape
    wr_t = jnp.zeros((LANES, d), BF16).at[:N_EXPERTS, :].set(w_router.T.astype(BF16))
    br = b_router.reshape(N_EXPERTS, 1).astype(F32)
    row = lambda w: pl.BlockSpec((tm, w), lambda i: (i, 0))
    const = lambda r, c: pl.BlockSpec((r, c), lambda i: (0, 0))
    return pl.pallas_call(
        _out_router_kernel,
        grid=(t // tm,),
        in_specs=[row(MOBA_WIDTH), row(RET_WIDTH), row(MEM_WIDTH), row(d),
                  const(d, d), const(1, d), const(LANES, d), const(N_EXPERTS, 1)],
        out_specs=[row(d), row(d // 2), row(LANES),
                   pl.BlockSpec((4 * TOP_K, tm), lambda i: (0, i)), const(N_EXPERTS, LANES)],
        out_shape=[jax.ShapeDtypeStruct((t, d), F32), jax.ShapeDtypeStruct((t, d // 2), jnp.uint32),
                   jax.ShapeDtypeStruct((t, LANES), F32),
                   jax.ShapeDtypeStruct((4 * TOP_K, t), F32),
                   jax.ShapeDtypeStruct((N_EXPERTS, LANES), F32)],
        scratch_shapes=[pltpu.VMEM((N_EXPERTS, LANES), F32)],
        compiler_params=_params("arbitrary"),
    )(oa, orr, om, x2d, w_out.astype(BF16), g_ffn.reshape(1, d), wr_t, br)


MXU_COLS = 256


def _moe_kernel(first_ref, nblk_ref, cnt_ref, xs_hbm, w1_ref, b1g_ref, b1l_ref, w2_ref, b2_ref,
                ys_hbm, w1p_ref, w2b_ref, act_ref, xbuf, ybuf, xsem, ysem):
    e = pl.program_id(0)
    rb = MOE_BLOCK
    half = MXU_COLS // 2
    n_chunks = w1_ref.shape[2] // MXU_COLS
    g0 = first_ref[e]
    total = first_ref[N_EXPERTS]

    def x_copy(g, slot):
        return pltpu.make_async_copy(xs_hbm.at[pl.ds(pl.multiple_of(g * rb, rb), rb)],
                                     xbuf.at[slot], xsem.at[slot])

    def y_copy(g, slot):
        return pltpu.make_async_copy(ybuf.at[slot],
                                     ys_hbm.at[pl.ds(pl.multiple_of(g * rb, rb), rb)],
                                     ysem.at[slot])

    @pl.when(e == 0)
    def _():
        x_copy(0, 0).start()

    @pl.when(nblk_ref[e] > 0)
    def _():
        r = lax.broadcasted_iota(jnp.int32, (MXU_COLS, MXU_COLS), 0)
        c = lax.broadcasted_iota(jnp.int32, (MXU_COLS, MXU_COLS), 1)
        src = jnp.where(c < half, 2 * c, 2 * (c - half) + 1)
        perm = jnp.where(r == src, 1.0, 0.0).astype(BF16)
        for ch in range(n_chunks):
            cols = slice(ch * MXU_COLS, (ch + 1) * MXU_COLS)
            w = w1_ref[0, :, cols].astype(BF16)
            w1p_ref[:, cols] = jnp.dot(w, perm, preferred_element_type=F32).astype(BF16)
        w2b_ref[...] = w2_ref[0].astype(BF16)

    def block(b, carry):
        g = g0 + b
        slot = g % 2
        x_copy(g, slot).wait()

        @pl.when(g + 1 < total)
        def _():
            x_copy(g + 1, 1 - slot).start()

        row = lax.broadcasted_iota(jnp.int32, (rb, xbuf.shape[2]), 0)
        packed = jnp.where(row < cnt_ref[e] - b * rb, xbuf[slot], jnp.uint32(0))
        x_lo, x_hi = _unpack_bf16_pairs(packed)
        x = jnp.concatenate([x_lo.astype(BF16), x_hi.astype(BF16)], axis=1)
        for ch in range(n_chunks):
            z = jnp.dot(x, w1p_ref[:, ch * MXU_COLS:(ch + 1) * MXU_COLS], preferred_element_type=F32)
            hs = slice(ch * half, (ch + 1) * half)
            xg = jnp.minimum(z[:, :half] + b1g_ref[0][:, hs], SWIGLU_LIMIT)
            xl = jnp.clip(z[:, half:] + b1l_ref[0][:, hs], -SWIGLU_LIMIT, SWIGLU_LIMIT)
            act_ref[:, hs] = (xg * jax.nn.sigmoid(SWIGLU_ALPHA * xg) * (xl + 1.0)).astype(BF16)
        y = jnp.dot(act_ref[...], w2b_ref[...], preferred_element_type=F32) + b2_ref[0]

        @pl.when(g >= 2)
        def _():
            y_copy(g - 2, slot).wait()

        ybuf[slot] = _pack_bf16_pairs(y)
        y_copy(g, slot).start()
        return carry

    lax.fori_loop(0, nblk_ref[e], block, 0)

    @pl.when(e == pl.num_programs(0) - 1)
    def _():
        @pl.when(total >= 2)
        def _():
            y_copy(total - 2, total % 2).wait()

        y_copy(total - 1, (total - 1) % 2).wait()


def _moe_experts(xs, first_blk, n_blk, cnt, w1, b1g, b1l, w2, b2):
    p_rows = xs.shape[0]
    n_exp, d = w1.shape[0], w1.shape[1]
    d_ff = w2.shape[1]
    assert w1.shape[2] == 2 * d_ff and (2 * d_ff) % MXU_COLS == 0 and xs.shape[1] == d // 2
    wsel = lambda e, *_: (e, 0, 0)
    grid_spec = pltpu.PrefetchScalarGridSpec(
        num_scalar_prefetch=3,
        grid=(n_exp,),
        in_specs=[pl.BlockSpec(memory_space=pl.ANY),
                  pl.BlockSpec((1, d, 2 * d_ff), wsel),
                  pl.BlockSpec((1, 1, d_ff), wsel), pl.BlockSpec((1, 1, d_ff), wsel),
                  pl.BlockSpec((1, d_ff, d), wsel), pl.BlockSpec((1, 1, d), wsel)],
        out_specs=pl.BlockSpec(memory_space=pl.ANY),
        scratch_shapes=[pltpu.VMEM((d, 2 * d_ff), BF16), pltpu.VMEM((d_ff, d), BF16),
                        pltpu.VMEM((MOE_BLOCK, d_ff), BF16),
                        pltpu.VMEM((2, MOE_BLOCK, d // 2), jnp.uint32),
                        pltpu.VMEM((2, MOE_BLOCK, d // 2), jnp.uint32),
                        pltpu.SemaphoreType.DMA((2,)), pltpu.SemaphoreType.DMA((2,))],
    )
    return pl.pallas_call(
        _moe_kernel,
        grid_spec=grid_spec,
        out_shape=jax.ShapeDtypeStruct((p_rows, d // 2), jnp.uint32),
        compiler_params=pltpu.CompilerParams(dimension_semantics=("arbitrary",),
                                             vmem_limit_bytes=MOE_VMEM_LIMIT),
    )(first_blk, n_blk, cnt, xs, w1, b1g, b1l, w2, b2)


def _pack_bf16_pairs(y):
    n = y.shape[1] // 2
    lo = lax.bitcast_convert_type(y[:, :n].astype(BF16).astype(F32), jnp.uint32)
    hi = lax.bitcast_convert_type(y[:, n:].astype(BF16).astype(F32), jnp.uint32)
    return (lo >> 16) | hi


def _unpack_bf16_pairs(w):
    lo = lax.bitcast_convert_type(w << 16, F32)
    hi = lax.bitcast_convert_type(w & jnp.uint32(0xFFFF0000), F32)
    return lo, hi


def _sc_gather_rows(src, idx):
    n, w = src.shape
    m = idx.shape[0]
    workers = SC_CORES * SC_SUBCORES
    chunk = SC_GATHER_ROWS
    assert m % (workers * chunk) == 0
    per_worker = m // workers
    mesh = plsc.VectorSubcoreMesh(core_axis_name="c", subcore_axis_name="s",
                                  num_cores=SC_CORES, num_subcores=SC_SUBCORES)

    @functools.partial(
        pl.kernel, mesh=mesh, out_type=jax.ShapeDtypeStruct((m, w), src.dtype),
        scratch_types=[pltpu.VMEM((chunk,), jnp.int32), pltpu.VMEM((chunk, w), src.dtype),
                       pltpu.SemaphoreType.DMA])
    def gather(src_hbm, idx_hbm, out_hbm, idx_v, rows_v, sem):
        base = (lax.axis_index("s") * SC_CORES + lax.axis_index("c")) * per_worker

        @pl.loop(0, per_worker // chunk)
        def _(c):
            off = base + c * chunk
            pltpu.sync_copy(idx_hbm.at[pl.ds(off, chunk)], idx_v)
            pltpu.async_copy(src_hbm.at[idx_v], rows_v, sem).wait()
            pltpu.sync_copy(rows_v, out_hbm.at[pl.ds(off, chunk)])

    return gather(src, idx)


def _sc_scatter_rows(src, idx_t, n_out):
    t, w = src.shape
    fan = idx_t.shape[0]
    workers = SC_CORES * SC_SUBCORES
    chunk = SC_GATHER_ROWS
    assert t % (workers * chunk) == 0 and idx_t.shape[1] == t
    per_worker = t // workers
    mesh = plsc.VectorSubcoreMesh(core_axis_name="c", subcore_axis_name="s",
                                  num_cores=SC_CORES, num_subcores=SC_SUBCORES)

    @functools.partial(
        pl.kernel, mesh=mesh, out_type=jax.ShapeDtypeStruct((n_out, w), src.dtype),
        scratch_types=[pltpu.VMEM((fan, chunk), jnp.int32), pltpu.VMEM((chunk, w), src.dtype)])
    def scatter(src_hbm, idx_hbm, out_hbm, idx_v, rows_v):
        base = (lax.axis_index("s") * SC_CORES + lax.axis_index("c")) * per_worker

        @pl.loop(0, per_worker // chunk)
        def _(c):
            off = base + c * chunk
            pltpu.sync_copy(idx_hbm.at[:, pl.ds(off, chunk)], idx_v)
            pltpu.sync_copy(src_hbm.at[pl.ds(off, chunk)], rows_v)
            for j in range(fan):
                pltpu.sync_copy(rows_v, out_hbm.at[idx_v.at[j]])

    return scatter(src, idx_t)


def _combine_kernel(x2_ref, aux_ref, yg_ref, o_ref):
    x2 = x2_ref[...]
    aux = aux_ref[...]
    half = x2.shape[1] // 2
    out_lo, out_hi = x2[:, :half], x2[:, half:]
    for k in range(TOP_K):
        lo, hi = _unpack_bf16_pairs(yg_ref[k])
        wk = aux[:, TOP_K + k:TOP_K + k + 1]
        out_lo = out_lo + wk * lo
        out_hi = out_hi + wk * hi
    o_ref[:, :half] = out_lo
    o_ref[:, half:] = out_hi


def _combine(acc, aux, yg, first_row, tm):
    t, d = acc.shape
    tc = yg.shape[1]
    assert tc % tm == 0 and first_row % tm == 0
    off = first_row // tm
    return pl.pallas_call(
        _combine_kernel,
        grid=(tc // tm,),
        in_specs=[pl.BlockSpec((tm, d), lambda i: (off + i, 0)),
                  pl.BlockSpec((tm, LANES), lambda i: (off + i, 0)),
                  pl.BlockSpec((TOP_K, tm, d // 2), lambda i: (0, i, 0))],
        out_specs=pl.BlockSpec((tm, d), lambda i: (off + i, 0)),
        out_shape=jax.ShapeDtypeStruct((t, d), F32),
        input_output_aliases={0: 0},
        compiler_params=_params("parallel"),
    )(acc, aux, yg)


def _layer(x, mem, g_attn, w_in, moba_q_gain, moba_k_gain, ret_out_gain, g_mem, w_mem_kv,
           mem_q_gain, mem_k_gain, w_out, g_ffn, w_router, b_router, w1, b1, w2, b2):
    batch, seq, d = x.shape
    t = batch * seq
    x2d = x.reshape(t, d)
    tm = min(512, t)

    proj = _rms_proj(x2d, g_attn, w_in.astype(BF16), tm)
    oa = _moba(proj, moba_q_gain, moba_k_gain, batch, seq)
    orr = _retention(proj, ret_out_gain, batch, seq)
    km, vm = _mem_kv(mem, g_mem, w_mem_kv, mem_k_gain)
    om = _mem_attn(proj, km, vm, mem_q_gain, mem_k_gain, batch, seq, min(512, seq))
    x2, h2, aux, meta, counts = _out_router(oa, orr, om, x2d, w_out, g_ffn, w_router, b_router, tm)

    eidx = meta[0:TOP_K].astype(jnp.int32)
    rank = meta[2 * TOP_K:3 * TOP_K].astype(jnp.int32)
    cnt = counts[:, 0].astype(jnp.int32)
    n_blk = (cnt + MOE_BLOCK - 1) // MOE_BLOCK
    blk_ends = jnp.cumsum(n_blk)
    first_blk = jnp.concatenate([jnp.zeros((1,), jnp.int32), blk_ends]).astype(jnp.int32)
    pstarts = first_blk[:N_EXPERTS] * MOE_BLOCK
    dest_t = pstarts[eidx] + rank
    a = t * TOP_K
    p_rows = -(-(a + N_EXPERTS * (MOE_BLOCK - 1)) // MOE_BLOCK) * MOE_BLOCK
    xs = _sc_scatter_rows(h2, dest_t, p_rows)

    d_ff = w2.shape[1]
    b1g = b1[:, 0::2].reshape(N_EXPERTS, 1, d_ff)
    b1l = b1[:, 1::2].reshape(N_EXPERTS, 1, d_ff)
    ys = _moe_experts(xs, first_blk, n_blk.astype(jnp.int32), cnt, w1, b1g, b1l, w2,
                      b2.reshape(N_EXPERTS, 1, d))
    n_chunks = COMBINE_CHUNKS if t % (COMBINE_CHUNKS * SC_CORES * SC_SUBCORES * SC_GATHER_ROWS) == 0 else 1
    tc = t // n_chunks
    out = x2
    for c in range(n_chunks):
        idx = dest_t[:, c * tc:(c + 1) * tc].reshape(TOP_K * tc)
        yg = _sc_gather_rows(ys, idx).reshape(TOP_K, tc, d // 2)
        out = _combine(out, aux, yg, c * tc, min(256, tc))
    return out.reshape(batch, seq, d)


def kernel(x, mem, g_attn, w_in, moba_q_gain, moba_k_gain, ret_out_gain, g_mem, w_mem_kv,
           mem_q_gain, mem_k_gain, w_out, g_ffn, w_router, b_router, w1, b1, w2, b2):
    for l in range(g_attn.shape[0]):
        x = _layer(x, mem, g_attn[l], w_in[l], moba_q_gain[l], moba_k_gain[l], ret_out_gain[l],
                   g_mem[l], w_mem_kv[l], mem_q_gain[l], mem_k_gain[l], w_out[l], g_ffn[l],
                   w_router[l], b_router[l], w1[l], b1[l], w2[l], b2[l])
    return x
```

```python
import functools

import jax
import jax.numpy as jnp
from jax import lax
from jax.experimental import pallas as pl
from jax.experimental.pallas import tpu as pltpu
from jax.experimental.pallas import tpu_sc as plsc

F32 = jnp.float32
BF16 = jnp.bfloat16

LANES = 128
HEAD_DIM = 64
MOBA_HEADS = 8
RET_HEADS = 4
MEM_HEADS = 4
MOBA_WIDTH = MOBA_HEADS * HEAD_DIM
RET_WIDTH = RET_HEADS * HEAD_DIM
MEM_WIDTH = MEM_HEADS * HEAD_DIM
IN_WIDTH = 3 * MOBA_WIDTH + 4 * RET_WIDTH + MEM_WIDTH
MOBA_BLOCK = 256
MOBA_TOPK = 3
RET_CHUNK = 128
RET_ROPE_BASE = 10000.0
N_EXPERTS = 32
TOP_K = 4
SWIGLU_ALPHA = 1.702
SWIGLU_LIMIT = 7.0
MOE_BLOCK = 512
COMBINE_CHUNKS = 4
RET_UNROLL = 8
PREP_UNROLL = 4
EPS = 1e-6
NEG = -1e30
QK_SCALE = HEAD_DIM ** -0.5

VMEM_LIMIT = 48 * 1024 * 1024
MOE_VMEM_LIMIT = 56 * 1024 * 1024
MIX_DTYPE = BF16
SC_CORES = 2
SC_SUBCORES = 16
SC_GATHER_ROWS = 128

_NT = (((1,), (1,)), ((), ()))
_TN = (((0,), (0,)), ((), ()))


def _params(*sem):
    return pltpu.CompilerParams(dimension_semantics=sem, vmem_limit_bytes=VMEM_LIMIT)


def _lane_iota(shape):
    return lax.broadcasted_iota(jnp.int32, shape, len(shape) - 1)


def _pair_rms(t, gain, lo):
    t2 = t * t
    s0 = jnp.sum(jnp.where(lo, t2, 0.0), axis=-1, keepdims=True)
    s1 = jnp.sum(jnp.where(lo, 0.0, t2), axis=-1, keepdims=True)
    r = jnp.where(lo, lax.rsqrt(s0 / HEAD_DIM + EPS), lax.rsqrt(s1 / HEAD_DIM + EPS))
    return t * r * gain


def _rms_proj_kernel(x_ref, g_ref, w_ref, o_ref):
    x = x_ref[...]
    ms = jnp.mean(x * x, axis=-1, keepdims=True)
    h = (x * lax.rsqrt(ms + EPS) * g_ref[...]).astype(BF16)
    o_ref[...] = jnp.dot(h, w_ref[...], preferred_element_type=F32)


def _rms_proj(x2d, gain, w, tm):
    t, d = x2d.shape
    n = w.shape[1]
    return pl.pallas_call(
        _rms_proj_kernel,
        grid=(t // tm,),
        in_specs=[pl.BlockSpec((tm, d), lambda i: (i, 0)),
                  pl.BlockSpec((1, d), lambda i: (0, 0)),
                  pl.BlockSpec((d, n), lambda i: (0, 0))],
        out_specs=pl.BlockSpec((tm, n), lambda i: (i, 0)),
        out_shape=jax.ShapeDtypeStruct((t, n), F32),
        compiler_params=_params("parallel"),
    )(x2d, gain.reshape(1, d), w)


SHIFT_SAFE = 80.0
MOBA_GROUP = 4


def _moba_kernel(tab_ref, q_ref, k_ref, v_ref, qg_ref, kg_ref, m_ref, o_ref,
                 ka_ref, va_ref, qa_ref, gt_ref, acc_ref, pa_ref, pb_ref, *, n_blk, n_trips):
    blk = MOBA_BLOCK
    grp = MOBA_GROUP
    nb8 = -(-n_blk // 8) * 8
    lane = _lane_iota((blk, LANES))
    lo = lane < HEAD_DIM
    lane1 = _lane_iota((1, LANES))
    qg = qg_ref[...]
    kg = kg_ref[...]
    tri = (lax.broadcasted_iota(jnp.int32, (blk, blk), 0)
           >= lax.broadcasted_iota(jnp.int32, (blk, blk), 1))
    row_t = lax.broadcasted_iota(jnp.int32, (nb8, blk), 0)
    row_tf = row_t.astype(F32)
    fill_row = lax.broadcasted_iota(jnp.int32, (HEAD_DIM - nb8, blk), 0)
    filler = jnp.where(fill_row == HEAD_DIM - nb8 - 1, -m_ref[...], 0.0)

    def rows_of(j):
        return pl.ds(pl.multiple_of(j * blk, blk), blk)

    gt_ref[...] = jnp.zeros_like(gt_ref)

    def prep_k(j, carry):
        rows = rows_of(j)
        kn = _pair_rms(k_ref[rows, :], kg, lo)
        km = jnp.mean(kn, axis=0, keepdims=True)
        gt_ref[pl.ds(HEAD_DIM + j, 1), :] = jnp.where(lane1 < HEAD_DIM, km, 0.0)
        gt_ref[pl.ds(j, 1), :] = jnp.where(lane1 < HEAD_DIM, 0.0, km)
        tag0 = jnp.where(jnp.logical_or(lane == HEAD_DIM + j, lane == LANES - 1), 1.0, 0.0)
        tag1 = jnp.where(jnp.logical_or(lane == j, lane == HEAD_DIM - 1), 1.0, 0.0)
        ka_ref[0, rows, :] = jnp.where(lo, kn, tag0).astype(BF16)
        ka_ref[1, rows, :] = jnp.where(lo, tag1, kn).astype(BF16)
        v = v_ref[rows, :]
        va_ref[0, rows, :] = jnp.where(lo, v, jnp.where(lane == HEAD_DIM, 1.0, 0.0)).astype(BF16)
        va_ref[1, rows, :] = jnp.where(lo, jnp.where(lane == 0, 1.0, 0.0), v).astype(BF16)
        return carry

    lax.fori_loop(0, n_blk, prep_k, 0, unroll=PREP_UNROLL)

    def block_bias(g, n):
        valid = row_t < n
        g = jnp.where(valid, g, NEG)
        sel = jnp.zeros((nb8, blk), jnp.bool_)
        for _ in range(MOBA_TOPK):
            m = jnp.max(g, axis=0, keepdims=True)
            first = jnp.min(jnp.where(g == m, row_tf, jnp.inf), axis=0, keepdims=True)
            pick = row_tf == first
            sel = jnp.logical_or(sel, pick)
            g = jnp.where(pick, -jnp.inf, g)
        keep = jnp.logical_or(jnp.logical_and(sel, valid), row_t == n)
        return jnp.where(keep, 0.0, NEG)

    def prep_q(n, carry):
        rows_n = rows_of(n)
        qn = _pair_rms(q_ref[rows_n, :], qg, lo)
        gate_t = lax.dot_general(gt_ref[...].astype(BF16), qn.astype(BF16), _NT,
                                 preferred_element_type=F32)
        bias_t = jnp.concatenate([block_bias(gate_t[0:nb8], n), filler,
                                  block_bias(gate_t[HEAD_DIM:HEAD_DIM + nb8], n), filler], axis=0)
        bias = bias_t.T
        qs = qn * QK_SCALE
        qa_ref[0, rows_n, :] = jnp.where(lo, qs, bias).astype(BF16)
        qa_ref[1, rows_n, :] = jnp.where(lo, bias, qs).astype(BF16)
        return carry

    lax.fori_loop(0, n_blk, prep_q, 0, unroll=PREP_UNROLL)

    def scores(h, n, j):
        s = lax.dot_general(qa_ref[h, rows_of(n), :], ka_ref[h, rows_of(j), :], _NT,
                            preferred_element_type=F32)
        return jnp.where(jnp.logical_or(tri, j != n), s, NEG)

    def normalise(n, a0, a1):
        o_ref[rows_of(n), :] = jnp.where(lo, a0 / a0[:, HEAD_DIM:HEAD_DIM + 1],
                                         a1 / a1[:, 0:1]).astype(o_ref.dtype)

    @pl.when(tab_ref[0, 0] == 1)
    def _():
        acc_ref[...] = jnp.zeros_like(acc_ref)

        def make_probs(trip, dst_ref):
            for g in range(grp):
                n = tab_ref[1, trip * grp + g]
                j = tab_ref[2, trip * grp + g]
                for h in range(2):
                    dst_ref[g, h] = jnp.exp(scores(h, n, j)).astype(BF16)

        def apply_probs(trip, src_ref):
            for g in range(grp):
                n = tab_ref[1, trip * grp + g]
                j = tab_ref[2, trip * grp + g]
                for h in range(2):
                    acc_ref[h, rows_of(n), :] += jnp.dot(src_ref[g, h], va_ref[h, rows_of(j), :],
                                                         preferred_element_type=F32)

        make_probs(0, pa_ref)

        def two_trips(i, carry):
            apply_probs(2 * i, pa_ref)
            make_probs(2 * i + 1, pb_ref)
            apply_probs(2 * i + 1, pb_ref)
            make_probs(2 * i + 2, pa_ref)
            return carry

        lax.fori_loop(0, n_trips // 2, two_trips, 0)

        def fin(n, carry):
            normalise(n, acc_ref[0, rows_of(n), :], acc_ref[1, rows_of(n), :])
            return carry

        lax.fori_loop(0, n_blk, fin, 0, unroll=PREP_UNROLL)

    @pl.when(tab_ref[0, 0] != 1)
    def _():
        def qblock(n, carry):
            def one(j, st):
                out = []
                for h in range(2):
                    m, acc = st[2 * h], st[2 * h + 1]
                    s = scores(h, n, j)
                    m_new = jnp.maximum(m, jnp.max(s, axis=-1, keepdims=True))
                    p = jnp.exp(s - m_new).astype(BF16)
                    acc = jnp.exp(m - m_new) * acc + jnp.dot(p, va_ref[h, rows_of(j), :],
                                                             preferred_element_type=F32)
                    out += [m_new, acc]
                return tuple(out)

            zero = jnp.zeros((blk, LANES), F32)
            ninf = jnp.full((blk, 1), -jnp.inf, F32)
            st = lax.fori_loop(0, n + 1, one, (ninf, zero, ninf, zero))
            normalise(n, st[1], st[3])
            return carry

        lax.fori_loop(0, n_blk, qblock, 0)


def _moba_tiles(n_blk):
    tiles = [(n, j) for n in range(n_blk) for j in range(n + 1)]
    per_two = 2 * MOBA_GROUP
    n_trips = 2 * (-(-len(tiles) // per_two))
    tiles += [(0, 1)] * ((n_trips + 1) * MOBA_GROUP - len(tiles))
    return n_trips, tiles


def _moba(proj, q_gain, k_gain, batch, seq):
    n_blk = seq // MOBA_BLOCK
    assert seq % MOBA_BLOCK == 0 and MOBA_TOPK <= n_blk <= HEAD_DIM - 8
    n_pairs = MOBA_WIDTH // LANES
    qg = jnp.tile(q_gain, 2).reshape(1, LANES)
    kg = jnp.tile(k_gain, 2).reshape(1, LANES)
    shift = HEAD_DIM * QK_SCALE * jnp.max(jnp.abs(q_gain)) * jnp.max(jnp.abs(k_gain))
    flag = (2.0 * shift <= SHIFT_SAFE).astype(jnp.int32)
    n_trips, tiles = _moba_tiles(n_blk)
    tab = jnp.stack([jnp.full((len(tiles),), flag, jnp.int32),
                     jnp.asarray([t[0] for t in tiles], jnp.int32),
                     jnp.asarray([t[1] for t in tiles], jnp.int32)])
    mrow = jnp.full((1, MOBA_BLOCK), shift, F32)
    blk = lambda off: pl.BlockSpec((seq, LANES), lambda b, p, f: (b, off + p))
    vec = lambda w: pl.BlockSpec((1, w), lambda b, p, f: (0, 0))
    prob = pltpu.VMEM((MOBA_GROUP, 2, MOBA_BLOCK, MOBA_BLOCK), BF16)
    grid_spec = pltpu.PrefetchScalarGridSpec(
        num_scalar_prefetch=1,
        grid=(batch, n_pairs),
        in_specs=[blk(0), blk(n_pairs), blk(2 * n_pairs), vec(LANES), vec(LANES), vec(MOBA_BLOCK)],
        out_specs=pl.BlockSpec((seq, LANES), lambda b, p, f: (b, p)),
        scratch_shapes=[pltpu.VMEM((2, seq, LANES), BF16),
                        pltpu.VMEM((2, seq, LANES), BF16),
                        pltpu.VMEM((2, seq, LANES), BF16),
                        pltpu.VMEM((LANES, LANES), F32),
                        pltpu.VMEM((2, seq, LANES), F32),
                        prob, prob],
    )
    return pl.pallas_call(
        functools.partial(_moba_kernel, n_blk=n_blk, n_trips=n_trips),
        grid_spec=grid_spec,
        out_shape=jax.ShapeDtypeStruct((batch * seq, MOBA_WIDTH), MIX_DTYPE),
        compiler_params=_params("parallel", "parallel"),
    )(tab, proj, proj, proj, qg, kg, mrow)


def _ret_kernel(q_ref, k_ref, v_ref, g_ref, cos_ref, sin_ref, intra_ref, kdec_ref, qdec_ref,
                cdm_ref, mask_ref, gain_ref, o_ref, *, n_ch):
    c = RET_CHUNK
    lane = _lane_iota((c, LANES))
    lo = lane < HEAD_DIM
    first_half = (lane % HEAD_DIM) < (HEAD_DIM // 2)
    gain = gain_ref[...]
    intra0 = intra_ref[0, 0]
    intra1 = intra_ref[0, 1]
    kdec = kdec_ref[0]
    qdec = qdec_ref[0]
    cdm = cdm_ref[0]
    mask = mask_ref[...]

    def rope(t, cos_t, sin_t):
        partner = jnp.where(first_half, pltpu.roll(t, LANES - HEAD_DIM // 2, 1),
                            pltpu.roll(t, HEAD_DIM // 2, 1))
        return t * cos_t + partner * sin_t

    def chunk(i, state):
        rows = pl.ds(pl.multiple_of(i * c, c), c)
        cos_t = cos_ref[rows, :]
        sin_t = sin_ref[rows, :]
        q = rope(q_ref[rows, :], cos_t, sin_t)
        k = rope(k_ref[rows, :], cos_t, sin_t) * QK_SCALE
        v = v_ref[rows, :]
        kb = k.astype(BF16)
        s0 = lax.dot_general(jnp.where(lo, q, 0.0).astype(BF16), kb, _NT,
                             preferred_element_type=F32) * intra0
        s1 = lax.dot_general(jnp.where(lo, 0.0, q).astype(BF16), kb, _NT,
                             preferred_element_type=F32) * intra1
        sc = jnp.concatenate([s0, s1], axis=1).astype(BF16)
        vv = jnp.concatenate([jnp.where(lo, v, 0.0), jnp.where(lo, 0.0, v)], axis=0).astype(BF16)
        o = jnp.dot(sc, vv, preferred_element_type=F32)
        o = o + jnp.dot(q.astype(BF16), state.astype(BF16), preferred_element_type=F32) * qdec
        kd = (k * kdec).astype(BF16)
        kv = lax.dot_general(kd, v.astype(BF16), _TN, preferred_element_type=F32)
        state = cdm * state + mask * kv
        o2 = o * o
        m0 = jnp.sum(jnp.where(lo, o2, 0.0), axis=-1, keepdims=True)
        m1 = jnp.sum(jnp.where(lo, 0.0, o2), axis=-1, keepdims=True)
        r = jnp.where(lo, lax.rsqrt(m0 / HEAD_DIM + EPS), lax.rsqrt(m1 / HEAD_DIM + EPS))
        gt = g_ref[rows, :]
        o_ref[rows, :] = ((gt * jax.nn.sigmoid(gt)) * (o * r * gain)).astype(o_ref.dtype)
        return state

    lax.fori_loop(0, n_ch, chunk, jnp.zeros((LANES, LANES), F32), unroll=RET_UNROLL)


def _ret_tables(seq):
    c = RET_CHUNK
    half = HEAD_DIM // 2
    inv_freq = 1.0 / (RET_ROPE_BASE ** jnp.linspace(0.0, 1.0, half, dtype=F32))
    ang = jnp.arange(seq, dtype=F32)[:, None] * inv_freq[None, :]
    cos, sin = jnp.cos(ang), jnp.sin(ang)
    cos_t = jnp.tile(cos, (1, LANES // half))
    sin_t = jnp.tile(jnp.concatenate([-sin, sin], axis=1), (1, LANES // HEAD_DIM))
    log_gamma = jnp.log1p(-jnp.exp2(-5.0 - jnp.arange(RET_HEADS, dtype=F32)))
    idx = jnp.arange(c, dtype=F32)
    diff = idx[:, None] - idx[None, :]
    intra = jnp.where(diff >= 0, jnp.exp(log_gamma[:, None, None] * jnp.maximum(diff, 0.0)), 0.0)
    k_decay = jnp.exp(log_gamma[:, None] * (c - 1.0 - idx)[None, :])
    q_decay = jnp.exp(log_gamma[:, None] * (idx + 1.0)[None, :])
    chunk_decay = jnp.exp(log_gamma * c)
    n_pairs = RET_HEADS // 2

    def lanes(t):
        return jnp.repeat(t.reshape(n_pairs, 2, c).transpose(0, 2, 1), HEAD_DIM, axis=2)

    blockdiag = (jnp.arange(LANES)[:, None] // HEAD_DIM) == (jnp.arange(LANES)[None, :] // HEAD_DIM)
    mask = blockdiag.astype(F32)
    cdm = jnp.repeat(chunk_decay.reshape(n_pairs, 2), HEAD_DIM, axis=1)[:, :, None] * mask[None]
    return (cos_t, sin_t, intra.reshape(n_pairs, 2, c, c), lanes(k_decay), lanes(q_decay), cdm, mask)


def _retention(proj, out_gain, batch, seq):
    c = RET_CHUNK
    assert seq % c == 0
    n_pairs = RET_HEADS // 2
    base = 3 * MOBA_WIDTH // LANES
    cos_t, sin_t, intra, kdec, qdec, cdm, mask = _ret_tables(seq)
    blk = lambda off: pl.BlockSpec((seq, LANES), lambda b, p: (b, base + off + p))
    tab = pl.BlockSpec((seq, LANES), lambda b, p: (0, 0))
    return pl.pallas_call(
        functools.partial(_ret_kernel, n_ch=seq // c),
        grid=(batch, n_pairs),
        in_specs=[blk(0), blk(n_pairs), blk(2 * n_pairs), blk(3 * n_pairs), tab, tab,
                  pl.BlockSpec((1, 2, c, c), lambda b, p: (p, 0, 0, 0)),
                  pl.BlockSpec((1, c, LANES), lambda b, p: (p, 0, 0)),
                  pl.BlockSpec((1, c, LANES), lambda b, p: (p, 0, 0)),
                  pl.BlockSpec((1, LANES, LANES), lambda b, p: (p, 0, 0)),
                  pl.BlockSpec((LANES, LANES), lambda b, p: (0, 0)),
                  pl.BlockSpec((1, LANES), lambda b, p: (0, p))],
        out_specs=pl.BlockSpec((seq, LANES), lambda b, p: (b, p)),
        out_shape=jax.ShapeDtypeStruct((batch * seq, RET_WIDTH), MIX_DTYPE),
        compiler_params=_params("parallel", "parallel"),
    )(proj, proj, proj, proj, cos_t, sin_t, intra, kdec, qdec, cdm, mask,
      out_gain.reshape(1, RET_WIDTH))


def _head_tags(lane, half, k_side):
    if k_side:
        one_at = LANES - 1 if half == 0 else HEAD_DIM - 1
    else:
        one_at = HEAD_DIM if half == 0 else 0
    return jnp.where(lane == one_at, 1.0, 0.0)


def _mem_kv_kernel(m_ref, g_ref, w_ref, kg_ref, k_ref, v_ref):
    x = m_ref[0]
    ms = jnp.mean(x * x, axis=-1, keepdims=True)
    h = (x * lax.rsqrt(ms + EPS) * g_ref[...]).astype(BF16)
    kv = jnp.dot(h, w_ref[...], preferred_element_type=F32)
    lane = _lane_iota((x.shape[0], LANES))
    lo = lane < HEAD_DIM
    for p in range(MEM_WIDTH // LANES):
        kt = _pair_rms(kv[:, p * LANES:(p + 1) * LANES], kg_ref[...], lo)
        vt = kv[:, MEM_WIDTH + p * LANES:MEM_WIDTH + (p + 1) * LANES]
        for half in range(2):
            own = lo if half == 0 else jnp.logical_not(lo)
            k_ref[0, 2 * p + half] = jnp.where(own, kt, _head_tags(lane, half, True)).astype(BF16)
            v_ref[0, 2 * p + half] = jnp.where(own, vt, _head_tags(lane, half, False)).astype(BF16)


def _mem_kv(mem, g_mem, w_mem_kv, k_gain):
    b, m, d = mem.shape
    spec = pl.BlockSpec((1, MEM_HEADS, m, LANES), lambda i: (i, 0, 0, 0))
    shape = jax.ShapeDtypeStruct((b, MEM_HEADS, m, LANES), BF16)
    return pl.pallas_call(
        _mem_kv_kernel,
        grid=(b,),
        in_specs=[pl.BlockSpec((1, m, d), lambda i: (i, 0, 0)),
                  pl.BlockSpec((1, d), lambda i: (0, 0)),
                  pl.BlockSpec((d, 2 * MEM_WIDTH), lambda i: (0, 0)),
                  pl.BlockSpec((1, LANES), lambda i: (0, 0))],
        out_specs=[spec, spec],
        out_shape=[shape, shape],
        compiler_params=_params("parallel"),
    )(mem, g_mem.reshape(1, d), w_mem_kv.astype(BF16), jnp.tile(k_gain, 2).reshape(1, LANES))


def _mem_attn_kernel(flag_ref, q_ref, k_ref, v_ref, qg_ref, m_ref, o_ref):
    tq = q_ref.shape[0]
    lane = _lane_iota((tq, LANES))
    lo = lane < HEAD_DIM

    def attend(subtract_max):
        for p in range(MEM_WIDTH // LANES):
            cols = slice(p * LANES, (p + 1) * LANES)
            qn = _pair_rms(q_ref[:, cols], qg_ref[...], lo) * QK_SCALE
            accs = []
            for half in range(2):
                own = lo if half == 0 else jnp.logical_not(lo)
                shift_at = LANES - 1 if half == 0 else HEAD_DIM - 1
                qa = jnp.where(own, qn, jnp.where(lane == shift_at, -m_ref[...], 0.0)).astype(BF16)
                s = lax.dot_general(qa, k_ref[0, 2 * p + half], _NT, preferred_element_type=F32)
                if subtract_max:
                    s = s - jnp.max(s, axis=-1, keepdims=True)
                accs.append(jnp.dot(jnp.exp(s).astype(BF16), v_ref[0, 2 * p + half],
                                    preferred_element_type=F32))
            a0, a1 = accs
            o_ref[:, cols] = jnp.where(lo, a0 / a0[:, HEAD_DIM:HEAD_DIM + 1],
                                       a1 / a1[:, 0:1]).astype(o_ref.dtype)

    @pl.when(flag_ref[0] == 1)
    def _():
        attend(False)

    @pl.when(flag_ref[0] != 1)
    def _():
        attend(True)


def _mem_attn(proj, km, vm, q_gain, k_gain, batch, seq, tq):
    m = km.shape[2]
    qcol = (IN_WIDTH - MEM_WIDTH) // MEM_WIDTH
    assert qcol * MEM_WIDTH == IN_WIDTH - MEM_WIDTH
    nq = seq // tq
    shift = HEAD_DIM * QK_SCALE * jnp.max(jnp.abs(q_gain)) * jnp.max(jnp.abs(k_gain))
    flag = (2.0 * shift <= SHIFT_SAFE).astype(jnp.int32).reshape(1)
    kv_spec = pl.BlockSpec((1, MEM_HEADS, m, LANES), lambda b, i, f: (b, 0, 0, 0))
    grid_spec = pltpu.PrefetchScalarGridSpec(
        num_scalar_prefetch=1,
        grid=(batch, nq),
        in_specs=[pl.BlockSpec((tq, MEM_WIDTH), lambda b, i, f: (b * nq + i, qcol)),
                  kv_spec, kv_spec,
                  pl.BlockSpec((1, LANES), lambda b, i, f: (0, 0)),
                  pl.BlockSpec((1, LANES), lambda b, i, f: (0, 0))],
        out_specs=pl.BlockSpec((tq, MEM_WIDTH), lambda b, i, f: (b * nq + i, 0)),
    )
    return pl.pallas_call(
        _mem_attn_kernel,
        grid_spec=grid_spec,
        out_shape=jax.ShapeDtypeStruct((batch * seq, MEM_WIDTH), MIX_DTYPE),
        compiler_params=_params("parallel", "parallel"),
    )(flag, proj, km, vm, jnp.tile(q_gain, 2).reshape(1, LANES), jnp.full((1, LANES), shift, F32))


def _out_router_kernel(oa_ref, or_ref, om_ref, x_ref, wo_ref, g_ref, wr_ref, br_ref,
                       x2_ref, h2_ref, aux_ref, meta_ref, cnt_ref, run_ref):
    tm = x_ref.shape[0]

    @pl.when(pl.program_id(0) == 0)
    def _():
        run_ref[...] = jnp.zeros_like(run_ref)

    y = jnp.dot(oa_ref[...].astype(BF16), wo_ref[0:MOBA_WIDTH, :], preferred_element_type=F32)
    y += jnp.dot(or_ref[...].astype(BF16), wo_ref[MOBA_WIDTH:MOBA_WIDTH + RET_WIDTH, :],
                 preferred_element_type=F32)
    y += jnp.dot(om_ref[...].astype(BF16), wo_ref[MOBA_WIDTH + RET_WIDTH:, :],
                 preferred_element_type=F32)
    x2 = x_ref[...] + y
    x2_ref[...] = x2
    ms = jnp.mean(x2 * x2, axis=-1, keepdims=True)
    h2 = x2 * lax.rsqrt(ms + EPS) * g_ref[...]
    h2_ref[...] = _pack_bf16_pairs(h2)
    logits_t = lax.dot_general(wr_ref[...], h2.astype(BF16), _NT,
                               preferred_element_type=F32)[0:N_EXPERTS] + br_ref[...]
    row_f = lax.broadcasted_iota(jnp.int32, (N_EXPERTS, tm), 0).astype(F32)
    lg = logits_t
    vals, firsts, picks = [], [], []
    for _ in range(TOP_K):
        m = jnp.max(lg, axis=0, keepdims=True)
        first = jnp.min(jnp.where(lg == m, row_f, jnp.inf), axis=0, keepdims=True)
        pick = row_f == first
        vals.append(m)
        firsts.append(first)
        picks.append(pick)
        lg = jnp.where(pick, -jnp.inf, lg)
    exps = [jnp.exp(v - vals[0]) for v in vals]
    denom = exps[0] + exps[1] + exps[2] + exps[3]
    sel = jnp.zeros((N_EXPERTS, tm), F32)
    for pick in picks:
        sel = sel + jnp.where(pick, 1.0, 0.0)
    before = jnp.where(lax.broadcasted_iota(jnp.int32, (tm, tm), 0)
                       < lax.broadcasted_iota(jnp.int32, (tm, tm), 1), 1.0, 0.0).astype(BF16)
    run = run_ref[:, 0:1]
    pos = run + jnp.dot(sel.astype(BF16), before, preferred_element_type=F32)
    ranks = [jnp.sum(jnp.where(p, pos, 0.0), axis=0, keepdims=True) for p in picks]
    slot = lax.broadcasted_iota(jnp.int32, (4 * TOP_K, tm), 0)
    aux_t = jnp.zeros((4 * TOP_K, tm), F32)
    for k in range(TOP_K):
        aux_t = aux_t + jnp.where(slot == k, firsts[k], 0.0)
        aux_t = aux_t + jnp.where(slot == TOP_K + k, exps[k] / denom, 0.0)
        aux_t = aux_t + jnp.where(slot == 2 * TOP_K + k, ranks[k], 0.0)
    meta_ref[...] = aux_t
    aux_t = jnp.concatenate([aux_t, jnp.zeros((LANES - 4 * TOP_K, tm), F32)], axis=0)
    aux_ref[...] = aux_t.T
    run = run + jnp.sum(sel, axis=1, keepdims=True)
    run_ref[...] = jnp.broadcast_to(run, run_ref.shape)
    cnt_ref[...] = jnp.broadcast_to(run, cnt_ref.shape)


def _out_router(oa, orr, om, x2d, w_out, g_ffn, w_router, b_router, tm):
    t, d = x2d.shape
    wr_t = jnp.zeros((LANES, d), BF16).at[:N_EXPERTS, :].set(w_router.T.astype(BF16))
    br = b_router.reshape(N_EXPERTS, 1).astype(F32)
    row = lambda w: pl.BlockSpec((tm, w), lambda i: (i, 0))
    const = lambda r, c: pl.BlockSpec((r, c), lambda i: (0, 0))
    return pl.pallas_call(
        _out_router_kernel,
        grid=(t // tm,),
        in_specs=[row(MOBA_WIDTH), row(RET_WIDTH), row(MEM_WIDTH), row(d),
                  const(d, d), const(1, d), const(LANES, d), const(N_EXPERTS, 1)],
        out_specs=[row(d), row(d // 2), row(LANES),
                   pl.BlockSpec((4 * TOP_K, tm), lambda i: (0, i)), const(N_EXPERTS, LANES)],
        out_shape=[jax.ShapeDtypeStruct((t, d), F32), jax.ShapeDtypeStruct((t, d // 2), jnp.uint32),
                   jax.ShapeDtypeStruct((t, LANES), F32),
                   jax.ShapeDtypeStruct((4 * TOP_K, t), F32),
                   jax.ShapeDtypeStruct((N_EXPERTS, LANES), F32)],
        scratch_shapes=[pltpu.VMEM((N_EXPERTS, LANES), F32)],
        compiler_params=_params("arbitrary"),
    )(oa, orr, om, x2d, w_out.astype(BF16), g_ffn.reshape(1, d), wr_t, br)


MXU_COLS = 256


def _moe_kernel(first_ref, nblk_ref, cnt_ref, xs_hbm, w1_ref, b1g_ref, b1l_ref, w2_ref, b2_ref,
                ys_hbm, w1p_ref, w2b_ref, act_ref, xbuf, ybuf, xsem, ysem):
    e = pl.program_id(0)
    rb = MOE_BLOCK
    half = MXU_COLS // 2
    n_chunks = w1_ref.shape[2] // MXU_COLS
    g0 = first_ref[e]
    total = first_ref[N_EXPERTS]

    def x_copy(g, slot):
        return pltpu.make_async_copy(xs_hbm.at[pl.ds(pl.multiple_of(g * rb, rb), rb)],
                                     xbuf.at[slot], xsem.at[slot])

    def y_copy(g, slot):
        return pltpu.make_async_copy(ybuf.at[slot],
                                     ys_hbm.at[pl.ds(pl.multiple_of(g * rb, rb), rb)],
                                     ysem.at[slot])

    @pl.when(e == 0)
    def _():
        x_copy(0, 0).start()

    @pl.when(nblk_ref[e] > 0)
    def _():
        r = lax.broadcasted_iota(jnp.int32, (MXU_COLS, MXU_COLS), 0)
        c = lax.broadcasted_iota(jnp.int32, (MXU_COLS, MXU_COLS), 1)
        src = jnp.where(c < half, 2 * c, 2 * (c - half) + 1)
        perm = jnp.where(r == src, 1.0, 0.0).astype(BF16)
        for ch in range(n_chunks):
            cols = slice(ch * MXU_COLS, (ch + 1) * MXU_COLS)
            w = w1_ref[0, :, cols].astype(BF16)
            w1p_ref[:, cols] = jnp.dot(w, perm, preferred_element_type=F32).astype(BF16)
        w2b_ref[...] = w2_ref[0].astype(BF16)

    def block(b, carry):
        g = g0 + b
        slot = g % 2
        x_copy(g, slot).wait()

        @pl.when(g + 1 < total)
        def _():
            x_copy(g + 1, 1 - slot).start()

        row = lax.broadcasted_iota(jnp.int32, (rb, xbuf.shape[2]), 0)
        packed = jnp.where(row < cnt_ref[e] - b * rb, xbuf[slot], jnp.uint32(0))
        x_lo, x_hi = _unpack_bf16_pairs(packed)
        x = jnp.concatenate([x_lo.astype(BF16), x_hi.astype(BF16)], axis=1)
        for ch in range(n_chunks):
            z = jnp.dot(x, w1p_ref[:, ch * MXU_COLS:(ch + 1) * MXU_COLS], preferred_element_type=F32)
            hs = slice(ch * half, (ch + 1) * half)
            xg = jnp.minimum(z[:, :half] + b1g_ref[0][:, hs], SWIGLU_LIMIT)
            xl = jnp.clip(z[:, half:] + b1l_ref[0][:, hs], -SWIGLU_LIMIT, SWIGLU_LIMIT)
            act_ref[:, hs] = (xg * jax.nn.sigmoid(SWIGLU_ALPHA * xg) * (xl + 1.0)).astype(BF16)
        y = jnp.dot(act_ref[...], w2b_ref[...], preferred_element_type=F32) + b2_ref[0]

        @pl.when(g >= 2)
        def _():
            y_copy(g - 2, slot).wait()

        ybuf[slot] = _pack_bf16_pairs(y)
        y_copy(g, slot).start()
        return carry

    lax.fori_loop(0, nblk_ref[e], block, 0)

    @pl.when(e == pl.num_programs(0) - 1)
    def _():
        @pl.when(total >= 2)
        def _():
            y_copy(total - 2, total % 2).wait()

        y_copy(total - 1, (total - 1) % 2).wait()


def _moe_experts(xs, first_blk, n_blk, cnt, w1, b1g, b1l, w2, b2):
    p_rows = xs.shape[0]
    n_exp, d = w1.shape[0], w1.shape[1]
    d_ff = w2.shape[1]
    assert w1.shape[2] == 2 * d_ff and (2 * d_ff) % MXU_COLS == 0 and xs.shape[1] == d // 2
    wsel = lambda e, *_: (e, 0, 0)
    grid_spec = pltpu.PrefetchScalarGridSpec(
        num_scalar_prefetch=3,
        grid=(n_exp,),
        in_specs=[pl.BlockSpec(memory_space=pl.ANY),
                  pl.BlockSpec((1, d, 2 * d_ff), wsel),
                  pl.BlockSpec((1, 1, d_ff), wsel), pl.BlockSpec((1, 1, d_ff), wsel),
                  pl.BlockSpec((1, d_ff, d), wsel), pl.BlockSpec((1, 1, d), wsel)],
        out_specs=pl.BlockSpec(memory_space=pl.ANY),
        scratch_shapes=[pltpu.VMEM((d, 2 * d_ff), BF16), pltpu.VMEM((d_ff, d), BF16),
                        pltpu.VMEM((MOE_BLOCK, d_ff), BF16),
                        pltpu.VMEM((2, MOE_BLOCK, d // 2), jnp.uint32),
                        pltpu.VMEM((2, MOE_BLOCK, d // 2), jnp.uint32),
                        pltpu.SemaphoreType.DMA((2,)), pltpu.SemaphoreType.DMA((2,))],
    )
    return pl.pallas_call(
        _moe_kernel,
        grid_spec=grid_spec,
        out_shape=jax.ShapeDtypeStruct((p_rows, d // 2), jnp.uint32),
        compiler_params=pltpu.CompilerParams(dimension_semantics=("arbitrary",),
                                             vmem_limit_bytes=MOE_VMEM_LIMIT),
    )(first_blk, n_blk, cnt, xs, w1, b1g, b1l, w2, b2)


def _pack_bf16_pairs(y):
    n = y.shape[1] // 2
    lo = lax.bitcast_convert_type(y[:, :n].astype(BF16).astype(F32), jnp.uint32)
    hi = lax.bitcast_convert_type(y[:, n:].astype(BF16).astype(F32), jnp.uint32)
    return (lo >> 16) | hi


def _unpack_bf16_pairs(w):
    lo = lax.bitcast_convert_type(w << 16, F32)
    hi = lax.bitcast_convert_type(w & jnp.uint32(0xFFFF0000), F32)
    return lo, hi


def _sc_gather_rows(src, idx):
    n, w = src.shape
    m = idx.shape[0]
    workers = SC_CORES * SC_SUBCORES
    chunk = SC_GATHER_ROWS
    assert m % (workers * chunk) == 0
    per_worker = m // workers
    mesh = plsc.VectorSubcoreMesh(core_axis_name="c", subcore_axis_name="s",
                                  num_cores=SC_CORES, num_subcores=SC_SUBCORES)

    @functools.partial(
        pl.kernel, mesh=mesh, out_type=jax.ShapeDtypeStruct((m, w), src.dtype),
        scratch_types=[pltpu.VMEM((chunk,), jnp.int32), pltpu.VMEM((chunk, w), src.dtype),
                       pltpu.SemaphoreType.DMA])
    def gather(src_hbm, idx_hbm, out_hbm, idx_v, rows_v, sem):
        base = (lax.axis_index("s") * SC_CORES + lax.axis_index("c")) * per_worker

        @pl.loop(0, per_worker // chunk)
        def _(c):
            off = base + c * chunk
            pltpu.sync_copy(idx_hbm.at[pl.ds(off, chunk)], idx_v)
            pltpu.async_copy(src_hbm.at[idx_v], rows_v, sem).wait()
            pltpu.sync_copy(rows_v, out_hbm.at[pl.ds(off, chunk)])

    return gather(src, idx)


def _sc_scatter_rows(src, idx_t, n_out):
    t, w = src.shape
    fan = idx_t.shape[0]
    workers = SC_CORES * SC_SUBCORES
    chunk = SC_GATHER_ROWS
    assert t % (workers * chunk) == 0 and idx_t.shape[1] == t
    per_worker = t // workers
    mesh = plsc.VectorSubcoreMesh(core_axis_name="c", subcore_axis_name="s",
                                  num_cores=SC_CORES, num_subcores=SC_SUBCORES)

    @functools.partial(
        pl.kernel, mesh=mesh, out_type=jax.ShapeDtypeStruct((n_out, w), src.dtype),
        scratch_types=[pltpu.VMEM((fan, chunk), jnp.int32), pltpu.VMEM((chunk, w), src.dtype)])
    def scatter(src_hbm, idx_hbm, out_hbm, idx_v, rows_v):
        base = (lax.axis_index("s") * SC_CORES + lax.axis_index("c")) * per_worker

        @pl.loop(0, per_worker // chunk)
        def _(c):
            off = base + c * chunk
            pltpu.sync_copy(idx_hbm.at[:, pl.ds(off, chunk)], idx_v)
            pltpu.sync_copy(src_hbm.at[pl.ds(off, chunk)], rows_v)
            for j in range(fan):
                pltpu.sync_copy(rows_v, out_hbm.at[idx_v.at[j]])

    return scatter(src, idx_t)


def _combine_kernel(x2_ref, aux_ref, yg_ref, o_ref):
    x2 = x2_ref[...]
    aux = aux_ref[...]
    half = x2.shape[1] // 2
    out_lo, out_hi = x2[:, :half], x2[:, half:]
    for k in range(TOP_K):
        lo, hi = _unpack_bf16_pairs(yg_ref[k])
        wk = aux[:, TOP_K + k:TOP_K + k + 1]
        out_lo = out_lo + wk * lo
        out_hi = out_hi + wk * hi
    o_ref[:, :half] = out_lo
    o_ref[:, half:] = out_hi


def _combine(acc, aux, yg, first_row, tm):
    t, d = acc.shape
    tc = yg.shape[1]
    assert tc % tm == 0 and first_row % tm == 0
    off = first_row // tm
    return pl.pallas_call(
        _combine_kernel,
        grid=(tc // tm,),
        in_specs=[pl.BlockSpec((tm, d), lambda i: (off + i, 0)),
                  pl.BlockSpec((tm, LANES), lambda i: (off + i, 0)),
                  pl.BlockSpec((TOP_K, tm, d // 2), lambda i: (0, i, 0))],
        out_specs=pl.BlockSpec((tm, d), lambda i: (off + i, 0)),
        out_shape=jax.ShapeDtypeStruct((t, d), F32),
        input_output_aliases={0: 0},
        compiler_params=_params("parallel"),
    )(acc, aux, yg)


def _layer(x, mem, g_attn, w_in, moba_q_gain, moba_k_gain, ret_out_gain, g_mem, w_mem_kv,
           mem_q_gain, mem_k_gain, w_out, g_ffn, w_router, b_router, w1, b1, w2, b2):
    batch, seq, d = x.shape
    t = batch * seq
    x2d = x.reshape(t, d)
    tm = min(512, t)

    proj = _rms_proj(x2d, g_attn, w_in.astype(BF16), tm)
    oa = _moba(proj, moba_q_gain, moba_k_gain, batch, seq)
    orr = _retention(proj, ret_out_gain, batch, seq)
    km, vm = _mem_kv(mem, g_mem, w_mem_kv, mem_k_gain)
    om = _mem_attn(proj, km, vm, mem_q_gain, mem_k_gain, batch, seq, min(512, seq))
    x2, h2, aux, meta, counts = _out_router(oa, orr, om, x2d, w_out, g_ffn, w_router, b_router, tm)

    eidx = meta[0:TOP_K].astype(jnp.int32)
    rank = meta[2 * TOP_K:3 * TOP_K].astype(jnp.int32)
    cnt = counts[:, 0].astype(jnp.int32)
    n_blk = (cnt + MOE_BLOCK - 1) // MOE_BLOCK
    blk_ends = jnp.cumsum(n_blk)
    first_blk = jnp.concatenate([jnp.zeros((1,), jnp.int32), blk_ends]).astype(jnp.int32)
    pstarts = first_blk[:N_EXPERTS] * MOE_BLOCK
    hit = eidx[None] == jnp.arange(N_EXPERTS, dtype=jnp.int32)[:, None, None]
    dest_t = jnp.sum(jnp.where(hit, pstarts[:, None, None], 0), axis=0) + rank
    a = t * TOP_K
    p_rows = -(-(a + N_EXPERTS * (MOE_BLOCK - 1)) // MOE_BLOCK) * MOE_BLOCK
    xs = _sc_scatter_rows(h2, dest_t, p_rows)

    d_ff = w2.shape[1]
    b1g = b1[:, 0::2].reshape(N_EXPERTS, 1, d_ff)
    b1l = b1[:, 1::2].reshape(N_EXPERTS, 1, d_ff)
    ys = _moe_experts(xs, first_blk, n_blk.astype(jnp.int32), cnt, w1, b1g, b1l, w2,
                      b2.reshape(N_EXPERTS, 1, d))
    n_chunks = COMBINE_CHUNKS if t % (COMBINE_CHUNKS * SC_CORES * SC_SUBCORES * SC_GATHER_ROWS) == 0 else 1
    tc = t // n_chunks
    out = x2
    for c in range(n_chunks):
        idx = dest_t[:, c * tc:(c + 1) * tc].reshape(TOP_K * tc)
        yg = _sc_gather_rows(ys, idx).reshape(TOP_K, tc, d // 2)
        out = _combine(out, aux, yg, c * tc, min(256, tc))
    return out.reshape(batch, seq, d)


def kernel(x, mem, g_attn, w_in, moba_q_gain, moba_k_gain, ret_out_gain, g_mem, w_mem_kv,
           mem_q_gain, mem_k_gain, w_out, g_ffn, w_router, b_router, w1, b1, w2, b2):
    for l in range(g_attn.shape[0]):
        x = _layer(x, mem, g_attn[l], w_in[l], moba_q_gain[l], moba_k_gain[l], ret_out_gain[l],
                   g_mem[l], w_mem_kv[l], mem_q_gain[l], mem_k_gain[l], w_out[l], g_ffn[l],
                   w_router[l], b_router[l], w1[l], b1[l], w2[l], b2[l])
    return x
```
